```python
import jax, jax.numpy as jnp
from jax import lax
import numpy as np

D_MODEL = 1024
BATCH = 8
SEQ = 8192
DEPTH = 1

PLE_DIM = 256
D_FF = 2816
SB_HEADS = 8
SB_HEAD_DIM = 64
SB_BLOCK = 128
GDN_HEADS = 8
GDN_DK = 64
GDN_DV = 64
GDN_CHUNK = 64
CONV_K = 4
SB_WIDTH = SB_HEADS * SB_HEAD_DIM
GDN_KW = GDN_HEADS * GDN_DK
GDN_VW = GDN_HEADS * GDN_DV
GDN_CONV_CH = 2 * GDN_KW + GDN_VW
SPLITS = (SB_WIDTH, SB_WIDTH, SB_WIDTH, GDN_KW, GDN_KW, GDN_VW, GDN_VW, GDN_HEADS, GDN_HEADS, D_MODEL, D_MODEL)
N_IN = 3 * SB_WIDTH + 2 * GDN_KW + 2 * GDN_VW + 2 * GDN_HEADS + 2 * D_MODEL
DEEPNORM_ALPHA = (2 * DEPTH) ** 0.25
DEEPNORM_BETA = (8 * DEPTH) ** -0.25
LN_EPS = 1e-5
RMS_EPS = 1e-6

kernel_name = "hybrid_stickbreak_gdn_macaron_deepnorm"


def layer_norm(x, g, b):
    x32 = x.astype(jnp.float32)
    mu = jnp.mean(x32, axis=-1, keepdims=True)
    xc = x32 - mu
    var = jnp.mean(xc * xc, axis=-1, keepdims=True)
    return (xc * lax.rsqrt(var + LN_EPS) * g.astype(jnp.float32) + b.astype(jnp.float32)).astype(x.dtype)


def swiglu(h, w_in, w_out):
    gate, up = jnp.split(h @ w_in, 2, axis=-1)
    return (jax.nn.silu(gate) * up) @ w_out


def causal_depthwise_conv(x, w):
    ch = x.shape[-1]
    return lax.conv_general_dilated(x, w[:, None, :].astype(x.dtype), window_strides=(1,), padding=[(CONV_K - 1, 0)], dimension_numbers=("NWC", "WIO", "NWC"), feature_group_count=ch)


def l2norm(t):
    return t * lax.rsqrt(jnp.sum(t * t, axis=-1, keepdims=True) + RMS_EPS)


def stick_breaking_attention(q, k, v):
    S = q.shape[1]
    scale = q.shape[-1] ** -0.5
    outs = []
    for blk in range(S // SB_BLOCK):
        t0 = blk * SB_BLOCK
        t1 = t0 + SB_BLOCK
        z = jnp.einsum("bqhd,bkhd->bhqk", q[:, t0:t1], k[:, :t1]).astype(jnp.float32) * scale
        qpos = t0 + jnp.arange(SB_BLOCK)[:, None]
        kpos = jnp.arange(t1)[None, :]
        mask = kpos < qpos
        log_fail = jnp.where(mask, -jax.nn.softplus(z), 0.0)
        after = lax.cumsum(log_fail, axis=3, reverse=True) - log_fail
        weights = jnp.where(mask, jnp.exp(jax.nn.log_sigmoid(z) + after), 0.0)
        outs.append(jnp.einsum("bhqk,bkhd->bqhd", weights.astype(v.dtype), v[:, :t1]))
    return jnp.concatenate(outs, axis=1)


def gated_delta_rule(q, k, v, beta, g):
    B, S, H, dk = q.shape
    dv = v.shape[-1]
    C = GDN_CHUNK
    n = S // C
    q = l2norm(q) * dk ** -0.5
    k = l2norm(k)

    def chunks(t):
        return t.reshape(B, n, C, H, -1).transpose(0, 3, 1, 2, 4)

    q, k, v = chunks(q), chunks(k), chunks(v)
    beta = beta.reshape(B, n, C, H).transpose(0, 3, 1, 2)
    g = jnp.cumsum(g.reshape(B, n, C, H).transpose(0, 3, 1, 2), axis=-1)
    causal = jnp.tril(jnp.ones((C, C), dtype=bool))
    strict = jnp.tril(jnp.ones((C, C), dtype=bool), k=-1)
    decay = jnp.exp(jnp.where(causal, g[..., :, None] - g[..., None, :], -jnp.inf))
    kk = jnp.einsum("bhncd,bhnmd->bhncm", k, k)
    lower = jnp.where(strict, beta[..., :, None] * kk * decay, 0.0)
    rhs = jnp.concatenate([v * beta[..., None], k * (beta * jnp.exp(g))[..., None]], axis=-1)
    sol = lax.linalg.triangular_solve(lower, rhs, left_side=True, lower=True, unit_diagonal=True)
    u, w = sol[..., :dv], sol[..., dv:]
    qk = jnp.where(causal, jnp.einsum("bhncd,bhnmd->bhncm", q, k) * decay, 0.0)
    g_last = g[..., -1:]
    q_dec = q * jnp.exp(g)[..., None]
    k_dec = k * jnp.exp(g_last - g)[..., None]
    chunk_decay = jnp.exp(g_last[..., 0])
    xs = (jnp.moveaxis(qk, 2, 0), jnp.moveaxis(u, 2, 0), jnp.moveaxis(w, 2, 0), jnp.moveaxis(q_dec, 2, 0), jnp.moveaxis(k_dec, 2, 0), jnp.moveaxis(chunk_decay, 2, 0))

    def step(state, inp):
        qk_c, u_c, w_c, qd_c, kd_c, dec_c = inp
        v_new = u_c - jnp.einsum("bhck,bhkv->bhcv", w_c, state)
        o = jnp.einsum("bhck,bhkv->bhcv", qd_c, state) + jnp.einsum("bhcm,bhmv->bhcv", qk_c, v_new)
        state = state * dec_c[..., None, None] + jnp.einsum("bhck,bhcv->bhkv", kd_c, v_new)
        return state, o

    state0 = jnp.zeros((B, H, dk, dv), jnp.float32)
    _, o = lax.scan(step, state0, xs)
    return o.transpose(1, 0, 3, 2, 4).reshape(B, S, H, dv)


def hybrid_mixer(h, w_in, b_gate, conv_w, a_log, dt_bias, gdn_norm_w, w_branch_sb, w_branch_gdn, w_mix_out):
    B, S, _ = h.shape
    idx = np.cumsum(np.array(SPLITS))[:-1].tolist()
    sb_q, sb_k, sb_v, gq, gk, gv, gz, gb, ga, gate_sb, gate_gdn = jnp.split(h @ w_in, idx, axis=-1)
    y_sb = stick_breaking_attention(sb_q.reshape(B, S, SB_HEADS, SB_HEAD_DIM), sb_k.reshape(B, S, SB_HEADS, SB_HEAD_DIM), sb_v.reshape(B, S, SB_HEADS, SB_HEAD_DIM))
    y_sb = y_sb.reshape(B, S, SB_WIDTH) @ w_branch_sb
    qkv = jax.nn.silu(causal_depthwise_conv(jnp.concatenate([gq, gk, gv], axis=-1), conv_w))
    cq, ck, cv = jnp.split(qkv.astype(jnp.float32), [GDN_KW, 2 * GDN_KW], axis=-1)
    beta = jax.nn.sigmoid(gb.astype(jnp.float32))
    g = -jnp.exp(a_log.astype(jnp.float32)) * jax.nn.softplus(ga.astype(jnp.float32) + dt_bias.astype(jnp.float32))
    o = gated_delta_rule(cq.reshape(B, S, GDN_HEADS, GDN_DK), ck.reshape(B, S, GDN_HEADS, GDN_DK), cv.reshape(B, S, GDN_HEADS, GDN_DV), beta, g)
    o = o * lax.rsqrt(jnp.mean(o * o, axis=-1, keepdims=True) + RMS_EPS) * gdn_norm_w.astype(jnp.float32)
    o = o * jax.nn.silu(gz.astype(jnp.float32).reshape(B, S, GDN_HEADS, GDN_DV))
    y_gdn = o.reshape(B, S, GDN_VW).astype(h.dtype) @ w_branch_gdn
    g_sb = jax.nn.sigmoid(gate_sb + b_gate[:D_MODEL])
    g_gdn = jax.nn.sigmoid(gate_gdn + b_gate[D_MODEL:])
    return (g_sb * y_sb + g_gdn * y_gdn) @ w_mix_out


def _fwd_setup_inputs(seed: int = 0) -> dict:
    key = jax.random.key(seed)
    ks = jax.random.split(key, 32)
    f32 = jnp.float32

    def nrm(k, shape, scale):
        return jax.random.normal(k, shape, f32) * scale

    dt = jnp.exp(jax.random.uniform(ks[14], (DEPTH, GDN_HEADS), f32, minval=np.log(1e-3), maxval=np.log(0.1)))
    return {
        "x": nrm(ks[0], (BATCH, SEQ, D_MODEL), 1.0),
        "p": nrm(ks[1], (DEPTH, BATCH, SEQ, PLE_DIM), 1.0),
        "ffn1_w_in": nrm(ks[2], (DEPTH, D_MODEL, 2 * D_FF), D_MODEL ** -0.5),
        "ffn1_w_out": nrm(ks[3], (DEPTH, D_FF, D_MODEL), D_FF ** -0.5 * DEEPNORM_BETA),
        "ln1_g": 1.0 + nrm(ks[4], (DEPTH, D_MODEL), 0.02),
        "ln1_b": nrm(ks[5], (DEPTH, D_MODEL), 0.02),
        "w_mix_in": nrm(ks[6], (DEPTH, D_MODEL, N_IN), D_MODEL ** -0.5),
        "b_gate": nrm(ks[7], (DEPTH, 2 * D_MODEL), 0.1),
        "conv_w": nrm(ks[8], (DEPTH, CONV_K, GDN_CONV_CH), CONV_K ** -0.5),
        "a_log": jnp.log(jax.random.uniform(ks[9], (DEPTH, GDN_HEADS), f32, minval=1.0, maxval=16.0)),
        "dt_bias": dt + jnp.log(-jnp.expm1(-dt)),
        "gdn_norm_w": 1.0 + nrm(ks[10], (DEPTH, GDN_DV), 0.02),
        "w_branch_sb": nrm(ks[11], (DEPTH, SB_WIDTH, D_MODEL), SB_WIDTH ** -0.5),
        "w_branch_gdn": nrm(ks[12], (DEPTH, GDN_VW, D_MODEL), GDN_VW ** -0.5),
        "w_mix_out": nrm(ks[13], (DEPTH, D_MODEL, D_MODEL), D_MODEL ** -0.5 * DEEPNORM_BETA),
        "ln2_g": 1.0 + nrm(ks[15], (DEPTH, D_MODEL), 0.02),
        "ln2_b": nrm(ks[16], (DEPTH, D_MODEL), 0.02),
        "ffn2_w_in": nrm(ks[17], (DEPTH, D_MODEL, 2 * D_FF), D_MODEL ** -0.5),
        "ffn2_w_out": nrm(ks[18], (DEPTH, D_FF, D_MODEL), D_FF ** -0.5 * DEEPNORM_BETA),
        "ln3_g": 1.0 + nrm(ks[19], (DEPTH, D_MODEL), 0.02),
        "ln3_b": nrm(ks[20], (DEPTH, D_MODEL), 0.02),
        "w_ple_gate": nrm(ks[21], (DEPTH, D_MODEL, D_MODEL), D_MODEL ** -0.5),
        "b_ple_gate": nrm(ks[22], (DEPTH, D_MODEL), 0.1),
        "w_ple": nrm(ks[23], (DEPTH, PLE_DIM, D_MODEL), PLE_DIM ** -0.5 * DEEPNORM_BETA),
        "ln4_g": 1.0 + nrm(ks[24], (DEPTH, D_MODEL), 0.02),
        "ln4_b": nrm(ks[25], (DEPTH, D_MODEL), 0.02),
    }


def _fwd_reference(x, p, ffn1_w_in, ffn1_w_out, ln1_g, ln1_b, w_mix_in, b_gate, conv_w, a_log, dt_bias, gdn_norm_w, w_branch_sb, w_branch_gdn, w_mix_out, ln2_g, ln2_b, ffn2_w_in, ffn2_w_out, ln3_g, ln3_b, w_ple_gate, b_ple_gate, w_ple, ln4_g, ln4_b):
    h = x
    for i in range(DEPTH):
        h = layer_norm(DEEPNORM_ALPHA * h + 0.5 * swiglu(h, ffn1_w_in[i], ffn1_w_out[i]), ln1_g[i], ln1_b[i])
        mix = hybrid_mixer(h, w_mix_in[i], b_gate[i], conv_w[i], a_log[i], dt_bias[i], gdn_norm_w[i], w_branch_sb[i], w_branch_gdn[i], w_mix_out[i])
        h = layer_norm(DEEPNORM_ALPHA * h + mix, ln2_g[i], ln2_b[i])
        h = layer_norm(DEEPNORM_ALPHA * h + 0.5 * swiglu(h, ffn2_w_in[i], ffn2_w_out[i]), ln3_g[i], ln3_b[i])
        ple = jax.nn.sigmoid(h @ w_ple_gate[i] + b_ple_gate[i]) * (p[i] @ w_ple[i])
        h = layer_norm(DEEPNORM_ALPHA * h + ple, ln4_g[i], ln4_b[i])
    return h


import jax as _jax
import jax.numpy as _jnp

TWIN_FORMAT = 'train_step'
FWD_PARAMS = ['x', 'p', 'ffn1_w_in', 'ffn1_w_out', 'ln1_g', 'ln1_b', 'w_mix_in', 'b_gate', 'conv_w', 'a_log', 'dt_bias', 'gdn_norm_w', 'w_branch_sb', 'w_branch_gdn', 'w_mix_out', 'ln2_g', 'ln2_b', 'ffn2_w_in', 'ffn2_w_out', 'ln3_g', 'ln3_b', 'w_ple_gate', 'b_ple_gate', 'w_ple', 'ln4_g', 'ln4_b']
TWIN_WEIGHTS = ['ffn1_w_in', 'ffn1_w_out', 'ln1_g', 'ln1_b', 'w_mix_in', 'b_gate', 'conv_w', 'a_log', 'dt_bias', 'gdn_norm_w', 'w_branch_sb', 'w_branch_gdn', 'w_mix_out', 'ln2_g', 'ln2_b', 'ffn2_w_in', 'ffn2_w_out', 'ln3_g', 'ln3_b', 'w_ple_gate', 'b_ple_gate', 'w_ple', 'ln4_g', 'ln4_b']
TWIN_DIFF_INPUT = 'x'
TWIN_INPUTS = ['x', 'p', 'ffn1_w_in', 'ffn1_w_out', 'ln1_g', 'ln1_b', 'w_mix_in', 'b_gate', 'conv_w', 'a_log', 'dt_bias', 'gdn_norm_w', 'w_branch_sb', 'w_branch_gdn', 'w_mix_out', 'ln2_g', 'ln2_b', 'ffn2_w_in', 'ffn2_w_out', 'ln3_g', 'ln3_b', 'w_ple_gate', 'b_ple_gate', 'w_ple', 'ln4_g', 'ln4_b', 'loss_target', 'm_ffn1_w_in', 'm_ffn1_w_out', 'm_ln1_g', 'm_ln1_b', 'm_w_mix_in', 'm_b_gate', 'm_conv_w', 'm_a_log', 'm_dt_bias', 'm_gdn_norm_w', 'm_w_branch_sb', 'm_w_branch_gdn', 'm_w_mix_out', 'm_ln2_g', 'm_ln2_b', 'm_ffn2_w_in', 'm_ffn2_w_out', 'm_ln3_g', 'm_ln3_b', 'm_w_ple_gate', 'm_b_ple_gate', 'm_w_ple', 'm_ln4_g', 'm_ln4_b', 'v_ffn1_w_in', 'v_ffn1_w_out', 'v_ln1_g', 'v_ln1_b', 'v_w_mix_in', 'v_b_gate', 'v_conv_w', 'v_a_log', 'v_dt_bias', 'v_gdn_norm_w', 'v_w_branch_sb', 'v_w_branch_gdn', 'v_w_mix_out', 'v_ln2_g', 'v_ln2_b', 'v_ffn2_w_in', 'v_ffn2_w_out', 'v_ln3_g', 'v_ln3_b', 'v_w_ple_gate', 'v_b_ple_gate', 'v_w_ple', 'v_ln4_g', 'v_ln4_b']
TWIN_OUTPUTS = ['loss', 'grad_x', 'grad_ffn1_w_in', 'grad_ffn1_w_out', 'grad_ln1_g', 'grad_ln1_b', 'grad_w_mix_in', 'grad_b_gate', 'grad_conv_w', 'grad_a_log', 'grad_dt_bias', 'grad_gdn_norm_w', 'grad_w_branch_sb', 'grad_w_branch_gdn', 'grad_w_mix_out', 'grad_ln2_g', 'grad_ln2_b', 'grad_ffn2_w_in', 'grad_ffn2_w_out', 'grad_ln3_g', 'grad_ln3_b', 'grad_w_ple_gate', 'grad_b_ple_gate', 'grad_w_ple', 'grad_ln4_g', 'grad_ln4_b', 'delta_ffn1_w_in', 'delta_ffn1_w_out', 'delta_ln1_g', 'delta_ln1_b', 'delta_w_mix_in', 'delta_b_gate', 'delta_conv_w', 'delta_a_log', 'delta_dt_bias', 'delta_gdn_norm_w', 'delta_w_branch_sb', 'delta_w_branch_gdn', 'delta_w_mix_out', 'delta_ln2_g', 'delta_ln2_b', 'delta_ffn2_w_in', 'delta_ffn2_w_out', 'delta_ln3_g', 'delta_ln3_b', 'delta_w_ple_gate', 'delta_b_ple_gate', 'delta_w_ple', 'delta_ln4_g', 'delta_ln4_b', 'new_m_ffn1_w_in', 'new_m_ffn1_w_out', 'new_m_ln1_g', 'new_m_ln1_b', 'new_m_w_mix_in', 'new_m_b_gate', 'new_m_conv_w', 'new_m_a_log', 'new_m_dt_bias', 'new_m_gdn_norm_w', 'new_m_w_branch_sb', 'new_m_w_branch_gdn', 'new_m_w_mix_out', 'new_m_ln2_g', 'new_m_ln2_b', 'new_m_ffn2_w_in', 'new_m_ffn2_w_out', 'new_m_ln3_g', 'new_m_ln3_b', 'new_m_w_ple_gate', 'new_m_b_ple_gate', 'new_m_w_ple', 'new_m_ln4_g', 'new_m_ln4_b', 'new_v_ffn1_w_in', 'new_v_ffn1_w_out', 'new_v_ln1_g', 'new_v_ln1_b', 'new_v_w_mix_in', 'new_v_b_gate', 'new_v_conv_w', 'new_v_a_log', 'new_v_dt_bias', 'new_v_gdn_norm_w', 'new_v_w_branch_sb', 'new_v_w_branch_gdn', 'new_v_w_mix_out', 'new_v_ln2_g', 'new_v_ln2_b', 'new_v_ffn2_w_in', 'new_v_ffn2_w_out', 'new_v_ln3_g', 'new_v_ln3_b', 'new_v_w_ple_gate', 'new_v_b_ple_gate', 'new_v_w_ple', 'new_v_ln4_g', 'new_v_ln4_b']
TWIN_LEAF_KINDS = {'loss': 'loss', 'grad_x': 'grad_x', 'grad_ffn1_w_in': 'grad_w', 'grad_ffn1_w_out': 'grad_w', 'grad_ln1_g': 'grad_w', 'grad_ln1_b': 'grad_w', 'grad_w_mix_in': 'grad_w', 'grad_b_gate': 'grad_w', 'grad_conv_w': 'grad_w', 'grad_a_log': 'grad_w', 'grad_dt_bias': 'grad_w', 'grad_gdn_norm_w': 'grad_w', 'grad_w_branch_sb': 'grad_w', 'grad_w_branch_gdn': 'grad_w', 'grad_w_mix_out': 'grad_w', 'grad_ln2_g': 'grad_w', 'grad_ln2_b': 'grad_w', 'grad_ffn2_w_in': 'grad_w', 'grad_ffn2_w_out': 'grad_w', 'grad_ln3_g': 'grad_w', 'grad_ln3_b': 'grad_w', 'grad_w_ple_gate': 'grad_w', 'grad_b_ple_gate': 'grad_w', 'grad_w_ple': 'grad_w', 'grad_ln4_g': 'grad_w', 'grad_ln4_b': 'grad_w', 'delta_ffn1_w_in': 'delta_w', 'delta_ffn1_w_out': 'delta_w', 'delta_ln1_g': 'delta_w', 'delta_ln1_b': 'delta_w', 'delta_w_mix_in': 'delta_w', 'delta_b_gate': 'delta_w', 'delta_conv_w': 'delta_w', 'delta_a_log': 'delta_w', 'delta_dt_bias': 'delta_w', 'delta_gdn_norm_w': 'delta_w', 'delta_w_branch_sb': 'delta_w', 'delta_w_branch_gdn': 'delta_w', 'delta_w_mix_out': 'delta_w', 'delta_ln2_g': 'delta_w', 'delta_ln2_b': 'delta_w', 'delta_ffn2_w_in': 'delta_w', 'delta_ffn2_w_out': 'delta_w', 'delta_ln3_g': 'delta_w', 'delta_ln3_b': 'delta_w', 'delta_w_ple_gate': 'delta_w', 'delta_b_ple_gate': 'delta_w', 'delta_w_ple': 'delta_w', 'delta_ln4_g': 'delta_w', 'delta_ln4_b': 'delta_w', 'new_m_ffn1_w_in': 'new_m', 'new_m_ffn1_w_out': 'new_m', 'new_m_ln1_g': 'new_m', 'new_m_ln1_b': 'new_m', 'new_m_w_mix_in': 'new_m', 'new_m_b_gate': 'new_m', 'new_m_conv_w': 'new_m', 'new_m_a_log': 'new_m', 'new_m_dt_bias': 'new_m', 'new_m_gdn_norm_w': 'new_m', 'new_m_w_branch_sb': 'new_m', 'new_m_w_branch_gdn': 'new_m', 'new_m_w_mix_out': 'new_m', 'new_m_ln2_g': 'new_m', 'new_m_ln2_b': 'new_m', 'new_m_ffn2_w_in': 'new_m', 'new_m_ffn2_w_out': 'new_m', 'new_m_ln3_g': 'new_m', 'new_m_ln3_b': 'new_m', 'new_m_w_ple_gate': 'new_m', 'new_m_b_ple_gate': 'new_m', 'new_m_w_ple': 'new_m', 'new_m_ln4_g': 'new_m', 'new_m_ln4_b': 'new_m', 'new_v_ffn1_w_in': 'new_v', 'new_v_ffn1_w_out': 'new_v', 'new_v_ln1_g': 'new_v', 'new_v_ln1_b': 'new_v', 'new_v_w_mix_in': 'new_v', 'new_v_b_gate': 'new_v', 'new_v_conv_w': 'new_v', 'new_v_a_log': 'new_v', 'new_v_dt_bias': 'new_v', 'new_v_gdn_norm_w': 'new_v', 'new_v_w_branch_sb': 'new_v', 'new_v_w_branch_gdn': 'new_v', 'new_v_w_mix_out': 'new_v', 'new_v_ln2_g': 'new_v', 'new_v_ln2_b': 'new_v', 'new_v_ffn2_w_in': 'new_v', 'new_v_ffn2_w_out': 'new_v', 'new_v_ln3_g': 'new_v', 'new_v_ln3_b': 'new_v', 'new_v_w_ple_gate': 'new_v', 'new_v_b_ple_gate': 'new_v', 'new_v_w_ple': 'new_v', 'new_v_ln4_g': 'new_v', 'new_v_ln4_b': 'new_v'}


def _forward(args):
    return _fwd_reference(*[args[k] for k in FWD_PARAMS])


def _output_shape():
    def fwd():
        inp = _fwd_setup_inputs(0)
        return _fwd_reference(*[inp[k] for k in FWD_PARAMS])
    out = _jax.eval_shape(fwd)
    return out.shape, out.dtype

N_MICROBATCH = 1
ADAM_LR = 0.001
ADAM_B1 = 0.9
ADAM_B2 = 0.999
ADAM_EPS = 1e-08
ADAM_WD = 0.01
ADAM_STEP = 10
PER_EXAMPLE_BATCH_AXIS = {'x': 0, 'p': 1, 'loss_target': 0}
SHARED_INPUTS = []
_WEIGHT_DTYPES = {'ffn1_w_in': _jnp.float32, 'ffn1_w_out': _jnp.float32, 'ln1_g': _jnp.float32, 'ln1_b': _jnp.float32, 'w_mix_in': _jnp.float32, 'b_gate': _jnp.float32, 'conv_w': _jnp.float32, 'a_log': _jnp.float32, 'dt_bias': _jnp.float32, 'gdn_norm_w': _jnp.float32, 'w_branch_sb': _jnp.float32, 'w_branch_gdn': _jnp.float32, 'w_mix_out': _jnp.float32, 'ln2_g': _jnp.float32, 'ln2_b': _jnp.float32, 'ffn2_w_in': _jnp.float32, 'ffn2_w_out': _jnp.float32, 'ln3_g': _jnp.float32, 'ln3_b': _jnp.float32, 'w_ple_gate': _jnp.float32, 'b_ple_gate': _jnp.float32, 'w_ple': _jnp.float32, 'ln4_g': _jnp.float32, 'ln4_b': _jnp.float32}
MOMENT_SCALE = {'ffn1_w_in': 2.316012e-02, 'ffn1_w_out': 6.354659e-02, 'ln1_g': 1.858962e+00, 'ln1_b': 1.034207e+00, 'w_mix_in': 3.894500e-02, 'b_gate': 1.692086e-02, 'conv_w': 5.249862e-02, 'a_log': 1.346013e-01, 'dt_bias': 1.334954e-01, 'gdn_norm_w': 1.818566e-01, 'w_branch_sb': 4.169611e-02, 'w_branch_gdn': 4.240962e-02, 'w_mix_out': 1.007387e-01, 'ln2_g': 2.017076e+00, 'ln2_b': 1.001725e+00, 'ffn2_w_in': 2.200150e-02, 'ffn2_w_out': 6.049586e-02, 'ln3_g': 2.073310e+00, 'ln3_b': 1.009568e+00, 'w_ple_gate': 2.558623e-02, 'b_ple_gate': 4.528967e-02, 'w_ple': 1.102321e-01, 'ln4_g': 6.415183e+01, 'ln4_b': 2.662437e+00}


def _to_microbatches(a, axis):
    t = _jnp.moveaxis(a, axis, 0)
    t = t.reshape((N_MICROBATCH, t.shape[0] // N_MICROBATCH) + t.shape[1:])
    return _jnp.moveaxis(t, 1, axis + 1)


def setup_inputs(seed: int = 0) -> dict:
    inp = _fwd_setup_inputs(seed)
    key = _jax.random.fold_in(_jax.random.key(seed), 7919)
    shape, _ = _output_shape()
    out = dict(inp)
    out["loss_target"] = _jax.random.normal(_jax.random.fold_in(key, 0), shape, _jnp.float32)
    for i, name in enumerate(TWIN_WEIGHTS):
        w = inp[name].astype(_jnp.float32)
        if MOMENT_SCALE is None:
            s = _jnp.sqrt(_jnp.mean(_jnp.square(w)) + 1e-30)
        else:
            s = MOMENT_SCALE[name]
        km, kv = _jax.random.split(_jax.random.fold_in(key, i + 1))
        out[name] = w
        out["m_" + name] = s * _jax.random.normal(km, w.shape, _jnp.float32)
        out["v_" + name] = (s * s) * _jax.random.uniform(kv, w.shape, _jnp.float32, 0.5, 1.5)
    if N_MICROBATCH > 1:
        for name, axis in PER_EXAMPLE_BATCH_AXIS.items():
            out[name] = _to_microbatches(out[name], axis)
    return {'x': out['x'], 'p': out['p'], 'ffn1_w_in': out['ffn1_w_in'], 'ffn1_w_out': out['ffn1_w_out'], 'ln1_g': out['ln1_g'], 'ln1_b': out['ln1_b'], 'w_mix_in': out['w_mix_in'], 'b_gate': out['b_gate'], 'conv_w': out['conv_w'], 'a_log': out['a_log'], 'dt_bias': out['dt_bias'], 'gdn_norm_w': out['gdn_norm_w'], 'w_branch_sb': out['w_branch_sb'], 'w_branch_gdn': out['w_branch_gdn'], 'w_mix_out': out['w_mix_out'], 'ln2_g': out['ln2_g'], 'ln2_b': out['ln2_b'], 'ffn2_w_in': out['ffn2_w_in'], 'ffn2_w_out': out['ffn2_w_out'], 'ln3_g': out['ln3_g'], 'ln3_b': out['ln3_b'], 'w_ple_gate': out['w_ple_gate'], 'b_ple_gate': out['b_ple_gate'], 'w_ple': out['w_ple'], 'ln4_g': out['ln4_g'], 'ln4_b': out['ln4_b'], 'loss_target': out['loss_target'], 'm_ffn1_w_in': out['m_ffn1_w_in'], 'm_ffn1_w_out': out['m_ffn1_w_out'], 'm_ln1_g': out['m_ln1_g'], 'm_ln1_b': out['m_ln1_b'], 'm_w_mix_in': out['m_w_mix_in'], 'm_b_gate': out['m_b_gate'], 'm_conv_w': out['m_conv_w'], 'm_a_log': out['m_a_log'], 'm_dt_bias': out['m_dt_bias'], 'm_gdn_norm_w': out['m_gdn_norm_w'], 'm_w_branch_sb': out['m_w_branch_sb'], 'm_w_branch_gdn': out['m_w_branch_gdn'], 'm_w_mix_out': out['m_w_mix_out'], 'm_ln2_g': out['m_ln2_g'], 'm_ln2_b': out['m_ln2_b'], 'm_ffn2_w_in': out['m_ffn2_w_in'], 'm_ffn2_w_out': out['m_ffn2_w_out'], 'm_ln3_g': out['m_ln3_g'], 'm_ln3_b': out['m_ln3_b'], 'm_w_ple_gate': out['m_w_ple_gate'], 'm_b_ple_gate': out['m_b_ple_gate'], 'm_w_ple': out['m_w_ple'], 'm_ln4_g': out['m_ln4_g'], 'm_ln4_b': out['m_ln4_b'], 'v_ffn1_w_in': out['v_ffn1_w_in'], 'v_ffn1_w_out': out['v_ffn1_w_out'], 'v_ln1_g': out['v_ln1_g'], 'v_ln1_b': out['v_ln1_b'], 'v_w_mix_in': out['v_w_mix_in'], 'v_b_gate': out['v_b_gate'], 'v_conv_w': out['v_conv_w'], 'v_a_log': out['v_a_log'], 'v_dt_bias': out['v_dt_bias'], 'v_gdn_norm_w': out['v_gdn_norm_w'], 'v_w_branch_sb': out['v_w_branch_sb'], 'v_w_branch_gdn': out['v_w_branch_gdn'], 'v_w_mix_out': out['v_w_mix_out'], 'v_ln2_g': out['v_ln2_g'], 'v_ln2_b': out['v_ln2_b'], 'v_ffn2_w_in': out['v_ffn2_w_in'], 'v_ffn2_w_out': out['v_ffn2_w_out'], 'v_ln3_g': out['v_ln3_g'], 'v_ln3_b': out['v_ln3_b'], 'v_w_ple_gate': out['v_w_ple_gate'], 'v_b_ple_gate': out['v_b_ple_gate'], 'v_w_ple': out['v_w_ple'], 'v_ln4_g': out['v_ln4_g'], 'v_ln4_b': out['v_ln4_b']}


def _loss(weights, diff, rest, loss_target):
    with _jax.named_scope("forward"):
        args = {**rest, TWIN_DIFF_INPUT: diff, **{k: w.astype(_WEIGHT_DTYPES[k]) for k, w in weights.items()}}
        y = _forward(args)
    with _jax.named_scope("loss_head"):
        err = _jnp.square(y.astype(_jnp.float32) - loss_target)
        return 0.5 * _jnp.sum(_jnp.mean(err, axis=-1)) if err.ndim else 0.5 * err


def _adamw(w, g, m, v):
    m = ADAM_B1 * m + (1.0 - ADAM_B1) * g
    v = ADAM_B2 * v + (1.0 - ADAM_B2) * _jnp.square(g)
    m_hat = m / (1.0 - ADAM_B1 ** ADAM_STEP)
    v_hat = v / (1.0 - ADAM_B2 ** ADAM_STEP)
    delta = -ADAM_LR * (m_hat / (_jnp.sqrt(v_hat) + ADAM_EPS) + ADAM_WD * w)
    return delta, m, v


def reference(x, p, ffn1_w_in, ffn1_w_out, ln1_g, ln1_b, w_mix_in, b_gate, conv_w, a_log, dt_bias, gdn_norm_w, w_branch_sb, w_branch_gdn, w_mix_out, ln2_g, ln2_b, ffn2_w_in, ffn2_w_out, ln3_g, ln3_b, w_ple_gate, b_ple_gate, w_ple, ln4_g, ln4_b, loss_target, m_ffn1_w_in, m_ffn1_w_out, m_ln1_g, m_ln1_b, m_w_mix_in, m_b_gate, m_conv_w, m_a_log, m_dt_bias, m_gdn_norm_w, m_w_branch_sb, m_w_branch_gdn, m_w_mix_out, m_ln2_g, m_ln2_b, m_ffn2_w_in, m_ffn2_w_out, m_ln3_g, m_ln3_b, m_w_ple_gate, m_b_ple_gate, m_w_ple, m_ln4_g, m_ln4_b, v_ffn1_w_in, v_ffn1_w_out, v_ln1_g, v_ln1_b, v_w_mix_in, v_b_gate, v_conv_w, v_a_log, v_dt_bias, v_gdn_norm_w, v_w_branch_sb, v_w_branch_gdn, v_w_mix_out, v_ln2_g, v_ln2_b, v_ffn2_w_in, v_ffn2_w_out, v_ln3_g, v_ln3_b, v_w_ple_gate, v_b_ple_gate, v_w_ple, v_ln4_g, v_ln4_b):
    given = dict(x=x, p=p, ffn1_w_in=ffn1_w_in, ffn1_w_out=ffn1_w_out, ln1_g=ln1_g, ln1_b=ln1_b, w_mix_in=w_mix_in, b_gate=b_gate, conv_w=conv_w, a_log=a_log, dt_bias=dt_bias, gdn_norm_w=gdn_norm_w, w_branch_sb=w_branch_sb, w_branch_gdn=w_branch_gdn, w_mix_out=w_mix_out, ln2_g=ln2_g, ln2_b=ln2_b, ffn2_w_in=ffn2_w_in, ffn2_w_out=ffn2_w_out, ln3_g=ln3_g, ln3_b=ln3_b, w_ple_gate=w_ple_gate, b_ple_gate=b_ple_gate, w_ple=w_ple, ln4_g=ln4_g, ln4_b=ln4_b, loss_target=loss_target, m_ffn1_w_in=m_ffn1_w_in, m_ffn1_w_out=m_ffn1_w_out, m_ln1_g=m_ln1_g, m_ln1_b=m_ln1_b, m_w_mix_in=m_w_mix_in, m_b_gate=m_b_gate, m_conv_w=m_conv_w, m_a_log=m_a_log, m_dt_bias=m_dt_bias, m_gdn_norm_w=m_gdn_norm_w, m_w_branch_sb=m_w_branch_sb, m_w_branch_gdn=m_w_branch_gdn, m_w_mix_out=m_w_mix_out, m_ln2_g=m_ln2_g, m_ln2_b=m_ln2_b, m_ffn2_w_in=m_ffn2_w_in, m_ffn2_w_out=m_ffn2_w_out, m_ln3_g=m_ln3_g, m_ln3_b=m_ln3_b, m_w_ple_gate=m_w_ple_gate, m_b_ple_gate=m_b_ple_gate, m_w_ple=m_w_ple, m_ln4_g=m_ln4_g, m_ln4_b=m_ln4_b, v_ffn1_w_in=v_ffn1_w_in, v_ffn1_w_out=v_ffn1_w_out, v_ln1_g=v_ln1_g, v_ln1_b=v_ln1_b, v_w_mix_in=v_w_mix_in, v_b_gate=v_b_gate, v_conv_w=v_conv_w, v_a_log=v_a_log, v_dt_bias=v_dt_bias, v_gdn_norm_w=v_gdn_norm_w, v_w_branch_sb=v_w_branch_sb, v_w_branch_gdn=v_w_branch_gdn, v_w_mix_out=v_w_mix_out, v_ln2_g=v_ln2_g, v_ln2_b=v_ln2_b, v_ffn2_w_in=v_ffn2_w_in, v_ffn2_w_out=v_ffn2_w_out, v_ln3_g=v_ln3_g, v_ln3_b=v_ln3_b, v_w_ple_gate=v_w_ple_gate, v_b_ple_gate=v_b_ple_gate, v_w_ple=v_w_ple, v_ln4_g=v_ln4_g, v_ln4_b=v_ln4_b)
    weights = {n: given[n] for n in TWIN_WEIGHTS}
    shared = {n: given[n] for n in SHARED_INPUTS}
    per_example = {n: given[n] for n in ['x', 'p']}
    grad_fn = _jax.value_and_grad(_loss, argnums=(0, 1))

    def one_microbatch(ex, loss_target):
        ex = dict(ex)
        diff = ex.pop(TWIN_DIFF_INPUT)
        return grad_fn(weights, diff, {**shared, **ex}, loss_target)

    if N_MICROBATCH == 1:
        loss, (grad_w, grad_x) = one_microbatch(per_example, given["loss_target"])
    else:
        def body(carry, xs):
            loss_sum, grad_sum = carry
            l_k, (gw_k, gx_k) = one_microbatch(xs[0], xs[1])
            with _jax.named_scope("update"):
                return (loss_sum + l_k, _jax.tree.map(_jnp.add, grad_sum, gw_k)), gx_k

        init = (_jnp.zeros((), _jnp.float32), _jax.tree.map(_jnp.zeros_like, weights))
        (loss, grad_w), grad_x = _jax.lax.scan(body, init, (per_example, given["loss_target"]))
    with _jax.named_scope("update"):
        delta_w, new_m, new_v = {}, {}, {}
        for n in TWIN_WEIGHTS:
            delta_w[n], new_m[n], new_v[n] = _adamw(weights[n], grad_w[n], given["m_" + n], given["v_" + n])
    return (loss, grad_x, *[grad_w[n] for n in TWIN_WEIGHTS], *[delta_w[n] for n in TWIN_WEIGHTS],
            *[new_m[n] for n in TWIN_WEIGHTS], *[new_v[n] for n in TWIN_WEIGHTS])
```

```python
import functools
import math

import jax
import jax.numpy as jnp
from jax import lax
from jax.experimental import pallas as pl
from jax.experimental.pallas import tpu as pltpu

F32 = jnp.float32
BF16 = jnp.bfloat16
MESH = pl.DeviceIdType.MESH
HI = lax.Precision.HIGHEST

N_DEV = 8
D_MODEL = 1024
D_FF = 2816
PLE_DIM = 256
N_HEADS = 8
HEAD_DIM = 64
HEAD_W = N_HEADS * HEAD_DIM
GDN_CHUNK = 64
CONV_K = 4
CONV_CH = 3 * HEAD_W
ALPHA = 2.0 ** 0.25
LN_EPS = 1e-5
RMS_EPS = 1e-6
MIX_SB = 0
MIX_GDN = 3 * HEAD_W
MIX_GATES = MIX_GDN + 4 * HEAD_W
MIX_BA = MIX_GATES + 2 * D_MODEL
MIX_PAD = MIX_BA + 128
N_IN = 7 * HEAD_W + 2 * N_HEADS + 2 * D_MODEL

ADAM_LR = 0.001
ADAM_B1 = 0.9
ADAM_B2 = 0.999
ADAM_EPS = 1e-08
ADAM_WD = 0.01
ADAM_STEP = 10

SB_TQ = 256
SB_TK = 128
HALO = 8

BIG = ["ffn1_w_in", "ffn1_w_out", "w_mix_in", "conv_w", "w_branch_sb", "w_branch_gdn", "w_mix_out",
       "ffn2_w_in", "ffn2_w_out", "w_ple_gate", "w_ple"]
BIG_SHAPES = {
    "ffn1_w_in": ((D_MODEL, 2 * D_FF), 1), "ffn1_w_out": ((D_FF, D_MODEL), 0),
    "w_mix_in": ((D_MODEL, N_IN), 1), "conv_w": ((CONV_K, CONV_CH), 1),
    "w_branch_sb": ((HEAD_W, D_MODEL), 1), "w_branch_gdn": ((HEAD_W, D_MODEL), 1),
    "w_mix_out": ((D_MODEL, D_MODEL), 0),
    "ffn2_w_in": ((D_MODEL, 2 * D_FF), 1), "ffn2_w_out": ((D_FF, D_MODEL), 0),
    "w_ple_gate": ((D_MODEL, D_MODEL), 0), "w_ple": ((PLE_DIM, D_MODEL), 1),
}
SMALL = ["ln1_g", "ln1_b", "b_gate", "a_log", "dt_bias", "gdn_norm_w", "ln2_g", "ln2_b", "ln3_g", "ln3_b",
         "b_ple_gate", "ln4_g", "ln4_b"]
WEIGHTS = ["ffn1_w_in", "ffn1_w_out", "ln1_g", "ln1_b", "w_mix_in", "b_gate", "conv_w", "a_log", "dt_bias",
           "gdn_norm_w", "w_branch_sb", "w_branch_gdn", "w_mix_out", "ln2_g", "ln2_b", "ffn2_w_in", "ffn2_w_out",
           "ln3_g", "ln3_b", "w_ple_gate", "b_ple_gate", "w_ple", "ln4_g", "ln4_b"]
PACK_COLS = 1024
ADAM_ROWS = 208


def _pick(n, cands):
    for c in cands:
        if n % c == 0:
            return c
    raise ValueError(f"no tile for {n} in {cands}")


def _matmul(name, a, b, a_spec, b_spec, out_sds, out_spec, out_block, grid, nk, trans_a=False):
    dims = (((0,), (0,)), ((), ())) if trans_a else (((1,), (0,)), ((), ()))

    def body(a_ref, b_ref, o_ref, acc_ref):
        k = pl.program_id(2)

        @pl.when(k == 0)
        def _():
            acc_ref[...] = jnp.zeros_like(acc_ref)

        acc_ref[...] += lax.dot_general(a_ref[...], b_ref[...], dims, preferred_element_type=F32)

        @pl.when(k == nk - 1)
        def _():
            o_ref[...] = acc_ref[...].astype(o_ref.dtype)

    return pl.pallas_call(
        body, name=name, grid=grid, in_specs=[a_spec, b_spec], out_specs=out_spec, out_shape=out_sds,
        scratch_shapes=[pltpu.VMEM(out_block, F32)],
        compiler_params=pltpu.CompilerParams(dimension_semantics=("parallel", "parallel", "arbitrary")),
    )(a, b)


def _mm(name, a, w, out_dtype, n=None, col_off=0):
    M, K = a.shape
    n = w.shape[1] if n is None else n
    tm = _pick(M, (512, 256, 128))
    tn = _pick(n, (512, 384, 256, 128))
    tk = _pick(K, (1024, 1408, 1152, 512, 256))
    assert col_off % tn == 0
    cb, nk = col_off // tn, K // tk
    return _matmul(
        name, a, w, pl.BlockSpec((tm, tk), lambda i, j, k: (i, k)), pl.BlockSpec((tk, tn), lambda i, j, k: (k, j + cb)),
        jax.ShapeDtypeStruct((M, n), out_dtype), pl.BlockSpec((tm, tn), lambda i, j, k: (i, j)), (tm, tn),
        (M // tm, n // tn, nk), nk)


def _mm_gu(name, a3, w, out_dtype):
    _, M, Fh = a3.shape
    N = w.shape[1]
    tm = _pick(M, (512, 256, 128))
    tn = _pick(N, (512, 256, 128))
    tk = _pick(Fh, (1408, 512, 256, 128))
    nkf = Fh // tk
    return _matmul(
        name, a3, w, pl.BlockSpec((None, tm, tk), lambda i, j, k: (k // nkf, i, k % nkf)),
        pl.BlockSpec((tk, tn), lambda i, j, k: (k, j)),
        jax.ShapeDtypeStruct((M, N), out_dtype), pl.BlockSpec((tm, tn), lambda i, j, k: (i, j)), (tm, tn),
        (M // tm, N // tn, 2 * nkf), 2 * nkf)


def _mm_tn(name, a, b):
    S, M = a.shape
    N = b.shape[1]
    tm = _pick(M, (512, 256, 128))
    tn = _pick(N, (512, 640, 256, 128))
    tk = _pick(S, (1024, 512, 256, 128))
    nk = S // tk
    return _matmul(
        name, a, b, pl.BlockSpec((tk, tm), lambda i, j, k: (k, i)), pl.BlockSpec((tk, tn), lambda i, j, k: (k, j)),
        jax.ShapeDtypeStruct((M, N), F32), pl.BlockSpec((tm, tn), lambda i, j, k: (i, j)), (tm, tn),
        (M // tm, N // tn, nk), nk, trans_a=True)


def _mm_tn_gu(name, a, b3):
    S, M = a.shape
    Fh = b3.shape[2]
    tm = _pick(M, (512, 256, 128))
    tn = _pick(Fh, (256, 128))
    tk = _pick(S, (1024, 512, 256, 128))
    nf, nk = Fh // tn, S // tk
    return _matmul(
        name, a, b3, pl.BlockSpec((tk, tm), lambda i, j, k: (k, i)),
        pl.BlockSpec((None, tk, tn), lambda i, j, k: (j // nf, k, j % nf)),
        jax.ShapeDtypeStruct((M, 2 * Fh), F32), pl.BlockSpec((tm, tn), lambda i, j, k: (i, j)), (tm, tn),
        (M // tm, 2 * nf, nk), nk, trans_a=True)


def _sigmoid(z):
    return 1.0 / (1.0 + jnp.exp(-z))


def _ffn_up(name, xb, w_in):
    S, D = xb.shape
    Fh = w_in.shape[1] // 2
    ts = _pick(S, (512, 256, 128))
    tf = _pick(Fh, (256, 128))
    nf = Fh // tf

    def body(x_ref, wg_ref, wu_ref, a_ref, gu_ref):
        x = x_ref[...]
        g = jnp.dot(x, wg_ref[...], preferred_element_type=F32)
        u = jnp.dot(x, wu_ref[...], preferred_element_type=F32)
        a_ref[...] = (g * _sigmoid(g) * u).astype(a_ref.dtype)
        gu_ref[0] = g.astype(gu_ref.dtype)
        gu_ref[1] = u.astype(gu_ref.dtype)

    return pl.pallas_call(
        body, name=name, grid=(S // ts, nf),
        in_specs=[pl.BlockSpec((ts, D), lambda i, j: (i, 0)), pl.BlockSpec((D, tf), lambda i, j: (0, j)),
                  pl.BlockSpec((D, tf), lambda i, j: (0, j + nf))],
        out_specs=[pl.BlockSpec((ts, tf), lambda i, j: (i, j)), pl.BlockSpec((2, ts, tf), lambda i, j: (0, i, j))],
        out_shape=[jax.ShapeDtypeStruct((S, Fh), BF16), jax.ShapeDtypeStruct((2, S, Fh), BF16)],
        compiler_params=pltpu.CompilerParams(dimension_semantics=("parallel", "parallel")),
    )(xb, w_in, w_in)


def _ffn_bwd_act(name, dyb, w_out_t, gu):
    S, D = dyb.shape
    Fh = w_out_t.shape[1]
    ts = _pick(S, (512, 256, 128))
    tf = _pick(Fh, (256, 128))

    def body(dy_ref, w_ref, gu_ref, o_ref):
        da = jnp.dot(dy_ref[...], w_ref[...], preferred_element_type=F32)
        g = gu_ref[0].astype(F32)
        u = gu_ref[1].astype(F32)
        sg = _sigmoid(g)
        o_ref[0] = (da * u * (sg * (1.0 + g * (1.0 - sg)))).astype(o_ref.dtype)
        o_ref[1] = (da * (g * sg)).astype(o_ref.dtype)

    return pl.pallas_call(
        body, name=name, grid=(S // ts, Fh // tf),
        in_specs=[pl.BlockSpec((ts, D), lambda i, j: (i, 0)), pl.BlockSpec((D, tf), lambda i, j: (0, j)),
                  pl.BlockSpec((2, ts, tf), lambda i, j: (0, i, j))],
        out_specs=pl.BlockSpec((2, ts, tf), lambda i, j: (0, i, j)),
        out_shape=jax.ShapeDtypeStruct((2, S, Fh), BF16),
        compiler_params=pltpu.CompilerParams(dimension_semantics=("parallel", "parallel")),
    )(dyb, w_out_t, gu)


def _rowwise(name, fn, tiled, params, outs, reds=(), ts=256):
    S = tiled[0][0].shape[0]
    ts = _pick(S, (ts, 128))
    n_t, n_p, n_o = len(tiled), len(params), len(outs)

    def body(*refs):
        vals = [r[...] for r in refs[:n_t + n_p]]
        res_o, res_r = fn(*vals)
        o_refs = refs[n_t + n_p:n_t + n_p + n_o]
        r_refs = refs[n_t + n_p + n_o:]
        for r, val in zip(o_refs, res_o, strict=True):
            r[...] = val.astype(r.dtype)
        if r_refs:
            i = pl.program_id(0)

            @pl.when(i == 0)
            def _():
                for r, val in zip(r_refs, res_r, strict=True):
                    r[...] = val.astype(F32)

            @pl.when(i > 0)
            def _():
                for r, val in zip(r_refs, res_r, strict=True):
                    r[...] += val.astype(F32)

    in_specs = [pl.BlockSpec((ts, w), functools.partial(lambda i, cb: (i, cb), cb=cb)) for _, cb, w in tiled]
    in_specs += [pl.BlockSpec(p.shape, lambda i: (0, 0)) for p in params]
    out_specs = [pl.BlockSpec((ts, w), lambda i: (i, 0)) for w, _ in outs]
    out_specs += [pl.BlockSpec(tuple(r), lambda i: (0, 0)) for r in reds]
    out_shape = [jax.ShapeDtypeStruct((S, w), dt) for w, dt in outs]
    out_shape += [jax.ShapeDtypeStruct(tuple(r), F32) for r in reds]
    res = pl.pallas_call(
        body, name=name, grid=(S // ts,), in_specs=in_specs, out_specs=out_specs, out_shape=out_shape,
        compiler_params=pltpu.CompilerParams(dimension_semantics=("arbitrary",)),
    )(*[t[0] for t in tiled], *params)
    return res[:n_o], res[n_o:]


def _ln(r, g, b):
    mu = jnp.mean(r, axis=-1, keepdims=True)
    xc = r - mu
    var = jnp.mean(xc * xc, axis=-1, keepdims=True)
    return xc * lax.rsqrt(var + LN_EPS) * g + b


def _ln_bwd(r, g, b, dy):
    _, vjp = jax.vjp(_ln, r, g, b)
    return vjp(dy)


def _shift_down(x, d):
    return x if d == 0 else pltpu.roll(x, d, 0)


def _conv_fwd(name, x, w):
    S = x.shape[0]
    ts = _pick(S, (256, 128))
    hb = ts // HALO

    def body(x_ref, prev_ref, w_ref, o_ref):
        i = pl.program_id(0)
        prev = jnp.where(i > 0, prev_ref[...], 0.0)
        xe = jnp.concatenate([prev, x_ref[...]], axis=0)
        y = jnp.zeros((ts + HALO, CONV_CH), F32)
        for j in range(CONV_K):
            y = y + w_ref[pl.ds(j, 1), :] * _shift_down(xe, CONV_K - 1 - j)
        y = y[HALO:, :]
        o_ref[...] = y * _sigmoid(y)

    return pl.pallas_call(
        body, name=name, grid=(S // ts,),
        in_specs=[pl.BlockSpec((ts, CONV_CH), lambda i: (i, 0)),
                  pl.BlockSpec((HALO, CONV_CH), lambda i: (jnp.maximum(i * hb - 1, 0), 0)),
                  pl.BlockSpec((CONV_K, CONV_CH), lambda i: (0, 0))],
        out_specs=pl.BlockSpec((ts, CONV_CH), lambda i: (i, 0)),
        out_shape=jax.ShapeDtypeStruct((S, CONV_CH), F32),
        compiler_params=pltpu.CompilerParams(dimension_semantics=("arbitrary",)),
    )(x, x, w)


def _conv_bwd(name, x, w, dout):
    S = x.shape[0]
    ts = _pick(S, (256, 128))
    hb = ts // HALO
    nt = S // ts
    n_ext = ts + 2 * HALO

    def body(x_ref, prev_ref, next_ref, w_ref, d_ref, dnext_ref, dx_ref, dw_ref):
        i = pl.program_id(0)
        prev = jnp.where(i > 0, prev_ref[...], 0.0)
        last = i == nt - 1
        nxt = jnp.where(last, 0.0, next_ref[...])
        dnxt = jnp.where(last, 0.0, dnext_ref[...])
        xe = jnp.concatenate([prev, x_ref[...], nxt], axis=0)
        de = jnp.concatenate([jnp.zeros((HALO, CONV_CH), F32), d_ref[...], dnxt], axis=0)
        y = jnp.zeros((n_ext, CONV_CH), F32)
        for j in range(CONV_K):
            y = y + w_ref[pl.ds(j, 1), :] * _shift_down(xe, CONV_K - 1 - j)
        sg = _sigmoid(y)
        dy = de * (sg * (1.0 + y * (1.0 - sg)))
        dx = jnp.zeros((n_ext, CONV_CH), F32)
        for j in range(CONV_K):
            m = CONV_K - 1 - j
            dx = dx + w_ref[pl.ds(j, 1), :] * (dy if m == 0 else pltpu.roll(dy, n_ext - m, 0))
        dx_ref[...] = dx[HALO:HALO + ts, :].astype(dx_ref.dtype)
        row = lax.broadcasted_iota(jnp.int32, (n_ext, 1), 0)
        dy_own = jnp.where((row >= HALO) & (row < HALO + ts), dy, 0.0)
        parts = [jnp.sum(dy_own * _shift_down(xe, CONV_K - 1 - j), axis=0, keepdims=True) for j in range(CONV_K)]
        dw = jnp.concatenate(parts + [jnp.zeros((HALO - CONV_K, CONV_CH), F32)], axis=0)

        @pl.when(i == 0)
        def _():
            dw_ref[...] = dw

        @pl.when(i > 0)
        def _():
            dw_ref[...] += dw

    tile = pl.BlockSpec((ts, CONV_CH), lambda i: (i, 0))
    prev = pl.BlockSpec((HALO, CONV_CH), lambda i: (jnp.maximum(i * hb - 1, 0), 0))
    nxt = pl.BlockSpec((HALO, CONV_CH), lambda i: (jnp.minimum((i + 1) * hb, nt * hb - 1), 0))
    return pl.pallas_call(
        body, name=name, grid=(nt,),
        in_specs=[tile, prev, nxt, pl.BlockSpec((CONV_K, CONV_CH), lambda i: (0, 0)), tile, nxt],
        out_specs=[tile, pl.BlockSpec((HALO, CONV_CH), lambda i: (0, 0))],
        out_shape=[jax.ShapeDtypeStruct((S, CONV_CH), BF16), jax.ShapeDtypeStruct((HALO, CONV_CH), F32)],
        compiler_params=pltpu.CompilerParams(dimension_semantics=("arbitrary",)),
    )(x, x, x, w, dout, dout)


_NT = (((1,), (1,)), ((), ()))
_TN = (((0,), (0,)), ((), ()))


def _softplus(z):
    return jnp.maximum(z, 0.0) + jnp.log(1.0 + jnp.exp(-jnp.abs(z)))


def _dot3(x, tri):
    hi = x.astype(BF16)
    r1 = x - hi.astype(F32)
    mid = r1.astype(BF16)
    lo = (r1 - mid.astype(F32)).astype(BF16)
    d = lambda p: jnp.dot(p, tri, preferred_element_type=F32)
    return d(hi) + d(mid) + d(lo)


def _sb_tiles(S):
    tq = _pick(S, (SB_TQ, 128))
    return tq, SB_TK


def _sb_fwd(q, k, v):
    H, S, dh = q.shape
    tq, tk = _sb_tiles(S)
    scale = dh ** -0.5

    def body(q_ref, k_ref, v_ref, o_ref, t_ref):
        qi = pl.program_id(1)
        qt = q_ref[...]
        nkt = (qi + 1) * (tq // tk)
        row = qi * tq + lax.broadcasted_iota(jnp.int32, (tq, tk), 0)
        col = lax.broadcasted_iota(jnp.int32, (tq, tk), 1)
        ri = lax.broadcasted_iota(jnp.int32, (tk, tk), 0)
        ci = lax.broadcasted_iota(jnp.int32, (tk, tk), 1)
        later = (ri > ci).astype(BF16)

        def step(jj, carry):
            acc, suf = carry
            j = nkt - 1 - jj
            off = pl.multiple_of(j * tk, tk)
            ks = k_ref[pl.ds(off, tk), :]
            vs = v_ref[pl.ds(off, tk), :]
            z = lax.dot_general(qt, ks, _NT, preferred_element_type=F32) * scale
            mask = (col + j * tk) < row
            lf = jnp.where(mask, -_softplus(z), 0.0)
            after = _dot3(lf, later) + suf
            w = jnp.where(mask, jnp.exp(z + lf + after), 0.0)
            acc = acc + jnp.dot(w.astype(BF16), vs, preferred_element_type=F32)
            suf = suf + jnp.sum(lf, axis=1, keepdims=True)
            return acc, suf

        acc, suf = lax.fori_loop(0, nkt, step, (jnp.zeros((tq, dh), F32), jnp.zeros((tq, 1), F32)))
        o_ref[...] = acc.astype(o_ref.dtype)
        t_ref[...] = suf

    whole = pl.BlockSpec((None, S, dh), lambda h, i: (h, 0, 0))
    return pl.pallas_call(
        body, name="sb_fwd", grid=(H, S // tq),
        in_specs=[pl.BlockSpec((None, tq, dh), lambda h, i: (h, i, 0)), whole, whole],
        out_specs=[pl.BlockSpec((None, tq, dh), lambda h, i: (h, i, 0)), pl.BlockSpec((None, tq, 1), lambda h, i: (h, i, 0))],
        out_shape=[jax.ShapeDtypeStruct((H, S, dh), BF16), jax.ShapeDtypeStruct((H, S, 1), F32)],
        compiler_params=pltpu.CompilerParams(dimension_semantics=("arbitrary", "arbitrary")),
    )(q, k, v)


def _sb_bwd(q, k, v, tot, do):
    H, S, dh = q.shape
    tq, tk = _sb_tiles(S)
    scale = dh ** -0.5

    def body(q_ref, k_ref, v_ref, t_ref, do_ref, dq_ref, dk_ref, dv_ref):
        qi = pl.program_id(1)

        @pl.when(qi == 0)
        def _():
            dk_ref[...] = jnp.zeros_like(dk_ref)
            dv_ref[...] = jnp.zeros_like(dv_ref)

        qt = q_ref[...]
        dot_ = do_ref[...]
        total = t_ref[...]
        nkt = (qi + 1) * (tq // tk)
        row = qi * tq + lax.broadcasted_iota(jnp.int32, (tq, tk), 0)
        col = lax.broadcasted_iota(jnp.int32, (tq, tk), 1)
        ri = lax.broadcasted_iota(jnp.int32, (tk, tk), 0)
        ci = lax.broadcasted_iota(jnp.int32, (tk, tk), 1)
        upto = (ri <= ci).astype(BF16)

        def step(j, carry):
            dq, pre_l, pre_g = carry
            off = pl.multiple_of(j * tk, tk)
            ks = k_ref[pl.ds(off, tk), :]
            vs = v_ref[pl.ds(off, tk), :]
            z = lax.dot_general(qt, ks, _NT, preferred_element_type=F32) * scale
            mask = (col + j * tk) < row
            lf = jnp.where(mask, -_softplus(z), 0.0)
            after = total - (pre_l + _dot3(lf, upto))
            w = jnp.where(mask, jnp.exp(z + lf + after), 0.0)
            dw = lax.dot_general(dot_, vs, _NT, preferred_element_type=F32)
            gl = dw * w
            dz = jnp.where(mask, gl - _sigmoid(z) * (pre_g + _dot3(gl, upto)), 0.0)
            dzb = (dz * scale).astype(BF16)
            dq = dq + jnp.dot(dzb, ks, preferred_element_type=F32)
            dk_ref[pl.ds(off, tk), :] += lax.dot_general(dzb, qt, _TN, preferred_element_type=F32)
            dv_ref[pl.ds(off, tk), :] += lax.dot_general(w.astype(BF16), dot_, _TN, preferred_element_type=F32)
            pre_l = pre_l + jnp.sum(lf, axis=1, keepdims=True)
            pre_g = pre_g + jnp.sum(gl, axis=1, keepdims=True)
            return dq, pre_l, pre_g

        zero1 = jnp.zeros((tq, 1), F32)
        dq, _, _ = lax.fori_loop(0, nkt, step, (jnp.zeros((tq, dh), F32), zero1, zero1))
        dq_ref[...] = dq

    whole = pl.BlockSpec((None, S, dh), lambda h, i: (h, 0, 0))
    tile = pl.BlockSpec((None, tq, dh), lambda h, i: (h, i, 0))
    big = jax.ShapeDtypeStruct((H, S, dh), F32)
    return pl.pallas_call(
        body, name="sb_bwd", grid=(H, S // tq),
        in_specs=[tile, whole, whole, pl.BlockSpec((None, tq, 1), lambda h, i: (h, i, 0)), tile],
        out_specs=[tile, whole, whole], out_shape=[big, big, big],
        compiler_params=pltpu.CompilerParams(dimension_semantics=("arbitrary", "arbitrary")),
    )(q, k, v, tot, do)


def _bmm(a, b):
    return lax.dot_general(a, b, (((2,), (1,)), ((0,), (0,))), precision=HI, preferred_element_type=F32)


def _bmm_nt(a, b):
    return lax.dot_general(a, b, (((2,), (2,)), ((0,), (0,))), precision=HI, preferred_element_type=F32)


def _bmm_tn(a, b):
    return lax.dot_general(a, b, (((1,), (1,)), ((0,), (0,))), precision=HI, preferred_element_type=F32)


def _tri_inv(lower):
    C = lower.shape[-1]
    ii = lax.broadcasted_iota(jnp.int32, (C, C), 0)
    jj = lax.broadcasted_iota(jnp.int32, (C, C), 1)
    eye = (ii == jj).astype(F32)[None]
    xd = jnp.where((ii // 8 == jj // 8)[None], -lower, 0.0)
    x2 = _bmm(xd, xd)
    x4 = _bmm(x2, x2)
    inv = eye + xd
    inv = inv + _bmm(inv, x2)
    inv = inv + _bmm(inv, x4)
    b = 8
    while b < C:
        off = jnp.where(((ii // (2 * b) == jj // (2 * b)) & (ii // b != jj // b))[None], lower, 0.0)
        inv = inv - _bmm(inv, _bmm(off, inv))
        b *= 2
    return inv


@jax.custom_vjp
def _tri_solve(lower, rhs):
    return _bmm(_tri_inv(lower), rhs)


def _tri_solve_fwd(lower, rhs):
    inv = _tri_inv(lower)
    sol = _bmm(inv, rhs)
    return sol, (inv, sol)


def _tri_solve_bwd(res, dsol):
    inv, sol = res
    drhs = _bmm_tn(inv, dsol)
    C = inv.shape[-1]
    ii = lax.broadcasted_iota(jnp.int32, (C, C), 0)
    jj = lax.broadcasted_iota(jnp.int32, (C, C), 1)
    return jnp.where((jj < ii)[None], -_bmm_nt(drhs, sol), 0.0), drhs


_tri_solve.defvjp(_tri_solve_fwd, _tri_solve_bwd)


def _gdn_chunk(state, q, k, v, gz, gb, ga, a_log, dt_bias, norm_w):
    _, C, dk = q.shape
    beta = _sigmoid(gb)
    g = -jnp.exp(a_log) * _softplus(ga + dt_bias)
    qn = q * lax.rsqrt(jnp.sum(q * q, axis=-1, keepdims=True) + RMS_EPS) * (dk ** -0.5)
    kn = k * lax.rsqrt(jnp.sum(k * k, axis=-1, keepdims=True) + RMS_EPS)
    ii = lax.broadcasted_iota(jnp.int32, (C, C), 0)[None]
    jj = lax.broadcasted_iota(jnp.int32, (C, C), 1)[None]
    incl = jj <= ii
    g_row = jnp.sum(jnp.where(ii == jj, g, 0.0), axis=1, keepdims=True)
    gc_col = jnp.sum(jnp.where(incl, g_row, 0.0), axis=2, keepdims=True)
    gc_row = jnp.sum(jnp.where(ii <= jj, g, 0.0), axis=1, keepdims=True)
    decay = jnp.where(incl, jnp.exp(jnp.where(incl, gc_col - gc_row, 0.0)), 0.0)
    lower = jnp.where(jj < ii, beta * _bmm_nt(kn, kn) * decay, 0.0)
    egc = jnp.exp(gc_col)
    u = _tri_solve(lower, v * beta)
    w = _tri_solve(lower, kn * (beta * egc))
    qk = jnp.where(incl, _bmm_nt(qn, kn) * decay, 0.0)
    g_last = jnp.sum(g, axis=1, keepdims=True)
    v_new = u - _bmm(w, state)
    o = _bmm(qn * egc, state) + _bmm(qk, v_new)
    new_state = state * jnp.exp(g_last) + _bmm_tn(kn * jnp.exp(g_last - gc_col), v_new)
    o = o * lax.rsqrt(jnp.mean(o * o, axis=-1, keepdims=True) + RMS_EPS) * norm_w
    return new_state, o * (gz * _sigmoid(gz))


def _gdn_fwd(q, k, v, gz, gb, ga, a_log, dt_bias, norm_w):
    H, S, dh = q.shape
    C = GDN_CHUNK
    n = S // C

    def body(q_ref, k_ref, v_ref, gz_ref, gb_ref, ga_ref, al_ref, dt_ref, nw_ref, o_ref, st_ref, state):
        @pl.when(pl.program_id(0) == 0)
        def _():
            state[...] = jnp.zeros_like(state)

        st = state[...]
        st_ref[...] = st
        new, o = _gdn_chunk(st, q_ref[...], k_ref[...], v_ref[...], gz_ref[...], gb_ref[...], ga_ref[...],
                            al_ref[...], dt_ref[...], nw_ref[...])
        state[...] = new
        o_ref[...] = o.astype(o_ref.dtype)

    big = pl.BlockSpec((H, C, dh), lambda c: (0, c, 0))
    col = pl.BlockSpec((H, C, 1), lambda c: (0, c, 0))
    par = pl.BlockSpec((H, 1, 1), lambda c: (0, 0, 0))
    return pl.pallas_call(
        body, name="gdn_fwd", grid=(n,),
        in_specs=[big, big, big, big, col, col, par, par, pl.BlockSpec((1, 1, dh), lambda c: (0, 0, 0))],
        out_specs=[big, pl.BlockSpec((None, H, dh, dh), lambda c: (c, 0, 0, 0))],
        out_shape=[jax.ShapeDtypeStruct((H, S, dh), BF16), jax.ShapeDtypeStruct((n, H, dh, dh), F32)],
        scratch_shapes=[pltpu.VMEM((H, dh, dh), F32)],
        compiler_params=pltpu.CompilerParams(dimension_semantics=("arbitrary",)),
    )(q, k, v, gz, gb, ga, a_log, dt_bias, norm_w)


def _gdn_bwd(q, k, v, gz, gb, ga, a_log, dt_bias, norm_w, states, do):
    H, S, dh = q.shape
    C = GDN_CHUNK
    n = S // C

    def body(q_ref, k_ref, v_ref, gz_ref, gb_ref, ga_ref, al_ref, dt_ref, nw_ref, st_ref, do_ref,
             dq_ref, dk_ref, dv_ref, dgz_ref, dgb_ref, dga_ref, dal_ref, ddt_ref, dnw_ref, dstate):
        @pl.when(pl.program_id(0) == 0)
        def _():
            dstate[...] = jnp.zeros_like(dstate)
            dal_ref[...] = jnp.zeros_like(dal_ref)
            ddt_ref[...] = jnp.zeros_like(ddt_ref)
            dnw_ref[...] = jnp.zeros_like(dnw_ref)

        args = (st_ref[...], q_ref[...], k_ref[...], v_ref[...], gz_ref[...], gb_ref[...], ga_ref[...],
                al_ref[...], dt_ref[...], nw_ref[...])
        _, vjp = jax.vjp(_gdn_chunk, *args)
        dst, dq, dk, dv, dgz, dgb, dga, dal, ddt, dnw = vjp((dstate[...], do_ref[...]))
        dstate[...] = dst
        dq_ref[...] = dq
        dk_ref[...] = dk
        dv_ref[...] = dv
        dgz_ref[...] = dgz
        dgb_ref[...] = dgb
        dga_ref[...] = dga
        dal_ref[...] += dal
        ddt_ref[...] += ddt
        dnw_ref[...] += dnw

    rev = lambda c: (0, n - 1 - c, 0)
    big = pl.BlockSpec((H, C, dh), rev)
    col = pl.BlockSpec((H, C, 1), rev)
    par = pl.BlockSpec((H, 1, 1), lambda c: (0, 0, 0))
    nwp = pl.BlockSpec((1, 1, dh), lambda c: (0, 0, 0))
    f = jax.ShapeDtypeStruct
    return pl.pallas_call(
        body, name="gdn_bwd", grid=(n,),
        in_specs=[big, big, big, big, col, col, par, par, nwp,
                  pl.BlockSpec((None, H, dh, dh), lambda c: (n - 1 - c, 0, 0, 0)), big],
        out_specs=[big, big, big, big, col, col, par, par, nwp],
        out_shape=[f((H, S, dh), F32)] * 4 + [f((H, S, 1), F32)] * 2 + [f((H, 1, 1), F32)] * 2 + [f((1, 1, dh), F32)],
        scratch_shapes=[pltpu.VMEM((H, dh, dh), F32)],
        compiler_params=pltpu.CompilerParams(dimension_semantics=("arbitrary",)),
    )(q, k, v, gz, gb, ga, a_log, dt_bias, norm_w, states, do)


def _heads(t):
    S = t.shape[0]
    return t.reshape(S, N_HEADS, -1).transpose(1, 0, 2)


def _unheads(t):
    return t.transpose(1, 0, 2).reshape(t.shape[1], -1)


def _local_step(x, p, target, W, P):
    S = x.shape[0]
    D = D_MODEL
    row = lambda a: a.reshape(1, -1)
    xb = x.astype(BF16)
    pb = p.astype(BF16)
    Wt = {n: W[n].T for n in ("ffn1_w_in", "ffn1_w_out", "w_mix_pad", "w_branch_sb", "w_branch_gdn", "w_mix_out",
                              "ffn2_w_in", "ffn2_w_out", "w_ple_gate")}

    def ffn_fwd(tag, h, hb, w_in, w_out, g, b):
        a, gu = _ffn_up(f"{tag}_up", hb, w_in)
        f = _mm(f"{tag}_down", a, w_out, F32)

        def fn(h, f, g, b):
            r = ALPHA * h + 0.5 * f
            y = _ln(r, g, b)
            return (r, y, y), ()

        (r, y, yb), _ = _rowwise(f"{tag}_ln", fn, [(h, 0, D), (f, 0, D)], [g, b], [(D, F32), (D, F32), (D, BF16)])
        return a, gu, r, y, yb

    a1, gu1, r1, h1, h1b = ffn_fwd("ffn1", x, xb, W["ffn1_w_in"], W["ffn1_w_out"], P["ln1_g"], P["ln1_b"])

    wmix = W["w_mix_pad"]
    sbp = _mm("mix_sb", h1b, wmix, BF16, n=3 * HEAD_W, col_off=MIX_SB)
    gdnp = _mm("mix_gdn", h1b, wmix, F32, n=4 * HEAD_W, col_off=MIX_GDN)
    gates = _mm("mix_gates", h1b, wmix, F32, n=2 * D, col_off=MIX_GATES)
    ba = _mm("mix_ba", h1b, wmix, F32, n=128, col_off=MIX_BA)

    sq, sk, sv = (_heads(sbp[:, i * HEAD_W:(i + 1) * HEAD_W]) for i in range(3))
    att, sb_tot = _sb_fwd(sq, sk, sv)
    attb = _unheads(att)
    y_sb = _mm("sb_out", attb, W["w_branch_sb"], F32)

    conv_w = P["conv_w"]
    qkv = _conv_fwd("conv_fwd", gdnp, conv_w)
    cq, ck, cv = (_heads(qkv[:, i * HEAD_W:(i + 1) * HEAD_W]) for i in range(3))
    gz = _heads(gdnp[:, 3 * HEAD_W:])
    gb = ba[:, :N_HEADS].T.reshape(N_HEADS, S, 1)
    ga = ba[:, N_HEADS:2 * N_HEADS].T.reshape(N_HEADS, S, 1)
    a_log3 = P["a_log"].reshape(N_HEADS, 1, 1)
    dt3 = P["dt_bias"].reshape(N_HEADS, 1, 1)
    nw3 = P["gdn_norm_w"].reshape(1, 1, HEAD_DIM)
    go, states = _gdn_fwd(cq, ck, cv, gz, gb, ga, a_log3, dt3, nw3)
    gob = _unheads(go)
    y_gdn = _mm("gdn_out", gob, W["w_branch_gdn"], F32)

    def merge_fn(gs, gg, ys, yg, bs, bg):
        return ((_sigmoid(gs + bs) * ys + _sigmoid(gg + bg) * yg),), ()

    b_gate = P["b_gate"]
    bs, bg = b_gate[:, :D], b_gate[:, D:]
    (merged,), _ = _rowwise("mix_merge", merge_fn, [(gates, 0, D), (gates, 1, D), (y_sb, 0, D), (y_gdn, 0, D)], [bs, bg],
                            [(D, BF16)])
    mix = _mm("mix_out", merged, W["w_mix_out"], F32)

    def ln2_fn(h, f, g, b):
        r = ALPHA * h + f
        y = _ln(r, g, b)
        return (r, y, y), ()

    (r2, h2, h2b), _ = _rowwise("mix_ln", ln2_fn, [(h1, 0, D), (mix, 0, D)], [P["ln2_g"], P["ln2_b"]],
                                [(D, F32), (D, F32), (D, BF16)])

    a2, gu2, r3, h3, h3b = ffn_fwd("ffn2", h2, h2b, W["ffn2_w_in"], W["ffn2_w_out"], P["ln3_g"], P["ln3_b"])

    zg = _mm("ple_gate", h3b, W["w_ple_gate"], F32)
    pp = _mm("ple_proj", pb, W["w_ple"], F32)

    def ple(h, zg, pp, bp, g, b):
        return _ln(ALPHA * h + _sigmoid(zg + bp) * pp, g, b)

    def head_fn(h, zg, pp, tgt, bp, g, b):
        y, vjp = jax.vjp(ple, h, zg, pp, bp, g, b)
        err = y - tgt
        dh, dzg, dpp, dbp, dg, db = vjp(err * (1.0 / D))
        loss = 0.5 * jnp.sum(jnp.sum(err * err, axis=1, keepdims=True), axis=0, keepdims=True) * (1.0 / D)
        return (dh, dzg, dpp), (loss, dbp, dg, db)

    (dh3_a, dzg, dpp), (loss, d_bple, d_ln4g, d_ln4b) = _rowwise(
        "ple_head", head_fn, [(h3, 0, D), (zg, 0, D), (pp, 0, D), (target, 0, D)],
        [P["b_ple_gate"], P["ln4_g"], P["ln4_b"]], [(D, F32), (D, BF16), (D, BF16)],
        [(1, 1), (1, D), (1, D), (1, D)])

    grads, small = {}, {"b_ple_gate": d_bple, "ln4_g": d_ln4g, "ln4_b": d_ln4b}
    grads["w_ple_gate"] = _mm_tn("d_w_ple_gate", h3b, dzg)
    grads["w_ple"] = _mm_tn("d_w_ple", pb, dpp)
    dh3_b = _mm("d_ple_gate_in", dzg, Wt["w_ple_gate"], F32)

    def ffn_bwd(tag, dy_parts, r, g, b, a, gu, hb_in, w_in_t, w_out_t):
        n_parts = len(dy_parts)

        def fn(*vals):
            dy = vals[0]
            for extra in vals[1:n_parts]:
                dy = dy + extra
            r, g, b = vals[n_parts:]
            dr, dg, db = _ln_bwd(r, g, b, dy)
            return (ALPHA * dr, 0.5 * dr), (dg, db)

        (dh_res, dfb), (dg, db) = _rowwise(f"{tag}_ln_bwd", fn, [(t, 0, D) for t in dy_parts] + [(r, 0, D)], [g, b],
                                           [(D, F32), (D, BF16)], [(1, D), (1, D)])
        dgu = _ffn_bwd_act(f"{tag}_act_bwd", dfb, w_out_t, gu)
        d_w_out = _mm_tn(f"d_{tag}_w_out", a, dfb)
        d_w_in = _mm_tn_gu(f"d_{tag}_w_in", hb_in, dgu)
        dh_ffn = _mm_gu(f"{tag}_in_bwd", dgu, w_in_t, F32)
        return dh_res, dh_ffn, d_w_in, d_w_out, dg, db

    dh2_a, dh2_b, grads["ffn2_w_in"], grads["ffn2_w_out"], small["ln3_g"], small["ln3_b"] = ffn_bwd(
        "ffn2", [dh3_a, dh3_b], r3, P["ln3_g"], P["ln3_b"], a2, gu2, h2b, Wt["ffn2_w_in"], Wt["ffn2_w_out"])

    def ln2_bwd_fn(d1, d2, r, g, b):
        dr, dg, db = _ln_bwd(r, g, b, d1 + d2)
        return (ALPHA * dr, dr), (dg, db)

    (dh1_a, dmixb), (small["ln2_g"], small["ln2_b"]) = _rowwise(
        "mix_ln_bwd", ln2_bwd_fn, [(dh2_a, 0, D), (dh2_b, 0, D), (r2, 0, D)], [P["ln2_g"], P["ln2_b"]],
        [(D, F32), (D, BF16)], [(1, D), (1, D)])
    grads["w_mix_out"] = _mm_tn("d_w_mix_out", merged, dmixb)
    dmerged = _mm("mix_out_bwd", dmixb, Wt["w_mix_out"], F32)

    def merge_bwd_fn(dm, gs, gg, ys, yg, bs, bg):
        ss, sg = _sigmoid(gs + bs), _sigmoid(gg + bg)
        dgs = dm * ys * ss * (1.0 - ss)
        dgg = dm * yg * sg * (1.0 - sg)
        return (dgs, dgg, dm * ss, dm * sg), (jnp.sum(dgs, axis=0, keepdims=True), jnp.sum(dgg, axis=0, keepdims=True))

    (dgs, dgg, dy_sb, dy_gdn), (d_bs, d_bg) = _rowwise(
        "mix_merge_bwd", merge_bwd_fn, [(dmerged, 0, D), (gates, 0, D), (gates, 1, D), (y_sb, 0, D), (y_gdn, 0, D)],
        [bs, bg], [(D, BF16)] * 4, [(1, D), (1, D)])
    small["b_gate"] = jnp.concatenate([d_bs, d_bg], axis=1)

    grads["w_branch_sb"] = _mm_tn("d_w_branch_sb", attb, dy_sb)
    datt = _mm("sb_out_bwd", dy_sb, Wt["w_branch_sb"], BF16)
    dsq, dsk, dsv = _sb_bwd(sq, sk, sv, sb_tot, _heads(datt))

    grads["w_branch_gdn"] = _mm_tn("d_w_branch_gdn", gob, dy_gdn)
    dgo = _mm("gdn_out_bwd", dy_gdn, Wt["w_branch_gdn"], F32)
    dcq, dck, dcv, dgz, dgb, dga, d_alog, d_dt, d_nw = _gdn_bwd(cq, ck, cv, gz, gb, ga, a_log3, dt3, nw3, states,
                                                              _heads(dgo))
    small["a_log"] = d_alog.reshape(1, N_HEADS)
    small["dt_bias"] = d_dt.reshape(1, N_HEADS)
    small["gdn_norm_w"] = d_nw.reshape(1, HEAD_DIM)
    dqkv = jnp.concatenate([_unheads(dcq), _unheads(dck), _unheads(dcv)], axis=1)
    dconv_in, d_conv_w = _conv_bwd("conv_bwd", gdnp, conv_w, dqkv)
    grads["conv_w"] = d_conv_w[:CONV_K]

    dba = jnp.concatenate([dgb.reshape(N_HEADS, S).T, dga.reshape(N_HEADS, S).T,
                           jnp.zeros((S, 128 - 2 * N_HEADS), F32)], axis=1)
    dproj = jnp.concatenate([_unheads(dsq).astype(BF16), _unheads(dsk).astype(BF16), _unheads(dsv).astype(BF16),
                             dconv_in, _unheads(dgz).astype(BF16), dgs, dgg, dba.astype(BF16)], axis=1)
    d_wmix = _mm_tn("d_w_mix_in", h1b, dproj)
    grads["w_mix_in"] = jnp.concatenate([d_wmix[:, :MIX_GATES], d_wmix[:, MIX_BA:MIX_BA + 2 * N_HEADS],
                                         d_wmix[:, MIX_GATES:MIX_BA]], axis=1)
    dh1_b = _mm("mix_in_bwd", dproj, Wt["w_mix_pad"], F32)

    dx_a, dx_b, grads["ffn1_w_in"], grads["ffn1_w_out"], small["ln1_g"], small["ln1_b"] = ffn_bwd(
        "ffn1", [dh1_a, dh1_b], r1, P["ln1_g"], P["ln1_b"], a1, gu1, xb, Wt["ffn1_w_in"], Wt["ffn1_w_out"])

    (grad_x,), _ = _rowwise("grad_x", lambda a, b: ((a + b,), ()), [(dx_a, 0, D), (dx_b, 0, D)], [], [(D, F32)])
    return loss[0, 0], grad_x, grads, small


def _coords():
    return lax.axis_index("x"), lax.axis_index("y"), lax.axis_index("c")


def _all_gather(name, shard):
    R, C = shard.shape

    def body(x_ref, out_ref, send_sems, recv_sems, local_sem):
        x, y, c = _coords()
        me, sibling = (x, y, c), (x, y, 1 - c)
        chips = [(1 - x, y), (x, 1 - y), (1 - x, 1 - y)]

        def blk(px, py, pc):
            return out_ref.at[4 * px + 2 * py + pc]

        def copy(k, block, to, src=None):
            return pltpu.make_async_remote_copy(
                src_ref=blk(*block) if src is None else src, dst_ref=blk(*block),
                send_sem=send_sems.at[k], recv_sem=recv_sems.at[k], device_id=to, device_id_type=MESH)

        mine = pltpu.make_async_copy(x_ref, blk(*me), local_sem)
        mine.start()
        first = [copy(0, me, sibling, src=x_ref)]
        first += [copy(1 + j, me, (*chip, c), src=x_ref) for j, chip in enumerate(chips)]
        for cp in first:
            cp.start()
        passed = [copy(4 + j, (*chip, c), sibling) for j, chip in enumerate(chips)]
        for j, chip in enumerate(chips):
            copy(1 + j, (*chip, c), me).wait_recv()
            passed[j].start()
        copy(0, sibling, me).wait_recv()
        for j, chip in enumerate(chips):
            copy(4 + j, (*chip, 1 - c), me).wait_recv()
        for cp in first + passed:
            cp.wait_send()
        mine.wait()

    return pl.pallas_call(
        body, name=name, out_shape=jax.ShapeDtypeStruct((N_DEV, R, C), shard.dtype),
        in_specs=[pl.BlockSpec(memory_space=pl.ANY)], out_specs=pl.BlockSpec(memory_space=pl.ANY),
        scratch_shapes=[pltpu.SemaphoreType.DMA((7,)), pltpu.SemaphoreType.DMA((7,)), pltpu.SemaphoreType.DMA],
    )(shard)


def _all_to_all(name, src):
    def body(src_ref, dst_ref, send_sems, recv_sems, local_sem):
        x, y, c = _coords()
        me = 4 * x + 2 * y + c
        mine = pltpu.make_async_copy(src_ref.at[me], dst_ref.at[me], local_sem)
        mine.start()
        sends, recvs = [], []
        for k in range(1, N_DEV):
            px = 1 - x if k & 4 else x
            py = 1 - y if k & 2 else y
            pc = 1 - c if k & 1 else c
            peer = 4 * px + 2 * py + pc
            sends.append(pltpu.make_async_remote_copy(
                src_ref=src_ref.at[peer], dst_ref=dst_ref.at[me], send_sem=send_sems.at[k - 1],
                recv_sem=recv_sems.at[k - 1], device_id=(px, py, pc), device_id_type=MESH))
            recvs.append(pltpu.make_async_remote_copy(
                src_ref=src_ref.at[me], dst_ref=dst_ref.at[peer], send_sem=send_sems.at[k - 1],
                recv_sem=recv_sems.at[k - 1], device_id=(px, py, pc), device_id_type=MESH))
        for cp in sends:
            cp.start()
        for cp in recvs:
            cp.wait_recv()
        for cp in sends:
            cp.wait_send()
        mine.wait()

    return pl.pallas_call(
        body, name=name, out_shape=jax.ShapeDtypeStruct(src.shape, src.dtype),
        in_specs=[pl.BlockSpec(memory_space=pl.ANY)], out_specs=pl.BlockSpec(memory_space=pl.ANY),
        scratch_shapes=[pltpu.SemaphoreType.DMA((7,)), pltpu.SemaphoreType.DMA((7,)), pltpu.SemaphoreType.DMA],
    )(src)


def _adamw(name, parts, w, m, v, tr):
    R = w.shape[0]
    c1 = 1.0 - ADAM_B1 ** ADAM_STEP
    c2 = 1.0 - ADAM_B2 ** ADAM_STEP

    def body(p_ref, w_ref, m_ref, v_ref, g_ref, d_ref, nm_ref, nv_ref):
        g = p_ref[0].astype(F32)
        for d in range(1, N_DEV):
            g = g + p_ref[d].astype(F32)
        nm = ADAM_B1 * m_ref[...] + (1.0 - ADAM_B1) * g
        nv = ADAM_B2 * v_ref[...] + (1.0 - ADAM_B2) * (g * g)
        g_ref[...] = g
        nm_ref[...] = nm
        nv_ref[...] = nv
        d_ref[...] = -ADAM_LR * ((nm / c1) / (jnp.sqrt(nv / c2) + ADAM_EPS) + ADAM_WD * w_ref[...])

    t = pl.BlockSpec((tr, PACK_COLS), lambda i: (i, 0))
    o = jax.ShapeDtypeStruct((R, PACK_COLS), F32)
    return pl.pallas_call(
        body, name=name, grid=(R // tr,),
        in_specs=[pl.BlockSpec((N_DEV, tr, PACK_COLS), lambda i: (0, i, 0)), t, t, t],
        out_specs=[t, t, t, t], out_shape=[o, o, o, o],
        compiler_params=pltpu.CompilerParams(dimension_semantics=("parallel",)),
    )(parts, w, m, v)


def _pack_rows(flats, rows):
    cat = jnp.concatenate(flats, axis=-1)
    pad = rows * PACK_COLS - cat.shape[-1]
    cat = jnp.pad(cat, [(0, 0)] * (cat.ndim - 1) + [(0, pad)])
    return cat.reshape(cat.shape[:-1] + (rows, PACK_COLS))


def _shard_shape(name):
    full, ax = BIG_SHAPES[name]
    s = list(full)
    s[ax] //= N_DEV
    return tuple(s)


def _to_shards(name, full):
    shape, ax = BIG_SHAPES[name]
    if ax == 0:
        return full.reshape(N_DEV, -1)
    r, cdim = shape
    return full.reshape(r, N_DEV, cdim // N_DEV).transpose(1, 0, 2).reshape(N_DEV, -1)


def _from_shards(name, sh):
    shape, ax = BIG_SHAPES[name]
    if ax == 0:
        return sh.reshape(shape)
    r, cdim = shape
    return sh.reshape(N_DEV, r, cdim // N_DEV).transpose(1, 0, 2).reshape(shape)


def _big_rows():
    n = sum(math.prod(_shard_shape(k)) for k in BIG) + math.prod(_shard_shape("conv_w"))
    rows = -(-n // PACK_COLS)
    return -(-rows // ADAM_ROWS) * ADAM_ROWS


def kernel(x, p, ffn1_w_in, ffn1_w_out, ln1_g, ln1_b, w_mix_in, b_gate, conv_w, a_log, dt_bias, gdn_norm_w, w_branch_sb, w_branch_gdn, w_mix_out, ln2_g, ln2_b, ffn2_w_in, ffn2_w_out, ln3_g, ln3_b, w_ple_gate, b_ple_gate, w_ple, ln4_g, ln4_b, loss_target, m_ffn1_w_in, m_ffn1_w_out, m_ln1_g, m_ln1_b, m_w_mix_in, m_b_gate, m_conv_w, m_a_log, m_dt_bias, m_gdn_norm_w, m_w_branch_sb, m_w_branch_gdn, m_w_mix_out, m_ln2_g, m_ln2_b, m_ffn2_w_in, m_ffn2_w_out, m_ln3_g, m_ln3_b, m_w_ple_gate, m_b_ple_gate, m_w_ple, m_ln4_g, m_ln4_b, v_ffn1_w_in, v_ffn1_w_out, v_ln1_g, v_ln1_b, v_w_mix_in, v_b_gate, v_conv_w, v_a_log, v_dt_bias, v_gdn_norm_w, v_w_branch_sb, v_w_branch_gdn, v_w_mix_out, v_ln2_g, v_ln2_b, v_ffn2_w_in, v_ffn2_w_out, v_ln3_g, v_ln3_b, v_w_ple_gate, v_b_ple_gate, v_w_ple, v_ln4_g, v_ln4_b):
    given = dict(locals())
    w_loc = {n: given[n][0] for n in WEIGHTS}
    m_loc = {n: given["m_" + n][0] for n in WEIGHTS}
    v_loc = {n: given["v_" + n][0] for n in WEIGHTS}
    sizes = {n: w_loc[n].size for n in WEIGHTS}
    big_rows = _big_rows()

    conv_hi = conv_w[0].astype(BF16)
    conv_lo = (conv_w[0] - conv_hi.astype(F32)).astype(BF16)
    shard_bf = _pack_rows([w_loc[n].astype(BF16).reshape(-1) for n in BIG] + [conv_lo.reshape(-1)], big_rows)
    gathered = _all_gather("gather_weights", shard_bf).reshape(N_DEV, -1)
    W, off = {}, 0
    for n in BIG:
        W[n] = _from_shards(n, gathered[:, off:off + sizes[n]])
        off += sizes[n]
    conv_lo_full = _from_shards("conv_w", gathered[:, off:off + sizes["conv_w"]])
    wm = W.pop("w_mix_in")
    W["w_mix_pad"] = jnp.concatenate([wm[:, :7 * HEAD_W], wm[:, 7 * HEAD_W + 2 * N_HEADS:],
                                      wm[:, 7 * HEAD_W:7 * HEAD_W + 2 * N_HEADS],
                                      jnp.zeros((D_MODEL, 128 - 2 * N_HEADS), BF16)], axis=1)
    P = {n: w_loc[n].reshape(1, -1) for n in SMALL}
    P["conv_w"] = W.pop("conv_w").astype(F32) + conv_lo_full.astype(F32)

    loss, grad_x, grads, small = _local_step(x[0], p[0, 0], loss_target[0], W, P)
    loss = lax.psum(loss, ("x", "y", "c"))

    send = _pack_rows([_to_shards(n, grads[n]).astype(BF16) for n in BIG], big_rows)
    parts = _all_to_all("scatter_grads", send)
    small_rows = 16
    small_parts = _all_gather("gather_small_grads", _pack_rows([small[n].reshape(-1) for n in SMALL], small_rows))

    pack = lambda d, names, rows: _pack_rows([d[n].reshape(-1) for n in names], rows)
    big_out = _adamw("adamw_big", parts, pack(w_loc, BIG, big_rows), pack(m_loc, BIG, big_rows),
                     pack(v_loc, BIG, big_rows), ADAM_ROWS)
    small_out = _adamw("adamw_small", small_parts, pack(w_loc, SMALL, small_rows), pack(m_loc, SMALL, small_rows),
                       pack(v_loc, SMALL, small_rows), small_rows)

    def unpack(flat, names):
        res, off = {}, 0
        for n in names:
            res[n] = flat[off:off + sizes[n]].reshape(given[n].shape)
            off += sizes[n]
        return res

    outs = []
    for bo, so in zip(big_out, small_out, strict=True):
        d = {**unpack(bo.reshape(-1), BIG), **unpack(so.reshape(-1), SMALL)}
        outs.append([d[n] for n in WEIGHTS])
    g_out, d_out, nm_out, nv_out = outs
    return (loss, grad_x[None], *g_out, *d_out, *nm_out, *nv_out)
```

```python
import functools
import math

import jax
import jax.numpy as jnp
from jax import lax
from jax.experimental import pallas as pl
from jax.experimental.pallas import tpu as pltpu

F32 = jnp.float32
BF16 = jnp.bfloat16
MESH = pl.DeviceIdType.MESH
HI = lax.Precision.HIGHEST

N_DEV = 8
D_MODEL = 1024
D_FF = 2816
PLE_DIM = 256
N_HEADS = 8
HEAD_DIM = 64
HEAD_W = N_HEADS * HEAD_DIM
GDN_CHUNK = 64
CONV_K = 4
CONV_CH = 3 * HEAD_W
ALPHA = 2.0 ** 0.25
LN_EPS = 1e-5
RMS_EPS = 1e-6
MIX_SB = 0
MIX_GDN = 3 * HEAD_W
MIX_GATES = MIX_GDN + 4 * HEAD_W
MIX_BA = MIX_GATES + 2 * D_MODEL
MIX_PAD = MIX_BA + 128
N_IN = 7 * HEAD_W + 2 * N_HEADS + 2 * D_MODEL

ADAM_LR = 0.001
ADAM_B1 = 0.9
ADAM_B2 = 0.999
ADAM_EPS = 1e-08
ADAM_WD = 0.01
ADAM_STEP = 10

SB_TQ = 512
SB_TK = 128
HALO = 8

BIG = ["ffn1_w_in", "ffn1_w_out", "w_mix_in", "conv_w", "w_branch_sb", "w_branch_gdn", "w_mix_out",
       "ffn2_w_in", "ffn2_w_out", "w_ple_gate", "w_ple"]
BIG_SHAPES = {
    "ffn1_w_in": ((D_MODEL, 2 * D_FF), 1), "ffn1_w_out": ((D_FF, D_MODEL), 0),
    "w_mix_in": ((D_MODEL, N_IN), 1), "conv_w": ((CONV_K, CONV_CH), 1),
    "w_branch_sb": ((HEAD_W, D_MODEL), 1), "w_branch_gdn": ((HEAD_W, D_MODEL), 1),
    "w_mix_out": ((D_MODEL, D_MODEL), 0),
    "ffn2_w_in": ((D_MODEL, 2 * D_FF), 1), "ffn2_w_out": ((D_FF, D_MODEL), 0),
    "w_ple_gate": ((D_MODEL, D_MODEL), 0), "w_ple": ((PLE_DIM, D_MODEL), 1),
}
SMALL = ["ln1_g", "ln1_b", "b_gate", "a_log", "dt_bias", "gdn_norm_w", "ln2_g", "ln2_b", "ln3_g", "ln3_b",
         "b_ple_gate", "ln4_g", "ln4_b"]
WEIGHTS = ["ffn1_w_in", "ffn1_w_out", "ln1_g", "ln1_b", "w_mix_in", "b_gate", "conv_w", "a_log", "dt_bias",
           "gdn_norm_w", "w_branch_sb", "w_branch_gdn", "w_mix_out", "ln2_g", "ln2_b", "ffn2_w_in", "ffn2_w_out",
           "ln3_g", "ln3_b", "w_ple_gate", "b_ple_gate", "w_ple", "ln4_g", "ln4_b"]
PACK_COLS = 1024
ADAM_ROWS = 208


def _pick(n, cands):
    for c in cands:
        if n % c == 0:
            return c
    raise ValueError(f"no tile for {n} in {cands}")


def _matmul(name, a, b, a_spec, b_spec, out_sds, out_spec, out_block, grid, nk, trans_a=False):
    dims = (((0,), (0,)), ((), ())) if trans_a else (((1,), (0,)), ((), ()))

    def body(a_ref, b_ref, o_ref, acc_ref):
        k = pl.program_id(2)

        @pl.when(k == 0)
        def _():
            acc_ref[...] = jnp.zeros_like(acc_ref)

        acc_ref[...] += lax.dot_general(a_ref[...], b_ref[...], dims, preferred_element_type=F32)

        @pl.when(k == nk - 1)
        def _():
            o_ref[...] = acc_ref[...].astype(o_ref.dtype)

    return pl.pallas_call(
        body, name=name, grid=grid, in_specs=[a_spec, b_spec], out_specs=out_spec, out_shape=out_sds,
        scratch_shapes=[pltpu.VMEM(out_block, F32)],
        compiler_params=pltpu.CompilerParams(dimension_semantics=("parallel", "parallel", "arbitrary")),
    )(a, b)


def _mm(name, a, w, out_dtype, n=None, col_off=0):
    M, K = a.shape
    n = w.shape[1] if n is None else n
    tm = _pick(M, (512, 256, 128))
    tn = _pick(n, (512, 384, 256, 128))
    tk = _pick(K, (1024, 1408, 1152, 512, 256))
    assert col_off % tn == 0
    cb, nk = col_off // tn, K // tk
    return _matmul(
        name, a, w, pl.BlockSpec((tm, tk), lambda i, j, k: (i, k)), pl.BlockSpec((tk, tn), lambda i, j, k: (k, j + cb)),
        jax.ShapeDtypeStruct((M, n), out_dtype), pl.BlockSpec((tm, tn), lambda i, j, k: (i, j)), (tm, tn),
        (M // tm, n // tn, nk), nk)


def _mm_gu(name, a3, w, out_dtype):
    _, M, Fh = a3.shape
    N = w.shape[1]
    tm = _pick(M, (512, 256, 128))
    tn = _pick(N, (512, 256, 128))
    tk = _pick(Fh, (1408, 512, 256, 128))
    nkf = Fh // tk
    return _matmul(
        name, a3, w, pl.BlockSpec((None, tm, tk), lambda i, j, k: (k // nkf, i, k % nkf)),
        pl.BlockSpec((tk, tn), lambda i, j, k: (k, j)),
        jax.ShapeDtypeStruct((M, N), out_dtype), pl.BlockSpec((tm, tn), lambda i, j, k: (i, j)), (tm, tn),
        (M // tm, N // tn, 2 * nkf), 2 * nkf)


def _mm_tn(name, a, b):
    S, M = a.shape
    N = b.shape[1]
    tm = _pick(M, (512, 256, 128))
    tn = _pick(N, (512, 640, 256, 128))
    tk = _pick(S, (1024, 512, 256, 128))
    nk = S // tk
    return _matmul(
        name, a, b, pl.BlockSpec((tk, tm), lambda i, j, k: (k, i)), pl.BlockSpec((tk, tn), lambda i, j, k: (k, j)),
        jax.ShapeDtypeStruct((M, N), F32), pl.BlockSpec((tm, tn), lambda i, j, k: (i, j)), (tm, tn),
        (M // tm, N // tn, nk), nk, trans_a=True)


def _mm_tn_gu(name, a, b3):
    S, M = a.shape
    Fh = b3.shape[2]
    tm = _pick(M, (512, 256, 128))
    tn = _pick(Fh, (256, 128))
    tk = _pick(S, (1024, 512, 256, 128))
    nf, nk = Fh // tn, S // tk
    return _matmul(
        name, a, b3, pl.BlockSpec((tk, tm), lambda i, j, k: (k, i)),
        pl.BlockSpec((None, tk, tn), lambda i, j, k: (j // nf, k, j % nf)),
        jax.ShapeDtypeStruct((M, 2 * Fh), F32), pl.BlockSpec((tm, tn), lambda i, j, k: (i, j)), (tm, tn),
        (M // tm, 2 * nf, nk), nk, trans_a=True)


def _sigmoid(z):
    return 1.0 / (1.0 + jnp.exp(-z))


def _ffn_up(name, xb, w_in):
    S, D = xb.shape
    Fh = w_in.shape[1] // 2
    ts = _pick(S, (512, 256, 128))
    tf = _pick(Fh, (256, 128))
    nf = Fh // tf

    def body(x_ref, wg_ref, wu_ref, a_ref, gu_ref):
        x = x_ref[...]
        g = jnp.dot(x, wg_ref[...], preferred_element_type=F32)
        u = jnp.dot(x, wu_ref[...], preferred_element_type=F32)
        a_ref[...] = (g * _sigmoid(g) * u).astype(a_ref.dtype)
        gu_ref[0] = g.astype(gu_ref.dtype)
        gu_ref[1] = u.astype(gu_ref.dtype)

    return pl.pallas_call(
        body, name=name, grid=(S // ts, nf),
        in_specs=[pl.BlockSpec((ts, D), lambda i, j: (i, 0)), pl.BlockSpec((D, tf), lambda i, j: (0, j)),
                  pl.BlockSpec((D, tf), lambda i, j: (0, j + nf))],
        out_specs=[pl.BlockSpec((ts, tf), lambda i, j: (i, j)), pl.BlockSpec((2, ts, tf), lambda i, j: (0, i, j))],
        out_shape=[jax.ShapeDtypeStruct((S, Fh), BF16), jax.ShapeDtypeStruct((2, S, Fh), BF16)],
        compiler_params=pltpu.CompilerParams(dimension_semantics=("parallel", "parallel")),
    )(xb, w_in, w_in)


def _ffn_bwd_act(name, dyb, w_out_t, gu):
    S, D = dyb.shape
    Fh = w_out_t.shape[1]
    ts = _pick(S, (512, 256, 128))
    tf = _pick(Fh, (256, 128))

    def body(dy_ref, w_ref, gu_ref, o_ref):
        da = jnp.dot(dy_ref[...], w_ref[...], preferred_element_type=F32)
        g = gu_ref[0].astype(F32)
        u = gu_ref[1].astype(F32)
        sg = _sigmoid(g)
        o_ref[0] = (da * u * (sg * (1.0 + g * (1.0 - sg)))).astype(o_ref.dtype)
        o_ref[1] = (da * (g * sg)).astype(o_ref.dtype)

    return pl.pallas_call(
        body, name=name, grid=(S // ts, Fh // tf),
        in_specs=[pl.BlockSpec((ts, D), lambda i, j: (i, 0)), pl.BlockSpec((D, tf), lambda i, j: (0, j)),
                  pl.BlockSpec((2, ts, tf), lambda i, j: (0, i, j))],
        out_specs=pl.BlockSpec((2, ts, tf), lambda i, j: (0, i, j)),
        out_shape=jax.ShapeDtypeStruct((2, S, Fh), BF16),
        compiler_params=pltpu.CompilerParams(dimension_semantics=("parallel", "parallel")),
    )(dyb, w_out_t, gu)


def _rowwise(name, fn, tiled, params, outs, reds=(), ts=256):
    S = tiled[0][0].shape[0]
    ts = _pick(S, (ts, 128))
    n_t, n_p, n_o = len(tiled), len(params), len(outs)

    def body(*refs):
        vals = [r[...] for r in refs[:n_t + n_p]]
        res_o, res_r = fn(*vals)
        o_refs = refs[n_t + n_p:n_t + n_p + n_o]
        r_refs = refs[n_t + n_p + n_o:]
        for r, val in zip(o_refs, res_o, strict=True):
            r[...] = val.astype(r.dtype)
        if r_refs:
            i = pl.program_id(0)

            @pl.when(i == 0)
            def _():
                for r, val in zip(r_refs, res_r, strict=True):
                    r[...] = val.astype(F32)

            @pl.when(i > 0)
            def _():
                for r, val in zip(r_refs, res_r, strict=True):
                    r[...] += val.astype(F32)

    in_specs = [pl.BlockSpec((ts, w), functools.partial(lambda i, cb: (i, cb), cb=cb)) for _, cb, w in tiled]
    in_specs += [pl.BlockSpec(p.shape, lambda i: (0, 0)) for p in params]
    out_specs = [pl.BlockSpec((ts, w), lambda i: (i, 0)) for w, _ in outs]
    out_specs += [pl.BlockSpec(tuple(r), lambda i: (0, 0)) for r in reds]
    out_shape = [jax.ShapeDtypeStruct((S, w), dt) for w, dt in outs]
    out_shape += [jax.ShapeDtypeStruct(tuple(r), F32) for r in reds]
    res = pl.pallas_call(
        body, name=name, grid=(S // ts,), in_specs=in_specs, out_specs=out_specs, out_shape=out_shape,
        compiler_params=pltpu.CompilerParams(dimension_semantics=("arbitrary",)),
    )(*[t[0] for t in tiled], *params)
    return res[:n_o], res[n_o:]


def _ln(r, g, b):
    mu = jnp.mean(r, axis=-1, keepdims=True)
    xc = r - mu
    var = jnp.mean(xc * xc, axis=-1, keepdims=True)
    return xc * lax.rsqrt(var + LN_EPS) * g + b


def _ln_bwd(r, g, b, dy):
    _, vjp = jax.vjp(_ln, r, g, b)
    return vjp(dy)


def _shift_down(x, d):
    return x if d == 0 else pltpu.roll(x, d, 0)


def _conv_fwd(name, x, w):
    S = x.shape[0]
    ts = _pick(S, (256, 128))
    hb = ts // HALO

    def body(x_ref, prev_ref, w_ref, o_ref):
        i = pl.program_id(0)
        prev = jnp.where(i > 0, prev_ref[...], 0.0)
        xe = jnp.concatenate([prev, x_ref[...]], axis=0)
        y = jnp.zeros((ts + HALO, CONV_CH), F32)
        for j in range(CONV_K):
            y = y + w_ref[pl.ds(j, 1), :] * _shift_down(xe, CONV_K - 1 - j)
        y = y[HALO:, :]
        o_ref[...] = y * _sigmoid(y)

    return pl.pallas_call(
        body, name=name, grid=(S // ts,),
        in_specs=[pl.BlockSpec((ts, CONV_CH), lambda i: (i, 0)),
                  pl.BlockSpec((HALO, CONV_CH), lambda i: (jnp.maximum(i * hb - 1, 0), 0)),
                  pl.BlockSpec((CONV_K, CONV_CH), lambda i: (0, 0))],
        out_specs=pl.BlockSpec((ts, CONV_CH), lambda i: (i, 0)),
        out_shape=jax.ShapeDtypeStruct((S, CONV_CH), F32),
        compiler_params=pltpu.CompilerParams(dimension_semantics=("arbitrary",)),
    )(x, x, w)


def _conv_bwd(name, x, w, dout):
    S = x.shape[0]
    ts = _pick(S, (256, 128))
    hb = ts // HALO
    nt = S // ts
    n_ext = ts + 2 * HALO

    def body(x_ref, prev_ref, next_ref, w_ref, d_ref, dnext_ref, dx_ref, dw_ref):
        i = pl.program_id(0)
        prev = jnp.where(i > 0, prev_ref[...], 0.0)
        last = i == nt - 1
        nxt = jnp.where(last, 0.0, next_ref[...])
        dnxt = jnp.where(last, 0.0, dnext_ref[...])
        xe = jnp.concatenate([prev, x_ref[...], nxt], axis=0)
        de = jnp.concatenate([jnp.zeros((HALO, CONV_CH), F32), d_ref[...], dnxt], axis=0)
        y = jnp.zeros((n_ext, CONV_CH), F32)
        for j in range(CONV_K):
            y = y + w_ref[pl.ds(j, 1), :] * _shift_down(xe, CONV_K - 1 - j)
        sg = _sigmoid(y)
        dy = de * (sg * (1.0 + y * (1.0 - sg)))
        dx = jnp.zeros((n_ext, CONV_CH), F32)
        for j in range(CONV_K):
            m = CONV_K - 1 - j
            dx = dx + w_ref[pl.ds(j, 1), :] * (dy if m == 0 else pltpu.roll(dy, n_ext - m, 0))
        dx_ref[...] = dx[HALO:HALO + ts, :].astype(dx_ref.dtype)
        row = lax.broadcasted_iota(jnp.int32, (n_ext, 1), 0)
        dy_own = jnp.where((row >= HALO) & (row < HALO + ts), dy, 0.0)
        parts = [jnp.sum(dy_own * _shift_down(xe, CONV_K - 1 - j), axis=0, keepdims=True) for j in range(CONV_K)]
        dw = jnp.concatenate(parts + [jnp.zeros((HALO - CONV_K, CONV_CH), F32)], axis=0)

        @pl.when(i == 0)
        def _():
            dw_ref[...] = dw

        @pl.when(i > 0)
        def _():
            dw_ref[...] += dw

    tile = pl.BlockSpec((ts, CONV_CH), lambda i: (i, 0))
    prev = pl.BlockSpec((HALO, CONV_CH), lambda i: (jnp.maximum(i * hb - 1, 0), 0))
    nxt = pl.BlockSpec((HALO, CONV_CH), lambda i: (jnp.minimum((i + 1) * hb, nt * hb - 1), 0))
    return pl.pallas_call(
        body, name=name, grid=(nt,),
        in_specs=[tile, prev, nxt, pl.BlockSpec((CONV_K, CONV_CH), lambda i: (0, 0)), tile, nxt],
        out_specs=[tile, pl.BlockSpec((HALO, CONV_CH), lambda i: (0, 0))],
        out_shape=[jax.ShapeDtypeStruct((S, CONV_CH), BF16), jax.ShapeDtypeStruct((HALO, CONV_CH), F32)],
        compiler_params=pltpu.CompilerParams(dimension_semantics=("arbitrary",)),
    )(x, x, x, w, dout, dout)


_NT = (((1,), (1,)), ((), ()))
_TN = (((0,), (0,)), ((), ()))


def _softplus(z):
    return jnp.maximum(z, 0.0) + jnp.log(1.0 + jnp.exp(-jnp.abs(z)))


def _dot2(x, tri2):
    hi = x.astype(BF16)
    lo = (x - hi.astype(F32)).astype(BF16)
    return jnp.dot(jnp.concatenate([hi, lo], axis=1), tri2, preferred_element_type=F32)


def _pair_tri(tk, keep):
    r = lax.broadcasted_iota(jnp.int32, (4 * tk, 2 * tk), 0)
    c = lax.broadcasted_iota(jnp.int32, (4 * tk, 2 * tk), 1)
    same_head = ((r // tk) % 2) == (c // tk)
    return (same_head & keep(r % tk, c % tk)).astype(BF16)


def _pair_rows(x2, first):
    return jnp.concatenate([jnp.where(first, x2, 0), jnp.where(first, 0, x2)], axis=0)


def _pair_sum(x):
    h = x.shape[1] // 2
    return jnp.sum(x[:, :h], axis=1, keepdims=True), jnp.sum(x[:, h:], axis=1, keepdims=True)


def _sb_tiles(S):
    tq = _pick(S, (SB_TQ, 256, 128))
    return tq, SB_TK


N_PAIRS = N_HEADS // 2


def _sb_specs(S, tq):
    q = pl.BlockSpec((tq, 128), lambda p, i: (i, p))
    k = pl.BlockSpec((S, 128), lambda p, i: (0, N_PAIRS + p))
    v = pl.BlockSpec((S, 128), lambda p, i: (0, 2 * N_PAIRS + p))
    return q, k, v


def _scaled(q):
    return (q.astype(F32) * (HEAD_DIM ** -0.5)).astype(BF16)


def _sb_fwd(qkv):
    S = qkv.shape[0]
    tq, tk = _sb_tiles(S)
    nd = tq // tk

    def body(q_ref, k_ref, v_ref, o_ref, ta_ref, tb_ref):
        qi = pl.program_id(1)
        qs = _scaled(q_ref[...])
        row = qi * tq + lax.broadcasted_iota(jnp.int32, (tq, 2 * tk), 0)
        col = lax.broadcasted_iota(jnp.int32, (tq, 2 * tk), 1)
        left = col < tk
        col = col % tk
        later = _pair_tri(tk, lambda r, c: r > c)
        first = lax.broadcasted_iota(jnp.int32, (tk, 128), 1) < HEAD_DIM

        def tile(j, carry, masked):
            acc, suf_a, suf_b = carry
            off = pl.multiple_of(j * tk, tk)
            kc = _pair_rows(k_ref[pl.ds(off, tk), :], first)
            vc = _pair_rows(v_ref[pl.ds(off, tk), :], first)
            mask = (col + j * tk) < row
            z = lax.dot_general(qs, kc, _NT, preferred_element_type=F32)
            lf = -_softplus(z)
            if masked:
                lf = jnp.where(mask, lf, 0.0)
            w = jnp.exp(z + lf + _dot2(lf, later) + jnp.where(left, suf_a, suf_b))
            if masked:
                w = jnp.where(mask, w, 0.0)
            acc = acc + jnp.dot(w.astype(BF16), vc, preferred_element_type=F32)
            sum_a, sum_b = _pair_sum(lf)
            return acc, suf_a + sum_a, suf_b + sum_b

        zero1 = jnp.zeros((tq, 1), F32)
        carry = (jnp.zeros((tq, 128), F32), zero1, zero1)
        n_full = qi * nd
        carry = lax.fori_loop(0, nd, lambda d, c: tile(n_full + nd - 1 - d, c, True), carry)
        acc, suf_a, suf_b = lax.fori_loop(0, n_full, lambda jj, c: tile(n_full - 1 - jj, c, False), carry)
        o_ref[...] = acc.astype(o_ref.dtype)
        ta_ref[...] = suf_a
        tb_ref[...] = suf_b

    tot = pl.BlockSpec((None, tq, 1), lambda p, i: (p, i, 0))
    tshape = jax.ShapeDtypeStruct((N_PAIRS, S, 1), F32)
    return pl.pallas_call(
        body, name="sb_fwd", grid=(N_PAIRS, S // tq), in_specs=list(_sb_specs(S, tq)),
        out_specs=[pl.BlockSpec((tq, 128), lambda p, i: (i, p)), tot, tot],
        out_shape=[jax.ShapeDtypeStruct((S, HEAD_W), BF16), tshape, tshape],
        compiler_params=pltpu.CompilerParams(dimension_semantics=("arbitrary", "arbitrary")),
    )(qkv, qkv, qkv)


def _sb_bwd(qkv, tot_a, tot_b, do):
    S = qkv.shape[0]
    tq, tk = _sb_tiles(S)
    nd = tq // tk
    scale = HEAD_DIM ** -0.5

    def body(q_ref, k_ref, v_ref, ta_ref, tb_ref, do_ref, dq_ref, dk_ref, dv_ref):
        qi = pl.program_id(1)

        @pl.when(qi == 0)
        def _():
            dk_ref[...] = jnp.zeros_like(dk_ref)
            dv_ref[...] = jnp.zeros_like(dv_ref)

        qs = _scaled(q_ref[...])
        do2 = do_ref[...]
        row = qi * tq + lax.broadcasted_iota(jnp.int32, (tq, 2 * tk), 0)
        col = lax.broadcasted_iota(jnp.int32, (tq, 2 * tk), 1)
        left = col < tk
        col = col % tk
        total = jnp.where(left, ta_ref[...], tb_ref[...])
        upto = _pair_tri(tk, lambda r, c: r <= c)
        first = lax.broadcasted_iota(jnp.int32, (tk, 128), 1) < HEAD_DIM

        def tile(j, carry, masked):
            dq, pl_a, pl_b, pg_a, pg_b = carry
            off = pl.multiple_of(j * tk, tk)
            kc = _pair_rows(k_ref[pl.ds(off, tk), :], first)
            vc = _pair_rows(v_ref[pl.ds(off, tk), :], first)
            mask = (col + j * tk) < row
            z = lax.dot_general(qs, kc, _NT, preferred_element_type=F32)
            lf = -_softplus(z)
            sig = jnp.exp(z + lf)
            if masked:
                lf = jnp.where(mask, lf, 0.0)
            w = jnp.exp(z + lf + (total - (jnp.where(left, pl_a, pl_b) + _dot2(lf, upto))))
            if masked:
                w = jnp.where(mask, w, 0.0)
            gl = lax.dot_general(do2, vc, _NT, preferred_element_type=F32) * w
            dz = gl - sig * (jnp.where(left, pg_a, pg_b) + _dot2(gl, upto))
            if masked:
                dz = jnp.where(mask, dz, 0.0)
            dzb = dz.astype(BF16)
            dq = dq + jnp.dot(dzb, kc, preferred_element_type=F32)
            dkc = lax.dot_general(dzb, qs, _TN, preferred_element_type=F32)
            dvc = lax.dot_general(w.astype(BF16), do2, _TN, preferred_element_type=F32)
            dk_ref[pl.ds(off, tk), :] += jnp.where(first, dkc[:tk], dkc[tk:])
            dv_ref[pl.ds(off, tk), :] += jnp.where(first, dvc[:tk], dvc[tk:])
            sl_a, sl_b = _pair_sum(lf)
            sg_a, sg_b = _pair_sum(gl)
            return dq, pl_a + sl_a, pl_b + sl_b, pg_a + sg_a, pg_b + sg_b

        zero1 = jnp.zeros((tq, 1), F32)
        carry = (jnp.zeros((tq, 128), F32), zero1, zero1, zero1, zero1)
        n_full = qi * nd
        carry = lax.fori_loop(0, n_full, lambda j, c: tile(j, c, False), carry)
        dq = lax.fori_loop(0, nd, lambda d, c: tile(n_full + d, c, True), carry)[0]
        dq_ref[...] = dq * scale

    q_spec, k_spec, v_spec = _sb_specs(S, tq)
    tot = pl.BlockSpec((None, tq, 1), lambda p, i: (p, i, 0))
    whole = pl.BlockSpec((S, 128), lambda p, i: (0, p))
    big = jax.ShapeDtypeStruct((S, HEAD_W), F32)
    return pl.pallas_call(
        body, name="sb_bwd", grid=(N_PAIRS, S // tq),
        in_specs=[q_spec, k_spec, v_spec, tot, tot, q_spec],
        out_specs=[q_spec, whole, whole], out_shape=[big, big, big],
        compiler_params=pltpu.CompilerParams(dimension_semantics=("arbitrary", "arbitrary")),
    )(qkv, qkv, qkv, tot_a, tot_b, do)


def _bmm(a, b):
    return lax.dot_general(a, b, (((2,), (1,)), ((0,), (0,))), precision=HI, preferred_element_type=F32)


def _bmm_nt(a, b):
    return lax.dot_general(a, b, (((2,), (2,)), ((0,), (0,))), precision=HI, preferred_element_type=F32)


def _bmm_tn(a, b):
    return lax.dot_general(a, b, (((1,), (1,)), ((0,), (0,))), precision=HI, preferred_element_type=F32)


def _tri_inv(lower):
    C = lower.shape[-1]
    ii = lax.broadcasted_iota(jnp.int32, (C, C), 0)
    jj = lax.broadcasted_iota(jnp.int32, (C, C), 1)
    eye = (ii == jj).astype(F32)[None]
    xd = jnp.where((ii // 8 == jj // 8)[None], -lower, 0.0)
    x2 = _bmm(xd, xd)
    x4 = _bmm(x2, x2)
    inv = eye + xd
    inv = inv + _bmm(inv, x2)
    inv = inv + _bmm(inv, x4)
    b = 8
    while b < C:
        off = jnp.where(((ii // (2 * b) == jj // (2 * b)) & (ii // b != jj // b))[None], lower, 0.0)
        inv = inv - _bmm(inv, _bmm(off, inv))
        b *= 2
    return inv


@jax.custom_vjp
def _tri_solve(lower, rhs):
    return _bmm(_tri_inv(lower), rhs)


def _tri_solve_fwd(lower, rhs):
    inv = _tri_inv(lower)
    sol = _bmm(inv, rhs)
    return sol, (inv, sol)


def _tri_solve_bwd(res, dsol):
    inv, sol = res
    drhs = _bmm_tn(inv, dsol)
    C = inv.shape[-1]
    ii = lax.broadcasted_iota(jnp.int32, (C, C), 0)
    jj = lax.broadcasted_iota(jnp.int32, (C, C), 1)
    return jnp.where((jj < ii)[None], -_bmm_nt(drhs, sol), 0.0), drhs


_tri_solve.defvjp(_tri_solve_fwd, _tri_solve_bwd)


def _pairs(x):
    return jnp.stack([x[:, 128 * p:128 * (p + 1)] for p in range(N_PAIRS)], axis=0)


def _unpairs(x):
    return jnp.concatenate([x[p] for p in range(N_PAIRS)], axis=1)


def _gdn_chunk(state, qkv, gz, ba, a_log_x, dt_x, norm_x):
    C = qkv.shape[0]
    lane = lax.broadcasted_iota(jnp.int32, (1, 1, 128), 2)
    first = lane < HEAD_DIM

    def split_heads(x2):
        return jnp.stack([jnp.where(first, x2, 0.0), jnp.where(first, 0.0, x2)], axis=1).reshape(N_HEADS, *x2.shape[1:])

    def merge_heads(xh):
        x = xh.reshape(N_PAIRS, 2, *xh.shape[1:])
        return x[:, 0] + x[:, 1]

    def head_cols(x2):
        a = jnp.sum(jnp.where(lane == 0, x2, 0.0), axis=-1, keepdims=True)
        b = jnp.sum(jnp.where(lane == HEAD_DIM, x2, 0.0), axis=-1, keepdims=True)
        return jnp.stack([a, b], axis=1).reshape(N_HEADS, *a.shape[1:])

    def to_pair(xh):
        x = xh.reshape(N_PAIRS, 2, *xh.shape[1:])
        return jnp.where(first, x[:, 0], x[:, 1])

    def head_sums(x2):
        a = jnp.sum(jnp.where(first, x2, 0.0), axis=-1, keepdims=True)
        b = jnp.sum(jnp.where(first, 0.0, x2), axis=-1, keepdims=True)
        return jnp.where(first, a, b)

    er = lax.broadcasted_iota(jnp.int32, (128, 2 * HEAD_W), 0)
    ec = lax.broadcasted_iota(jnp.int32, (128, 2 * HEAD_W), 1)
    spread = (er == ec // HEAD_DIM).astype(F32)
    bx = lax.dot_general(ba, spread, (((1,), (0,)), ((), ())), precision=HI, preferred_element_type=F32)
    beta2 = _pairs(_sigmoid(bx[:, :HEAD_W]))
    g2 = _pairs(-jnp.exp(a_log_x) * _softplus(bx[:, HEAD_W:] + dt_x))
    beta = head_cols(beta2)
    g = head_cols(g2)

    q2, k2, v2 = (_pairs(qkv[:, i * HEAD_W:(i + 1) * HEAD_W]) for i in range(3))
    qn2 = q2 * lax.rsqrt(head_sums(q2 * q2) + RMS_EPS) * (HEAD_DIM ** -0.5)
    kn2 = k2 * lax.rsqrt(head_sums(k2 * k2) + RMS_EPS)
    knh = split_heads(kn2)

    ii = lax.broadcasted_iota(jnp.int32, (C, C), 0)[None]
    jj = lax.broadcasted_iota(jnp.int32, (C, C), 1)[None]
    incl = jj <= ii
    g_row = jnp.sum(jnp.where(ii == jj, g, 0.0), axis=1, keepdims=True)
    gc_col = jnp.sum(jnp.where(incl, g_row, 0.0), axis=2, keepdims=True)
    gc_row = jnp.sum(jnp.where(ii <= jj, g, 0.0), axis=1, keepdims=True)
    decay = jnp.where(incl, jnp.exp(jnp.where(incl, gc_col - gc_row, 0.0)), 0.0)
    lower = jnp.where(jj < ii, beta * _bmm_nt(knh, knh) * decay, 0.0)
    gc2 = to_pair(gc_col)
    egc2 = jnp.exp(gc2)
    u2 = merge_heads(_tri_solve(lower, split_heads(v2 * beta2)))
    w2 = merge_heads(_tri_solve(lower, split_heads(kn2 * (beta2 * egc2))))
    qk = jnp.where(incl, _bmm_nt(jnp.repeat(qn2, 2, axis=0), knh) * decay, 0.0)
    g_last2 = to_pair(jnp.sum(g, axis=1, keepdims=True))
    v_new2 = u2 - _bmm(w2, state)
    o2 = _bmm(qn2 * egc2, state) + merge_heads(_bmm(qk, split_heads(v_new2)))
    sr = lax.broadcasted_iota(jnp.int32, (128, 128), 0)
    sc = lax.broadcasted_iota(jnp.int32, (128, 128), 1)
    same_head = ((sr < HEAD_DIM) == (sc < HEAD_DIM))[None]
    new_state = state * jnp.exp(g_last2) + jnp.where(same_head, _bmm_tn(kn2 * jnp.exp(g_last2 - gc2), v_new2), 0.0)
    o2 = o2 * lax.rsqrt(head_sums(o2 * o2) * (1.0 / HEAD_DIM) + RMS_EPS) * _pairs(norm_x)
    gz2 = _pairs(gz)
    return new_state, _unpairs(o2 * (gz2 * _sigmoid(gz2)))


def _gdn_specs(order):
    C = GDN_CHUNK
    par = pl.BlockSpec((1, HEAD_W), lambda c: (0, 0))
    return [pl.BlockSpec((C, CONV_CH), lambda c: (order(c), 0)), pl.BlockSpec((C, HEAD_W), lambda c: (order(c), 3)),
            pl.BlockSpec((C, 128), lambda c: (order(c), 0)), par, par, par]


def _gdn_fwd(qkv, gdnp, ba, a_log_x, dt_x, norm_x):
    S = qkv.shape[0]
    C = GDN_CHUNK
    n = S // C

    def body(qkv_ref, gz_ref, ba_ref, al_ref, dt_ref, nw_ref, o_ref, st_ref, state):
        @pl.when(pl.program_id(0) == 0)
        def _():
            state[...] = jnp.zeros_like(state)

        st = state[...]
        st_ref[...] = st
        new, o = _gdn_chunk(st, qkv_ref[...], gz_ref[...], ba_ref[...], al_ref[...], dt_ref[...], nw_ref[...])
        state[...] = new
        o_ref[...] = o.astype(o_ref.dtype)

    return pl.pallas_call(
        body, name="gdn_fwd", grid=(n,), in_specs=_gdn_specs(lambda c: c),
        out_specs=[pl.BlockSpec((C, HEAD_W), lambda c: (c, 0)),
                   pl.BlockSpec((None, N_PAIRS, 128, 128), lambda c: (c, 0, 0, 0))],
        out_shape=[jax.ShapeDtypeStruct((S, HEAD_W), BF16), jax.ShapeDtypeStruct((n, N_PAIRS, 128, 128), F32)],
        scratch_shapes=[pltpu.VMEM((N_PAIRS, 128, 128), F32)],
        compiler_params=pltpu.CompilerParams(dimension_semantics=("arbitrary",)),
    )(qkv, gdnp, ba, a_log_x, dt_x, norm_x)


def _gdn_bwd(qkv, gdnp, ba, a_log_x, dt_x, norm_x, states, do):
    S = qkv.shape[0]
    C = GDN_CHUNK
    n = S // C

    def body(qkv_ref, gz_ref, ba_ref, al_ref, dt_ref, nw_ref, st_ref, do_ref,
             dqkv_ref, dgz_ref, dba_ref, dal_ref, ddt_ref, dnw_ref, dstate):
        @pl.when(pl.program_id(0) == 0)
        def _():
            dstate[...] = jnp.zeros_like(dstate)
            dal_ref[...] = jnp.zeros_like(dal_ref)
            ddt_ref[...] = jnp.zeros_like(ddt_ref)
            dnw_ref[...] = jnp.zeros_like(dnw_ref)

        args = (st_ref[...], qkv_ref[...], gz_ref[...], ba_ref[...], al_ref[...], dt_ref[...], nw_ref[...])
        _, vjp = jax.vjp(_gdn_chunk, *args)
        dst, dqkv, dgz, dba, dal, ddt, dnw = vjp((dstate[...], do_ref[...]))
        dstate[...] = dst
        dqkv_ref[...] = dqkv
        dgz_ref[...] = dgz.astype(dgz_ref.dtype)
        dba_ref[...] = dba
        dal_ref[...] += dal
        ddt_ref[...] += ddt
        dnw_ref[...] += dnw

    rev = lambda c: n - 1 - c
    par = pl.BlockSpec((1, HEAD_W), lambda c: (0, 0))
    f = jax.ShapeDtypeStruct
    return pl.pallas_call(
        body, name="gdn_bwd", grid=(n,),
        in_specs=_gdn_specs(rev) + [pl.BlockSpec((None, N_PAIRS, 128, 128), lambda c: (rev(c), 0, 0, 0)),
                                    pl.BlockSpec((C, HEAD_W), lambda c: (rev(c), 0))],
        out_specs=[pl.BlockSpec((C, CONV_CH), lambda c: (rev(c), 0)), pl.BlockSpec((C, HEAD_W), lambda c: (rev(c), 0)),
                   pl.BlockSpec((C, 128), lambda c: (rev(c), 0)), par, par, par],
        out_shape=[f((S, CONV_CH), F32), f((S, HEAD_W), BF16), f((S, 128), F32)] + [f((1, HEAD_W), F32)] * 3,
        scratch_shapes=[pltpu.VMEM((N_PAIRS, 128, 128), F32)],
        compiler_params=pltpu.CompilerParams(dimension_semantics=("arbitrary",)),
    )(qkv, gdnp, ba, a_log_x, dt_x, norm_x, states, do)


def _local_step(x, p, target, W, P):
    S = x.shape[0]
    D = D_MODEL
    row = lambda a: a.reshape(1, -1)
    xb = x.astype(BF16)
    pb = p.astype(BF16)
    Wt = {n: W[n].T for n in ("ffn1_w_in", "ffn1_w_out", "w_mix_pad", "w_branch_sb", "w_branch_gdn", "w_mix_out",
                              "ffn2_w_in", "ffn2_w_out", "w_ple_gate")}

    def ffn_fwd(tag, h, hb, w_in, w_out, g, b):
        a, gu = _ffn_up(f"{tag}_up", hb, w_in)
        f = _mm(f"{tag}_down", a, w_out, F32)

        def fn(h, f, g, b):
            r = ALPHA * h + 0.5 * f
            y = _ln(r, g, b)
            return (r, y, y), ()

        (r, y, yb), _ = _rowwise(f"{tag}_ln", fn, [(h, 0, D), (f, 0, D)], [g, b], [(D, F32), (D, F32), (D, BF16)])
        return a, gu, r, y, yb

    a1, gu1, r1, h1, h1b = ffn_fwd("ffn1", x, xb, W["ffn1_w_in"], W["ffn1_w_out"], P["ln1_g"], P["ln1_b"])

    wmix = W["w_mix_pad"]
    sbp = _mm("mix_sb", h1b, wmix, BF16, n=3 * HEAD_W, col_off=MIX_SB)
    gdnp = _mm("mix_gdn", h1b, wmix, F32, n=4 * HEAD_W, col_off=MIX_GDN)
    gates = _mm("mix_gates", h1b, wmix, F32, n=2 * D, col_off=MIX_GATES)
    ba = _mm("mix_ba", h1b, wmix, F32, n=128, col_off=MIX_BA)

    attb, sb_tot_a, sb_tot_b = _sb_fwd(sbp)
    y_sb = _mm("sb_out", attb, W["w_branch_sb"], F32)

    conv_w = P["conv_w"]
    qkv = _conv_fwd("conv_fwd", gdnp, conv_w)
    a_log_x = jnp.repeat(P["a_log"], HEAD_DIM, axis=1)
    dt_x = jnp.repeat(P["dt_bias"], HEAD_DIM, axis=1)
    norm_x = jnp.tile(P["gdn_norm_w"], (1, N_HEADS))
    gob, states = _gdn_fwd(qkv, gdnp, ba, a_log_x, dt_x, norm_x)
    y_gdn = _mm("gdn_out", gob, W["w_branch_gdn"], F32)

    def merge_fn(gs, gg, ys, yg, bs, bg):
        return ((_sigmoid(gs + bs) * ys + _sigmoid(gg + bg) * yg),), ()

    b_gate = P["b_gate"]
    bs, bg = b_gate[:, :D], b_gate[:, D:]
    (merged,), _ = _rowwise("mix_merge", merge_fn, [(gates, 0, D), (gates, 1, D), (y_sb, 0, D), (y_gdn, 0, D)], [bs, bg],
                            [(D, BF16)])
    mix = _mm("mix_out", merged, W["w_mix_out"], F32)

    def ln2_fn(h, f, g, b):
        r = ALPHA * h + f
        y = _ln(r, g, b)
        return (r, y, y), ()

    (r2, h2, h2b), _ = _rowwise("mix_ln", ln2_fn, [(h1, 0, D), (mix, 0, D)], [P["ln2_g"], P["ln2_b"]],
                                [(D, F32), (D, F32), (D, BF16)])

    a2, gu2, r3, h3, h3b = ffn_fwd("ffn2", h2, h2b, W["ffn2_w_in"], W["ffn2_w_out"], P["ln3_g"], P["ln3_b"])

    zg = _mm("ple_gate", h3b, W["w_ple_gate"], F32)
    pp = _mm("ple_proj", pb, W["w_ple"], F32)

    def ple(h, zg, pp, bp, g, b):
        return _ln(ALPHA * h + _sigmoid(zg + bp) * pp, g, b)

    def head_fn(h, zg, pp, tgt, bp, g, b):
        y, vjp = jax.vjp(ple, h, zg, pp, bp, g, b)
        err = y - tgt
        dh, dzg, dpp, dbp, dg, db = vjp(err * (1.0 / D))
        loss = 0.5 * jnp.sum(jnp.sum(err * err, axis=1, keepdims=True), axis=0, keepdims=True) * (1.0 / D)
        return (dh, dzg, dpp), (loss, dbp, dg, db)

    (dh3_a, dzg, dpp), (loss, d_bple, d_ln4g, d_ln4b) = _rowwise(
        "ple_head", head_fn, [(h3, 0, D), (zg, 0, D), (pp, 0, D), (target, 0, D)],
        [P["b_ple_gate"], P["ln4_g"], P["ln4_b"]], [(D, F32), (D, BF16), (D, BF16)],
        [(1, 1), (1, D), (1, D), (1, D)])

    grads, small = {}, {"b_ple_gate": d_bple, "ln4_g": d_ln4g, "ln4_b": d_ln4b}
    grads["w_ple_gate"] = _mm_tn("d_w_ple_gate", h3b, dzg)
    grads["w_ple"] = _mm_tn("d_w_ple", pb, dpp)
    dh3_b = _mm("d_ple_gate_in", dzg, Wt["w_ple_gate"], F32)

    def ffn_bwd(tag, dy_parts, r, g, b, a, gu, hb_in, w_in_t, w_out_t):
        n_parts = len(dy_parts)

        def fn(*vals):
            dy = vals[0]
            for extra in vals[1:n_parts]:
                dy = dy + extra
            r, g, b = vals[n_parts:]
            dr, dg, db = _ln_bwd(r, g, b, dy)
            return (ALPHA * dr, 0.5 * dr), (dg, db)

        (dh_res, dfb), (dg, db) = _rowwise(f"{tag}_ln_bwd", fn, [(t, 0, D) for t in dy_parts] + [(r, 0, D)], [g, b],
                                           [(D, F32), (D, BF16)], [(1, D), (1, D)])
        dgu = _ffn_bwd_act(f"{tag}_act_bwd", dfb, w_out_t, gu)
        d_w_out = _mm_tn(f"d_{tag}_w_out", a, dfb)
        d_w_in = _mm_tn_gu(f"d_{tag}_w_in", hb_in, dgu)
        dh_ffn = _mm_gu(f"{tag}_in_bwd", dgu, w_in_t, F32)
        return dh_res, dh_ffn, d_w_in, d_w_out, dg, db

    dh2_a, dh2_b, grads["ffn2_w_in"], grads["ffn2_w_out"], small["ln3_g"], small["ln3_b"] = ffn_bwd(
        "ffn2", [dh3_a, dh3_b], r3, P["ln3_g"], P["ln3_b"], a2, gu2, h2b, Wt["ffn2_w_in"], Wt["ffn2_w_out"])

    def ln2_bwd_fn(d1, d2, r, g, b):
        dr, dg, db = _ln_bwd(r, g, b, d1 + d2)
        return (ALPHA * dr, dr), (dg, db)

    (dh1_a, dmixb), (small["ln2_g"], small["ln2_b"]) = _rowwise(
        "mix_ln_bwd", ln2_bwd_fn, [(dh2_a, 0, D), (dh2_b, 0, D), (r2, 0, D)], [P["ln2_g"], P["ln2_b"]],
        [(D, F32), (D, BF16)], [(1, D), (1, D)])
    grads["w_mix_out"] = _mm_tn("d_w_mix_out", merged, dmixb)
    dmerged = _mm("mix_out_bwd", dmixb, Wt["w_mix_out"], F32)

    def merge_bwd_fn(dm, gs, gg, ys, yg, bs, bg):
        ss, sg = _sigmoid(gs + bs), _sigmoid(gg + bg)
        dgs = dm * ys * ss * (1.0 - ss)
        dgg = dm * yg * sg * (1.0 - sg)
        return (dgs, dgg, dm * ss, dm * sg), (jnp.sum(dgs, axis=0, keepdims=True), jnp.sum(dgg, axis=0, keepdims=True))

    (dgs, dgg, dy_sb, dy_gdn), (d_bs, d_bg) = _rowwise(
        "mix_merge_bwd", merge_bwd_fn, [(dmerged, 0, D), (gates, 0, D), (gates, 1, D), (y_sb, 0, D), (y_gdn, 0, D)],
        [bs, bg], [(D, BF16)] * 4, [(1, D), (1, D)])
    small["b_gate"] = jnp.concatenate([d_bs, d_bg], axis=1)

    grads["w_branch_sb"] = _mm_tn("d_w_branch_sb", attb, dy_sb)
    datt = _mm("sb_out_bwd", dy_sb, Wt["w_branch_sb"], BF16)
    dsq, dsk, dsv = _sb_bwd(sbp, sb_tot_a, sb_tot_b, datt)

    grads["w_branch_gdn"] = _mm_tn("d_w_branch_gdn", gob, dy_gdn)
    dgo = _mm("gdn_out_bwd", dy_gdn, Wt["w_branch_gdn"], F32)
    dqkv, dgz, dba, d_alog_x, d_dt_x, d_norm_x = _gdn_bwd(qkv, gdnp, ba, a_log_x, dt_x, norm_x, states, dgo)
    small["a_log"] = jnp.sum(d_alog_x.reshape(N_HEADS, HEAD_DIM), axis=1).reshape(1, N_HEADS)
    small["dt_bias"] = jnp.sum(d_dt_x.reshape(N_HEADS, HEAD_DIM), axis=1).reshape(1, N_HEADS)
    small["gdn_norm_w"] = jnp.sum(d_norm_x.reshape(N_HEADS, HEAD_DIM), axis=0).reshape(1, HEAD_DIM)
    dconv_in, d_conv_w = _conv_bwd("conv_bwd", gdnp, conv_w, dqkv)
    grads["conv_w"] = d_conv_w[:CONV_K]

    dproj = jnp.concatenate([dsq.astype(BF16), dsk.astype(BF16), dsv.astype(BF16), dconv_in, dgz, dgs, dgg,
                             dba.astype(BF16)], axis=1)
    d_wmix = _mm_tn("d_w_mix_in", h1b, dproj)
    grads["w_mix_in"] = jnp.concatenate([d_wmix[:, :MIX_GATES], d_wmix[:, MIX_BA:MIX_BA + 2 * N_HEADS],
                                         d_wmix[:, MIX_GATES:MIX_BA]], axis=1)
    dh1_b = _mm("mix_in_bwd", dproj, Wt["w_mix_pad"], F32)

    dx_a, dx_b, grads["ffn1_w_in"], grads["ffn1_w_out"], small["ln1_g"], small["ln1_b"] = ffn_bwd(
        "ffn1", [dh1_a, dh1_b], r1, P["ln1_g"], P["ln1_b"], a1, gu1, xb, Wt["ffn1_w_in"], Wt["ffn1_w_out"])

    (grad_x,), _ = _rowwise("grad_x", lambda a, b: ((a + b,), ()), [(dx_a, 0, D), (dx_b, 0, D)], [], [(D, F32)])
    return loss[0, 0], grad_x, grads, small


def _coords():
    return lax.axis_index("x"), lax.axis_index("y"), lax.axis_index("c")


def _all_gather(name, shard):
    R, C = shard.shape

    def body(x_ref, out_ref, send_sems, recv_sems, local_sem):
        x, y, c = _coords()
        me, sibling = (x, y, c), (x, y, 1 - c)
        chips = [(1 - x, y), (x, 1 - y), (1 - x, 1 - y)]

        def blk(px, py, pc):
            return out_ref.at[4 * px + 2 * py + pc]

        def copy(k, block, to, src=None):
            return pltpu.make_async_remote_copy(
                src_ref=blk(*block) if src is None else src, dst_ref=blk(*block),
                send_sem=send_sems.at[k], recv_sem=recv_sems.at[k], device_id=to, device_id_type=MESH)

        mine = pltpu.make_async_copy(x_ref, blk(*me), local_sem)
        mine.start()
        first = [copy(0, me, sibling, src=x_ref)]
        first += [copy(1 + j, me, (*chip, c), src=x_ref) for j, chip in enumerate(chips)]
        for cp in first:
            cp.start()
        passed = [copy(4 + j, (*chip, c), sibling) for j, chip in enumerate(chips)]
        for j, chip in enumerate(chips):
            copy(1 + j, (*chip, c), me).wait_recv()
            passed[j].start()
        copy(0, sibling, me).wait_recv()
        for j, chip in enumerate(chips):
            copy(4 + j, (*chip, 1 - c), me).wait_recv()
        for cp in first + passed:
            cp.wait_send()
        mine.wait()

    return pl.pallas_call(
        body, name=name, out_shape=jax.ShapeDtypeStruct((N_DEV, R, C), shard.dtype),
        in_specs=[pl.BlockSpec(memory_space=pl.ANY)], out_specs=pl.BlockSpec(memory_space=pl.ANY),
        scratch_shapes=[pltpu.SemaphoreType.DMA((7,)), pltpu.SemaphoreType.DMA((7,)), pltpu.SemaphoreType.DMA],
    )(shard)


def _all_to_all(name, src):
    def body(src_ref, dst_ref, send_sems, recv_sems, local_sem):
        x, y, c = _coords()
        me = 4 * x + 2 * y + c
        mine = pltpu.make_async_copy(src_ref.at[me], dst_ref.at[me], local_sem)
        mine.start()
        sends, recvs = [], []
        for k in range(1, N_DEV):
            px = 1 - x if k & 4 else x
            py = 1 - y if k & 2 else y
            pc = 1 - c if k & 1 else c
            peer = 4 * px + 2 * py + pc
            sends.append(pltpu.make_async_remote_copy(
                src_ref=src_ref.at[peer], dst_ref=dst_ref.at[me], send_sem=send_sems.at[k - 1],
                recv_sem=recv_sems.at[k - 1], device_id=(px, py, pc), device_id_type=MESH))
            recvs.append(pltpu.make_async_remote_copy(
                src_ref=src_ref.at[me], dst_ref=dst_ref.at[peer], send_sem=send_sems.at[k - 1],
                recv_sem=recv_sems.at[k - 1], device_id=(px, py, pc), device_id_type=MESH))
        for cp in sends:
            cp.start()
        for cp in recvs:
            cp.wait_recv()
        for cp in sends:
            cp.wait_send()
        mine.wait()

    return pl.pallas_call(
        body, name=name, out_shape=jax.ShapeDtypeStruct(src.shape, src.dtype),
        in_specs=[pl.BlockSpec(memory_space=pl.ANY)], out_specs=pl.BlockSpec(memory_space=pl.ANY),
        scratch_shapes=[pltpu.SemaphoreType.DMA((7,)), pltpu.SemaphoreType.DMA((7,)), pltpu.SemaphoreType.DMA],
    )(src)


def _adamw(name, parts, w, m, v, tr):
    R = w.shape[0]
    c1 = 1.0 - ADAM_B1 ** ADAM_STEP
    c2 = 1.0 - ADAM_B2 ** ADAM_STEP

    def body(p_ref, w_ref, m_ref, v_ref, g_ref, d_ref, nm_ref, nv_ref):
        g = p_ref[0].astype(F32)
        for d in range(1, N_DEV):
            g = g + p_ref[d].astype(F32)
        nm = ADAM_B1 * m_ref[...] + (1.0 - ADAM_B1) * g
        nv = ADAM_B2 * v_ref[...] + (1.0 - ADAM_B2) * (g * g)
        g_ref[...] = g
        nm_ref[...] = nm
        nv_ref[...] = nv
        d_ref[...] = -ADAM_LR * ((nm / c1) / (jnp.sqrt(nv / c2) + ADAM_EPS) + ADAM_WD * w_ref[...])

    t = pl.BlockSpec((tr, PACK_COLS), lambda i: (i, 0))
    o = jax.ShapeDtypeStruct((R, PACK_COLS), F32)
    return pl.pallas_call(
        body, name=name, grid=(R // tr,),
        in_specs=[pl.BlockSpec((N_DEV, tr, PACK_COLS), lambda i: (0, i, 0)), t, t, t],
        out_specs=[t, t, t, t], out_shape=[o, o, o, o],
        compiler_params=pltpu.CompilerParams(dimension_semantics=("parallel",)),
    )(parts, w, m, v)


def _pack_rows(flats, rows):
    cat = jnp.concatenate(flats, axis=-1)
    pad = rows * PACK_COLS - cat.shape[-1]
    cat = jnp.pad(cat, [(0, 0)] * (cat.ndim - 1) + [(0, pad)])
    return cat.reshape(cat.shape[:-1] + (rows, PACK_COLS))


def _shard_shape(name):
    full, ax = BIG_SHAPES[name]
    s = list(full)
    s[ax] //= N_DEV
    return tuple(s)


def _to_shards(name, full):
    shape, ax = BIG_SHAPES[name]
    if ax == 0:
        return full.reshape(N_DEV, -1)
    r, cdim = shape
    return full.reshape(r, N_DEV, cdim // N_DEV).transpose(1, 0, 2).reshape(N_DEV, -1)


def _from_shards(name, sh):
    shape, ax = BIG_SHAPES[name]
    if ax == 0:
        return sh.reshape(shape)
    r, cdim = shape
    return sh.reshape(N_DEV, r, cdim // N_DEV).transpose(1, 0, 2).reshape(shape)


def _big_rows():
    n = sum(math.prod(_shard_shape(k)) for k in BIG) + math.prod(_shard_shape("conv_w"))
    rows = -(-n // PACK_COLS)
    return -(-rows // ADAM_ROWS) * ADAM_ROWS


def kernel(x, p, ffn1_w_in, ffn1_w_out, ln1_g, ln1_b, w_mix_in, b_gate, conv_w, a_log, dt_bias, gdn_norm_w, w_branch_sb, w_branch_gdn, w_mix_out, ln2_g, ln2_b, ffn2_w_in, ffn2_w_out, ln3_g, ln3_b, w_ple_gate, b_ple_gate, w_ple, ln4_g, ln4_b, loss_target, m_ffn1_w_in, m_ffn1_w_out, m_ln1_g, m_ln1_b, m_w_mix_in, m_b_gate, m_conv_w, m_a_log, m_dt_bias, m_gdn_norm_w, m_w_branch_sb, m_w_branch_gdn, m_w_mix_out, m_ln2_g, m_ln2_b, m_ffn2_w_in, m_ffn2_w_out, m_ln3_g, m_ln3_b, m_w_ple_gate, m_b_ple_gate, m_w_ple, m_ln4_g, m_ln4_b, v_ffn1_w_in, v_ffn1_w_out, v_ln1_g, v_ln1_b, v_w_mix_in, v_b_gate, v_conv_w, v_a_log, v_dt_bias, v_gdn_norm_w, v_w_branch_sb, v_w_branch_gdn, v_w_mix_out, v_ln2_g, v_ln2_b, v_ffn2_w_in, v_ffn2_w_out, v_ln3_g, v_ln3_b, v_w_ple_gate, v_b_ple_gate, v_w_ple, v_ln4_g, v_ln4_b):
    given = dict(locals())
    w_loc = {n: given[n][0] for n in WEIGHTS}
    m_loc = {n: given["m_" + n][0] for n in WEIGHTS}
    v_loc = {n: given["v_" + n][0] for n in WEIGHTS}
    sizes = {n: w_loc[n].size for n in WEIGHTS}
    big_rows = _big_rows()

    conv_hi = conv_w[0].astype(BF16)
    conv_lo = (conv_w[0] - conv_hi.astype(F32)).astype(BF16)
    shard_bf = _pack_rows([w_loc[n].astype(BF16).reshape(-1) for n in BIG] + [conv_lo.reshape(-1)], big_rows)
    gathered = _all_gather("gather_weights", shard_bf).reshape(N_DEV, -1)
    W, off = {}, 0
    for n in BIG:
        W[n] = _from_shards(n, gathered[:, off:off + sizes[n]])
        off += sizes[n]
    conv_lo_full = _from_shards("conv_w", gathered[:, off:off + sizes["conv_w"]])
    wm = W.pop("w_mix_in")
    W["w_mix_pad"] = jnp.concatenate([wm[:, :7 * HEAD_W], wm[:, 7 * HEAD_W + 2 * N_HEADS:],
                                      wm[:, 7 * HEAD_W:7 * HEAD_W + 2 * N_HEADS],
                                      jnp.zeros((D_MODEL, 128 - 2 * N_HEADS), BF16)], axis=1)
    P = {n: w_loc[n].reshape(1, -1) for n in SMALL}
    P["conv_w"] = W.pop("conv_w").astype(F32) + conv_lo_full.astype(F32)

    loss, grad_x, grads, small = _local_step(x[0], p[0, 0], loss_target[0], W, P)
    loss = lax.psum(loss, ("x", "y", "c"))

    send = _pack_rows([_to_shards(n, grads[n]).astype(BF16) for n in BIG], big_rows)
    parts = _all_to_all("scatter_grads", send)
    small_rows = 16
    small_parts = _all_gather("gather_small_grads", _pack_rows([small[n].reshape(-1) for n in SMALL], small_rows))

    pack = lambda d, names, rows: _pack_rows([d[n].reshape(-1) for n in names], rows)
    big_out = _adamw("adamw_big", parts, pack(w_loc, BIG, big_rows), pack(m_loc, BIG, big_rows),
                     pack(v_loc, BIG, big_rows), ADAM_ROWS)
    small_out = _adamw("adamw_small", small_parts, pack(w_loc, SMALL, small_rows), pack(m_loc, SMALL, small_rows),
                       pack(v_loc, SMALL, small_rows), small_rows)

    def unpack(flat, names):
        res, off = {}, 0
        for n in names:
            res[n] = flat[off:off + sizes[n]].reshape(given[n].shape)
            off += sizes[n]
        return res

    outs = []
    for bo, so in zip(big_out, small_out, strict=True):
        d = {**unpack(bo.reshape(-1), BIG), **unpack(so.reshape(-1), SMALL)}
        outs.append([d[n] for n in WEIGHTS])
    g_out, d_out, nm_out, nv_out = outs
    return (loss, grad_x[None], *g_out, *d_out, *nm_out, *nv_out)
```

```python
import functools
import math

import jax
import jax.numpy as jnp
from jax import lax
from jax.experimental import pallas as pl
from jax.experimental.pallas import tpu as pltpu

F32 = jnp.float32
BF16 = jnp.bfloat16
MESH = pl.DeviceIdType.MESH
HI = lax.Precision.HIGHEST

N_DEV = 8
D_MODEL = 1024
D_FF = 2816
PLE_DIM = 256
N_HEADS = 8
HEAD_DIM = 64
HEAD_W = N_HEADS * HEAD_DIM
GDN_CHUNK = 64
CONV_K = 4
CONV_CH = 3 * HEAD_W
ALPHA = 2.0 ** 0.25
LN_EPS = 1e-5
RMS_EPS = 1e-6
MIX_SB = 0
MIX_GDN = 3 * HEAD_W
MIX_GATES = MIX_GDN + 4 * HEAD_W
MIX_BA = MIX_GATES + 2 * D_MODEL
MIX_PAD = MIX_BA + 128
N_IN = 7 * HEAD_W + 2 * N_HEADS + 2 * D_MODEL

ADAM_LR = 0.001
ADAM_B1 = 0.9
ADAM_B2 = 0.999
ADAM_EPS = 1e-08
ADAM_WD = 0.01
ADAM_STEP = 10

SB_TQ = 512
SB_TK = 128
HALO = 8

BIG = ["ffn1_w_in", "ffn1_w_out", "w_mix_in", "conv_w", "w_branch_sb", "w_branch_gdn", "w_mix_out",
       "ffn2_w_in", "ffn2_w_out", "w_ple_gate", "w_ple"]
BIG_SHAPES = {
    "ffn1_w_in": ((D_MODEL, 2 * D_FF), 1), "ffn1_w_out": ((D_FF, D_MODEL), 0),
    "w_mix_in": ((D_MODEL, N_IN), 1), "conv_w": ((CONV_K, CONV_CH), 1),
    "w_branch_sb": ((HEAD_W, D_MODEL), 1), "w_branch_gdn": ((HEAD_W, D_MODEL), 1),
    "w_mix_out": ((D_MODEL, D_MODEL), 0),
    "ffn2_w_in": ((D_MODEL, 2 * D_FF), 1), "ffn2_w_out": ((D_FF, D_MODEL), 0),
    "w_ple_gate": ((D_MODEL, D_MODEL), 0), "w_ple": ((PLE_DIM, D_MODEL), 1),
}
SMALL = ["ln1_g", "ln1_b", "b_gate", "a_log", "dt_bias", "gdn_norm_w", "ln2_g", "ln2_b", "ln3_g", "ln3_b",
         "b_ple_gate", "ln4_g", "ln4_b"]
WEIGHTS = ["ffn1_w_in", "ffn1_w_out", "ln1_g", "ln1_b", "w_mix_in", "b_gate", "conv_w", "a_log", "dt_bias",
           "gdn_norm_w", "w_branch_sb", "w_branch_gdn", "w_mix_out", "ln2_g", "ln2_b", "ffn2_w_in", "ffn2_w_out",
           "ln3_g", "ln3_b", "w_ple_gate", "b_ple_gate", "w_ple", "ln4_g", "ln4_b"]
PACK_COLS = 1024
ADAM_ROWS = 208


def _pick(n, cands):
    for c in cands:
        if n % c == 0:
            return c
    raise ValueError(f"no tile for {n} in {cands}")


_NN = (((1,), (0,)), ((), ()))
_NT = (((1,), (1,)), ((), ()))
_TN = (((0,), (0,)), ((), ()))
SHARD_FF = 2 * D_FF // N_DEV


def _matmul(name, dims, a, b, a_spec, b_spec, out_sds, out_spec, out_block, grid, nk):
    def body(a_ref, b_ref, o_ref, acc_ref):
        k = pl.program_id(2)

        @pl.when(k == 0)
        def _():
            acc_ref[...] = jnp.zeros_like(acc_ref)

        acc_ref[...] += lax.dot_general(a_ref[...], b_ref[...], dims, preferred_element_type=F32)

        @pl.when(k == nk - 1)
        def _():
            o_ref[...] = acc_ref[...].astype(o_ref.dtype)

    return pl.pallas_call(
        body, name=name, grid=grid, in_specs=[a_spec, b_spec], out_specs=out_spec, out_shape=out_sds,
        scratch_shapes=[pltpu.VMEM(out_block, F32)],
        compiler_params=pltpu.CompilerParams(dimension_semantics=("parallel", "parallel", "arbitrary")),
    )(a, b)


def _mm(name, a, w, out_dtype, n=None, col_off=0):
    M, K = a.shape
    n = w.shape[1] if n is None else n
    tm = _pick(M, (512, 256, 128))
    tn = _pick(n, (512, 384, 256, 128))
    tk = _pick(K, (1024, 1408, 1152, 512, 256))
    assert col_off % tn == 0
    cb, nk = col_off // tn, K // tk
    return _matmul(
        name, _NN, a, w, pl.BlockSpec((tm, tk), lambda i, j, k: (i, k)), pl.BlockSpec((tk, tn), lambda i, j, k: (k, j + cb)),
        jax.ShapeDtypeStruct((M, n), out_dtype), pl.BlockSpec((tm, tn), lambda i, j, k: (i, j)), (tm, tn),
        (M // tm, n // tn, nk), nk)


def _mm_nt(name, a, w, out_dtype):
    M, K = a.shape
    N = w.shape[0]
    tm = _pick(M, (512, 256, 128))
    tn = _pick(N, (512, 256, 128))
    tk = _pick(K, (1024, 1152, 512, 256))
    nk = K // tk
    return _matmul(
        name, _NT, a, w, pl.BlockSpec((tm, tk), lambda i, j, k: (i, k)), pl.BlockSpec((tn, tk), lambda i, j, k: (j, k)),
        jax.ShapeDtypeStruct((M, N), out_dtype), pl.BlockSpec((tm, tn), lambda i, j, k: (i, j)), (tm, tn),
        (M // tm, N // tn, nk), nk)


def _mm_tn(name, a, b, out_dtype):
    S, M = a.shape
    N = b.shape[1]
    tm = _pick(M, (512, 256, 128))
    tn = _pick(N, (512, 640, 256, 128))
    tk = _pick(S, (512, 256, 128))
    nk = S // tk
    return _matmul(
        name, _TN, a, b, pl.BlockSpec((tk, tm), lambda i, j, k: (k, i)), pl.BlockSpec((tk, tn), lambda i, j, k: (k, j)),
        jax.ShapeDtypeStruct((M, N), out_dtype), pl.BlockSpec((tm, tn), lambda i, j, k: (i, j)), (tm, tn),
        (M // tm, N // tn, nk), nk)


def _mm_cols(name, a, wsh, out_dtype):
    M, K = a.shape
    _, _, c = wsh.shape
    tm = _pick(M, (512, 256, 128))
    return _matmul(
        name, _NN, a, wsh, pl.BlockSpec((tm, K), lambda i, j, k: (i, 0)), pl.BlockSpec((None, K, c), lambda i, j, k: (j, 0, 0)),
        jax.ShapeDtypeStruct((M, N_DEV * c), out_dtype), pl.BlockSpec((tm, c), lambda i, j, k: (i, j)), (tm, c),
        (M // tm, N_DEV, 1), 1)


def _mm_cols_nt(name, a, wsh, out_dtype):
    M = a.shape[0]
    _, K, c = wsh.shape
    tm = _pick(M, (512, 256, 128))
    return _matmul(
        name, _NT, a, wsh, pl.BlockSpec((tm, c), lambda i, j, k: (i, k)), pl.BlockSpec((None, K, c), lambda i, j, k: (k, 0, 0)),
        jax.ShapeDtypeStruct((M, K), out_dtype), pl.BlockSpec((tm, K), lambda i, j, k: (i, 0)), (tm, K),
        (M // tm, 1, N_DEV), N_DEV)


def _mm_tn_cols(name, a, b, out_dtype):
    S, K = a.shape
    c = b.shape[1] // N_DEV
    tk = _pick(S, (512, 256, 128))
    nk = S // tk
    return _matmul(
        name, _TN, a, b, pl.BlockSpec((tk, K), lambda i, j, k: (k, 0)), pl.BlockSpec((tk, c), lambda i, j, k: (k, j)),
        jax.ShapeDtypeStruct((N_DEV, K, c), out_dtype), pl.BlockSpec((None, K, c), lambda i, j, k: (j, 0, 0)), (K, c),
        (1, N_DEV, nk), nk)


def _sigmoid(z):
    return 1.0 / (1.0 + jnp.exp(-z))


def _ffn_up(name, xb, w_in):
    S, D = xb.shape
    c = w_in.shape[2]
    ts = _pick(S, (512, 256, 128))
    half = N_DEV // 2

    def body(x_ref, wg_ref, wu_ref, a_ref, gu_ref):
        x = x_ref[...]
        g = jnp.dot(x, wg_ref[...], preferred_element_type=F32)
        u = jnp.dot(x, wu_ref[...], preferred_element_type=F32)
        a_ref[...] = (g * _sigmoid(g) * u).astype(a_ref.dtype)
        gu_ref[0] = g.astype(gu_ref.dtype)
        gu_ref[1] = u.astype(gu_ref.dtype)

    return pl.pallas_call(
        body, name=name, grid=(S // ts, half),
        in_specs=[pl.BlockSpec((ts, D), lambda i, j: (i, 0)), pl.BlockSpec((None, D, c), lambda i, j: (j, 0, 0)),
                  pl.BlockSpec((None, D, c), lambda i, j: (j + half, 0, 0))],
        out_specs=[pl.BlockSpec((None, ts, c), lambda i, j: (j, i, 0)),
                   pl.BlockSpec((2, None, ts, c), lambda i, j: (0, j, i, 0))],
        out_shape=[jax.ShapeDtypeStruct((half, S, c), BF16), jax.ShapeDtypeStruct((2, half, S, c), BF16)],
        compiler_params=pltpu.CompilerParams(dimension_semantics=("parallel", "parallel")),
    )(xb, w_in, w_in)


def _ffn_down(name, a4, w_out4):
    n, S, c = a4.shape
    D = w_out4.shape[2]
    tm = _pick(S, (512, 256, 128))
    tn = _pick(D, (512, 256, 128))
    return _matmul(
        name, _NN, a4, w_out4, pl.BlockSpec((None, tm, c), lambda i, j, k: (k, i, 0)),
        pl.BlockSpec((None, c, tn), lambda i, j, k: (k, 0, j)),
        jax.ShapeDtypeStruct((S, D), F32), pl.BlockSpec((tm, tn), lambda i, j, k: (i, j)), (tm, tn),
        (S // tm, D // tn, n), n)


def _ffn_bwd_act(name, dyb, w_out4, gu):
    S, D = dyb.shape
    n, c, _ = w_out4.shape
    ts = _pick(S, (512, 256, 128))

    def body(dy_ref, w_ref, gu_ref, o_ref):
        da = lax.dot_general(dy_ref[...], w_ref[...], _NT, preferred_element_type=F32)
        g = gu_ref[0].astype(F32)
        u = gu_ref[1].astype(F32)
        sg = _sigmoid(g)
        o_ref[0] = (da * u * (sg * (1.0 + g * (1.0 - sg)))).astype(o_ref.dtype)
        o_ref[1] = (da * (g * sg)).astype(o_ref.dtype)

    blk = pl.BlockSpec((2, None, ts, c), lambda i, j: (0, j, i, 0))
    return pl.pallas_call(
        body, name=name, grid=(S // ts, n),
        in_specs=[pl.BlockSpec((ts, D), lambda i, j: (i, 0)), pl.BlockSpec((None, c, D), lambda i, j: (j, 0, 0)), blk],
        out_specs=blk, out_shape=jax.ShapeDtypeStruct((2, n, S, c), BF16),
        compiler_params=pltpu.CompilerParams(dimension_semantics=("parallel", "parallel")),
    )(dyb, w_out4, gu)


def _ffn_d_w_out(name, a4, dyb):
    n, S, c = a4.shape
    D = dyb.shape[1]
    tn = _pick(D, (512, 256, 128))
    tk = _pick(S, (512, 256, 128))
    nk = S // tk
    return _matmul(
        name, _TN, a4, dyb, pl.BlockSpec((None, tk, c), lambda i, j, k: (i, k, 0)), pl.BlockSpec((tk, tn), lambda i, j, k: (k, j)),
        jax.ShapeDtypeStruct((n, c, D), BF16), pl.BlockSpec((None, c, tn), lambda i, j, k: (i, 0, j)), (c, tn),
        (n, D // tn, nk), nk)


def _ffn_d_w_in(name, hb, dgu8):
    S, D = hb.shape
    n, _, c = dgu8.shape
    tm = _pick(D, (512, 256, 128))
    tk = _pick(S, (512, 256, 128))
    nk = S // tk
    return _matmul(
        name, _TN, hb, dgu8, pl.BlockSpec((tk, tm), lambda i, j, k: (k, j)), pl.BlockSpec((None, tk, c), lambda i, j, k: (i, k, 0)),
        jax.ShapeDtypeStruct((n, D, c), BF16), pl.BlockSpec((None, tm, c), lambda i, j, k: (i, j, 0)), (tm, c),
        (n, D // tm, nk), nk)


def _ffn_in_bwd(name, dgu8, w_in):
    n, S, c = dgu8.shape
    D = w_in.shape[1]
    tm = _pick(S, (512, 256, 128))
    tn = _pick(D, (512, 256, 128))
    return _matmul(
        name, _NT, dgu8, w_in, pl.BlockSpec((None, tm, c), lambda i, j, k: (k, i, 0)),
        pl.BlockSpec((None, tn, c), lambda i, j, k: (k, j, 0)),
        jax.ShapeDtypeStruct((S, D), F32), pl.BlockSpec((tm, tn), lambda i, j, k: (i, j)), (tm, tn),
        (S // tm, D // tn, n), n)


def _rowwise(name, fn, tiled, params, outs, reds=(), ts=256):
    S = tiled[0][0].shape[0]
    ts = _pick(S, (ts, 128))
    n_t, n_p, n_o = len(tiled), len(params), len(outs)

    def body(*refs):
        vals = [r[...] for r in refs[:n_t + n_p]]
        res_o, res_r = fn(*vals)
        o_refs = refs[n_t + n_p:n_t + n_p + n_o]
        r_refs = refs[n_t + n_p + n_o:]
        for r, val in zip(o_refs, res_o, strict=True):
            r[...] = val.astype(r.dtype)
        if r_refs:
            i = pl.program_id(0)

            @pl.when(i == 0)
            def _():
                for r, val in zip(r_refs, res_r, strict=True):
                    r[...] = val.astype(F32)

            @pl.when(i > 0)
            def _():
                for r, val in zip(r_refs, res_r, strict=True):
                    r[...] += val.astype(F32)

    in_specs = [pl.BlockSpec((ts, w), functools.partial(lambda i, cb: (i, cb), cb=cb)) for _, cb, w in tiled]
    in_specs += [pl.BlockSpec(p.shape, lambda i: (0, 0)) for p in params]
    out_specs = [pl.BlockSpec((ts, w), lambda i: (i, 0)) for w, _ in outs]
    out_specs += [pl.BlockSpec(tuple(r), lambda i: (0, 0)) for r in reds]
    out_shape = [jax.ShapeDtypeStruct((S, w), dt) for w, dt in outs]
    out_shape += [jax.ShapeDtypeStruct(tuple(r), F32) for r in reds]
    res = pl.pallas_call(
        body, name=name, grid=(S // ts,), in_specs=in_specs, out_specs=out_specs, out_shape=out_shape,
        compiler_params=pltpu.CompilerParams(dimension_semantics=("arbitrary",)),
    )(*[t[0] for t in tiled], *params)
    return res[:n_o], res[n_o:]


def _ln(r, g, b):
    mu = jnp.mean(r, axis=-1, keepdims=True)
    xc = r - mu
    var = jnp.mean(xc * xc, axis=-1, keepdims=True)
    return xc * lax.rsqrt(var + LN_EPS) * g + b


def _ln_bwd(r, g, b, dy):
    _, vjp = jax.vjp(_ln, r, g, b)
    return vjp(dy)


def _shift_down(x, d):
    return x if d == 0 else pltpu.roll(x, d, 0)


def _conv_fwd(name, x, w):
    S = x.shape[0]
    ts = _pick(S, (256, 128))
    hb = ts // HALO

    def body(x_ref, prev_ref, w_ref, o_ref):
        i = pl.program_id(0)
        prev = jnp.where(i > 0, prev_ref[...], 0.0)
        xe = jnp.concatenate([prev, x_ref[...]], axis=0)
        y = jnp.zeros((ts + HALO, CONV_CH), F32)
        for j in range(CONV_K):
            y = y + w_ref[pl.ds(j, 1), :] * _shift_down(xe, CONV_K - 1 - j)
        y = y[HALO:, :]
        o_ref[...] = y * _sigmoid(y)

    return pl.pallas_call(
        body, name=name, grid=(S // ts,),
        in_specs=[pl.BlockSpec((ts, CONV_CH), lambda i: (i, 0)),
                  pl.BlockSpec((HALO, CONV_CH), lambda i: (jnp.maximum(i * hb - 1, 0), 0)),
                  pl.BlockSpec((CONV_K, CONV_CH), lambda i: (0, 0))],
        out_specs=pl.BlockSpec((ts, CONV_CH), lambda i: (i, 0)),
        out_shape=jax.ShapeDtypeStruct((S, CONV_CH), F32),
        compiler_params=pltpu.CompilerParams(dimension_semantics=("arbitrary",)),
    )(x, x, w)


def _conv_bwd(name, x, w, dout):
    S = x.shape[0]
    ts = _pick(S, (256, 128))
    hb = ts // HALO
    nt = S // ts
    n_ext = ts + 2 * HALO

    def body(x_ref, prev_ref, next_ref, w_ref, d_ref, dnext_ref, dx_ref, dw_ref):
        i = pl.program_id(0)
        prev = jnp.where(i > 0, prev_ref[...], 0.0)
        last = i == nt - 1
        nxt = jnp.where(last, 0.0, next_ref[...])
        dnxt = jnp.where(last, 0.0, dnext_ref[...])
        xe = jnp.concatenate([prev, x_ref[...], nxt], axis=0)
        de = jnp.concatenate([jnp.zeros((HALO, CONV_CH), F32), d_ref[...], dnxt], axis=0)
        y = jnp.zeros((n_ext, CONV_CH), F32)
        for j in range(CONV_K):
            y = y + w_ref[pl.ds(j, 1), :] * _shift_down(xe, CONV_K - 1 - j)
        sg = _sigmoid(y)
        dy = de * (sg * (1.0 + y * (1.0 - sg)))
        dx = jnp.zeros((n_ext, CONV_CH), F32)
        for j in range(CONV_K):
            m = CONV_K - 1 - j
            dx = dx + w_ref[pl.ds(j, 1), :] * (dy if m == 0 else pltpu.roll(dy, n_ext - m, 0))
        dx_ref[...] = dx[HALO:HALO + ts, :].astype(dx_ref.dtype)
        row = lax.broadcasted_iota(jnp.int32, (n_ext, 1), 0)
        dy_own = jnp.where((row >= HALO) & (row < HALO + ts), dy, 0.0)
        parts = [jnp.sum(dy_own * _shift_down(xe, CONV_K - 1 - j), axis=0, keepdims=True) for j in range(CONV_K)]
        dw = jnp.concatenate(parts + [jnp.zeros((HALO - CONV_K, CONV_CH), F32)], axis=0)

        @pl.when(i == 0)
        def _():
            dw_ref[...] = dw

        @pl.when(i > 0)
        def _():
            dw_ref[...] += dw

    tile = pl.BlockSpec((ts, CONV_CH), lambda i: (i, 0))
    prev = pl.BlockSpec((HALO, CONV_CH), lambda i: (jnp.maximum(i * hb - 1, 0), 0))
    nxt = pl.BlockSpec((HALO, CONV_CH), lambda i: (jnp.minimum((i + 1) * hb, nt * hb - 1), 0))
    return pl.pallas_call(
        body, name=name, grid=(nt,),
        in_specs=[tile, prev, nxt, pl.BlockSpec((CONV_K, CONV_CH), lambda i: (0, 0)), tile, nxt],
        out_specs=[tile, pl.BlockSpec((HALO, CONV_CH), lambda i: (0, 0))],
        out_shape=[jax.ShapeDtypeStruct((S, CONV_CH), BF16), jax.ShapeDtypeStruct((HALO, CONV_CH), F32)],
        compiler_params=pltpu.CompilerParams(dimension_semantics=("arbitrary",)),
    )(x, x, x, w, dout, dout)


def _softplus(z):
    return jnp.maximum(z, 0.0) + jnp.log(1.0 + jnp.exp(-jnp.abs(z)))


def _dot2(x, tri2):
    hi = x.astype(BF16)
    lo = (x - hi.astype(F32)).astype(BF16)
    return jnp.dot(jnp.concatenate([hi, lo], axis=1), tri2, preferred_element_type=F32)


def _pair_tri(tk, keep):
    r = lax.broadcasted_iota(jnp.int32, (4 * tk, 2 * tk), 0)
    c = lax.broadcasted_iota(jnp.int32, (4 * tk, 2 * tk), 1)
    same_head = ((r // tk) % 2) == (c // tk)
    return (same_head & keep(r % tk, c % tk)).astype(BF16)


def _pair_rows(x2, first):
    return jnp.concatenate([jnp.where(first, x2, 0), jnp.where(first, 0, x2)], axis=0)


def _pair_sum(x):
    h = x.shape[1] // 2
    return jnp.sum(x[:, :h], axis=1, keepdims=True), jnp.sum(x[:, h:], axis=1, keepdims=True)


def _sb_tiles(S):
    tq = _pick(S, (SB_TQ, 256, 128))
    return tq, SB_TK


N_PAIRS = N_HEADS // 2


def _sb_specs(S, tq):
    q = pl.BlockSpec((tq, 128), lambda p, i: (i, p))
    k = pl.BlockSpec((S, 128), lambda p, i: (0, N_PAIRS + p))
    v = pl.BlockSpec((S, 128), lambda p, i: (0, 2 * N_PAIRS + p))
    return q, k, v


def _scaled(q):
    return (q.astype(F32) * (HEAD_DIM ** -0.5)).astype(BF16)


def _sb_fwd(qkv):
    S = qkv.shape[0]
    tq, tk = _sb_tiles(S)
    nd = tq // tk

    def body(q_ref, k_ref, v_ref, o_ref, ta_ref, tb_ref):
        qi = pl.program_id(1)
        qs = _scaled(q_ref[...])
        row = qi * tq + lax.broadcasted_iota(jnp.int32, (tq, 2 * tk), 0)
        col = lax.broadcasted_iota(jnp.int32, (tq, 2 * tk), 1)
        left = col < tk
        col = col % tk
        later = _pair_tri(tk, lambda r, c: r > c)
        first = lax.broadcasted_iota(jnp.int32, (tk, 128), 1) < HEAD_DIM

        def tile(j, carry, masked):
            acc, suf_a, suf_b = carry
            off = pl.multiple_of(j * tk, tk)
            kc = _pair_rows(k_ref[pl.ds(off, tk), :], first)
            vc = _pair_rows(v_ref[pl.ds(off, tk), :], first)
            mask = (col + j * tk) < row
            z = lax.dot_general(qs, kc, _NT, preferred_element_type=F32)
            lf = -_softplus(z)
            if masked:
                lf = jnp.where(mask, lf, 0.0)
            w = jnp.exp(z + lf + _dot2(lf, later) + jnp.where(left, suf_a, suf_b))
            if masked:
                w = jnp.where(mask, w, 0.0)
            acc = acc + jnp.dot(w.astype(BF16), vc, preferred_element_type=F32)
            sum_a, sum_b = _pair_sum(lf)
            return acc, suf_a + sum_a, suf_b + sum_b

        zero1 = jnp.zeros((tq, 1), F32)
        carry = (jnp.zeros((tq, 128), F32), zero1, zero1)
        n_full = qi * nd
        carry = lax.fori_loop(0, nd, lambda d, c: tile(n_full + nd - 1 - d, c, True), carry)
        acc, suf_a, suf_b = lax.fori_loop(0, n_full, lambda jj, c: tile(n_full - 1 - jj, c, False), carry)
        o_ref[...] = acc.astype(o_ref.dtype)
        ta_ref[...] = suf_a
        tb_ref[...] = suf_b

    tot = pl.BlockSpec((None, tq, 1), lambda p, i: (p, i, 0))
    tshape = jax.ShapeDtypeStruct((N_PAIRS, S, 1), F32)
    return pl.pallas_call(
        body, name="sb_fwd", grid=(N_PAIRS, S // tq), in_specs=list(_sb_specs(S, tq)),
        out_specs=[pl.BlockSpec((tq, 128), lambda p, i: (i, p)), tot, tot],
        out_shape=[jax.ShapeDtypeStruct((S, HEAD_W), BF16), tshape, tshape],
        compiler_params=pltpu.CompilerParams(dimension_semantics=("arbitrary", "arbitrary")),
    )(qkv, qkv, qkv)


def _sb_bwd(qkv, tot_a, tot_b, do):
    S = qkv.shape[0]
    tq, tk = _sb_tiles(S)
    nd = tq // tk
    scale = HEAD_DIM ** -0.5

    def body(q_ref, k_ref, v_ref, ta_ref, tb_ref, do_ref, dq_ref, dk_ref, dv_ref):
        qi = pl.program_id(1)

        @pl.when(qi == 0)
        def _():
            dk_ref[...] = jnp.zeros_like(dk_ref)
            dv_ref[...] = jnp.zeros_like(dv_ref)

        qs = _scaled(q_ref[...])
        do2 = do_ref[...]
        row = qi * tq + lax.broadcasted_iota(jnp.int32, (tq, 2 * tk), 0)
        col = lax.broadcasted_iota(jnp.int32, (tq, 2 * tk), 1)
        left = col < tk
        col = col % tk
        total = jnp.where(left, ta_ref[...], tb_ref[...])
        upto = _pair_tri(tk, lambda r, c: r <= c)
        first = lax.broadcasted_iota(jnp.int32, (tk, 128), 1) < HEAD_DIM

        def tile(j, carry, masked):
            dq, pl_a, pl_b, pg_a, pg_b = carry
            off = pl.multiple_of(j * tk, tk)
            kc = _pair_rows(k_ref[pl.ds(off, tk), :], first)
            vc = _pair_rows(v_ref[pl.ds(off, tk), :], first)
            mask = (col + j * tk) < row
            z = lax.dot_general(qs, kc, _NT, preferred_element_type=F32)
            lf = -_softplus(z)
            sig = jnp.exp(z + lf)
            if masked:
                lf = jnp.where(mask, lf, 0.0)
            w = jnp.exp(z + lf + (total - (jnp.where(left, pl_a, pl_b) + _dot2(lf, upto))))
            if masked:
                w = jnp.where(mask, w, 0.0)
            gl = lax.dot_general(do2, vc, _NT, preferred_element_type=F32) * w
            dz = gl - sig * (jnp.where(left, pg_a, pg_b) + _dot2(gl, upto))
            if masked:
                dz = jnp.where(mask, dz, 0.0)
            dzb = dz.astype(BF16)
            dq = dq + jnp.dot(dzb, kc, preferred_element_type=F32)
            dkc = lax.dot_general(dzb, qs, _TN, preferred_element_type=F32)
            dvc = lax.dot_general(w.astype(BF16), do2, _TN, preferred_element_type=F32)
            dk_ref[pl.ds(off, tk), :] += jnp.where(first, dkc[:tk], dkc[tk:])
            dv_ref[pl.ds(off, tk), :] += jnp.where(first, dvc[:tk], dvc[tk:])
            sl_a, sl_b = _pair_sum(lf)
            sg_a, sg_b = _pair_sum(gl)
            return dq, pl_a + sl_a, pl_b + sl_b, pg_a + sg_a, pg_b + sg_b

        zero1 = jnp.zeros((tq, 1), F32)
        carry = (jnp.zeros((tq, 128), F32), zero1, zero1, zero1, zero1)
        n_full = qi * nd
        carry = lax.fori_loop(0, n_full, lambda j, c: tile(j, c, False), carry)
        dq = lax.fori_loop(0, nd, lambda d, c: tile(n_full + d, c, True), carry)[0]
        dq_ref[...] = dq * scale

    q_spec, k_spec, v_spec = _sb_specs(S, tq)
    tot = pl.BlockSpec((None, tq, 1), lambda p, i: (p, i, 0))
    whole = pl.BlockSpec((S, 128), lambda p, i: (0, p))
    big = jax.ShapeDtypeStruct((S, HEAD_W), F32)
    return pl.pallas_call(
        body, name="sb_bwd", grid=(N_PAIRS, S // tq),
        in_specs=[q_spec, k_spec, v_spec, tot, tot, q_spec],
        out_specs=[q_spec, whole, whole], out_shape=[big, big, big],
        compiler_params=pltpu.CompilerParams(dimension_semantics=("arbitrary", "arbitrary")),
    )(qkv, qkv, qkv, tot_a, tot_b, do)


def _bmm(a, b):
    return lax.dot_general(a, b, (((2,), (1,)), ((0,), (0,))), precision=HI, preferred_element_type=F32)


def _bmm_nt(a, b):
    return lax.dot_general(a, b, (((2,), (2,)), ((0,), (0,))), precision=HI, preferred_element_type=F32)


def _bmm_tn(a, b):
    return lax.dot_general(a, b, (((1,), (1,)), ((0,), (0,))), precision=HI, preferred_element_type=F32)


def _tri_inv(lower):
    C = lower.shape[-1]
    ii = lax.broadcasted_iota(jnp.int32, (C, C), 0)
    jj = lax.broadcasted_iota(jnp.int32, (C, C), 1)
    eye = (ii == jj).astype(F32)[None]
    xd = jnp.where((ii // 8 == jj // 8)[None], -lower, 0.0)
    x2 = _bmm(xd, xd)
    x4 = _bmm(x2, x2)
    inv = eye + xd
    inv = inv + _bmm(inv, x2)
    inv = inv + _bmm(inv, x4)
    b = 8
    while b < C:
        off = jnp.where(((ii // (2 * b) == jj // (2 * b)) & (ii // b != jj // b))[None], lower, 0.0)
        inv = inv - _bmm(inv, _bmm(off, inv))
        b *= 2
    return inv


@jax.custom_vjp
def _tri_solve(lower, rhs):
    return _bmm(_tri_inv(lower), rhs)


def _tri_solve_fwd(lower, rhs):
    inv = _tri_inv(lower)
    sol = _bmm(inv, rhs)
    return sol, (inv, sol)


def _tri_solve_bwd(res, dsol):
    inv, sol = res
    drhs = _bmm_tn(inv, dsol)
    C = inv.shape[-1]
    ii = lax.broadcasted_iota(jnp.int32, (C, C), 0)
    jj = lax.broadcasted_iota(jnp.int32, (C, C), 1)
    return jnp.where((jj < ii)[None], -_bmm_nt(drhs, sol), 0.0), drhs


_tri_solve.defvjp(_tri_solve_fwd, _tri_solve_bwd)


def _pairs(x):
    return jnp.stack([x[:, 128 * p:128 * (p + 1)] for p in range(N_PAIRS)], axis=0)


def _unpairs(x):
    return jnp.concatenate([x[p] for p in range(N_PAIRS)], axis=1)


def _gdn_chunk(state, qkv, gz, ba, a_log_x, dt_x, norm_x):
    C = qkv.shape[0]
    lane = lax.broadcasted_iota(jnp.int32, (1, 1, 128), 2)
    first = lane < HEAD_DIM

    def split_heads(x2):
        return jnp.stack([jnp.where(first, x2, 0.0), jnp.where(first, 0.0, x2)], axis=1).reshape(N_HEADS, *x2.shape[1:])

    def merge_heads(xh):
        x = xh.reshape(N_PAIRS, 2, *xh.shape[1:])
        return x[:, 0] + x[:, 1]

    def head_cols(x2):
        a = jnp.sum(jnp.where(lane == 0, x2, 0.0), axis=-1, keepdims=True)
        b = jnp.sum(jnp.where(lane == HEAD_DIM, x2, 0.0), axis=-1, keepdims=True)
        return jnp.stack([a, b], axis=1).reshape(N_HEADS, *a.shape[1:])

    def to_pair(xh):
        x = xh.reshape(N_PAIRS, 2, *xh.shape[1:])
        return jnp.where(first, x[:, 0], x[:, 1])

    def head_sums(x2):
        a = jnp.sum(jnp.where(first, x2, 0.0), axis=-1, keepdims=True)
        b = jnp.sum(jnp.where(first, 0.0, x2), axis=-1, keepdims=True)
        return jnp.where(first, a, b)

    er = lax.broadcasted_iota(jnp.int32, (128, 2 * HEAD_W), 0)
    ec = lax.broadcasted_iota(jnp.int32, (128, 2 * HEAD_W), 1)
    spread = (er == ec // HEAD_DIM).astype(F32)
    bx = lax.dot_general(ba, spread, (((1,), (0,)), ((), ())), precision=HI, preferred_element_type=F32)
    beta2 = _pairs(_sigmoid(bx[:, :HEAD_W]))
    g2 = _pairs(-jnp.exp(a_log_x) * _softplus(bx[:, HEAD_W:] + dt_x))
    beta = head_cols(beta2)
    g = head_cols(g2)

    q2, k2, v2 = (_pairs(qkv[:, i * HEAD_W:(i + 1) * HEAD_W]) for i in range(3))
    qn2 = q2 * lax.rsqrt(head_sums(q2 * q2) + RMS_EPS) * (HEAD_DIM ** -0.5)
    kn2 = k2 * lax.rsqrt(head_sums(k2 * k2) + RMS_EPS)
    knh = split_heads(kn2)

    ii = lax.broadcasted_iota(jnp.int32, (C, C), 0)[None]
    jj = lax.broadcasted_iota(jnp.int32, (C, C), 1)[None]
    incl = jj <= ii
    g_row = jnp.sum(jnp.where(ii == jj, g, 0.0), axis=1, keepdims=True)
    gc_col = jnp.sum(jnp.where(incl, g_row, 0.0), axis=2, keepdims=True)
    gc_row = jnp.sum(jnp.where(ii <= jj, g, 0.0), axis=1, keepdims=True)
    decay = jnp.where(incl, jnp.exp(jnp.where(incl, gc_col - gc_row, 0.0)), 0.0)
    lower = jnp.where(jj < ii, beta * _bmm_nt(knh, knh) * decay, 0.0)
    gc2 = to_pair(gc_col)
    egc2 = jnp.exp(gc2)
    u2 = merge_heads(_tri_solve(lower, split_heads(v2 * beta2)))
    w2 = merge_heads(_tri_solve(lower, split_heads(kn2 * (beta2 * egc2))))
    qk = jnp.where(incl, _bmm_nt(jnp.repeat(qn2, 2, axis=0), knh) * decay, 0.0)
    g_last2 = to_pair(jnp.sum(g, axis=1, keepdims=True))
    v_new2 = u2 - _bmm(w2, state)
    o2 = _bmm(qn2 * egc2, state) + merge_heads(_bmm(qk, split_heads(v_new2)))
    sr = lax.broadcasted_iota(jnp.int32, (128, 128), 0)
    sc = lax.broadcasted_iota(jnp.int32, (128, 128), 1)
    same_head = ((sr < HEAD_DIM) == (sc < HEAD_DIM))[None]
    new_state = state * jnp.exp(g_last2) + jnp.where(same_head, _bmm_tn(kn2 * jnp.exp(g_last2 - gc2), v_new2), 0.0)
    o2 = o2 * lax.rsqrt(head_sums(o2 * o2) * (1.0 / HEAD_DIM) + RMS_EPS) * _pairs(norm_x)
    gz2 = _pairs(gz)
    return new_state, _unpairs(o2 * (gz2 * _sigmoid(gz2)))


def _gdn_specs(order):
    C = GDN_CHUNK
    par = pl.BlockSpec((1, HEAD_W), lambda c: (0, 0))
    return [pl.BlockSpec((C, CONV_CH), lambda c: (order(c), 0)), pl.BlockSpec((C, HEAD_W), lambda c: (order(c), 3)),
            pl.BlockSpec((C, 128), lambda c: (order(c), 0)), par, par, par]


def _gdn_fwd(qkv, gdnp, ba, a_log_x, dt_x, norm_x):
    S = qkv.shape[0]
    C = GDN_CHUNK
    n = S // C

    def body(qkv_ref, gz_ref, ba_ref, al_ref, dt_ref, nw_ref, o_ref, st_ref, state):
        @pl.when(pl.program_id(0) == 0)
        def _():
            state[...] = jnp.zeros_like(state)

        st = state[...]
        st_ref[...] = st
        new, o = _gdn_chunk(st, qkv_ref[...], gz_ref[...], ba_ref[...], al_ref[...], dt_ref[...], nw_ref[...])
        state[...] = new
        o_ref[...] = o.astype(o_ref.dtype)

    return pl.pallas_call(
        body, name="gdn_fwd", grid=(n,), in_specs=_gdn_specs(lambda c: c),
        out_specs=[pl.BlockSpec((C, HEAD_W), lambda c: (c, 0)),
                   pl.BlockSpec((None, N_PAIRS, 128, 128), lambda c: (c, 0, 0, 0))],
        out_shape=[jax.ShapeDtypeStruct((S, HEAD_W), BF16), jax.ShapeDtypeStruct((n, N_PAIRS, 128, 128), F32)],
        scratch_shapes=[pltpu.VMEM((N_PAIRS, 128, 128), F32)],
        compiler_params=pltpu.CompilerParams(dimension_semantics=("arbitrary",)),
    )(qkv, gdnp, ba, a_log_x, dt_x, norm_x)


def _gdn_bwd(qkv, gdnp, ba, a_log_x, dt_x, norm_x, states, do):
    S = qkv.shape[0]
    C = GDN_CHUNK
    n = S // C

    def body(qkv_ref, gz_ref, ba_ref, al_ref, dt_ref, nw_ref, st_ref, do_ref,
             dqkv_ref, dgz_ref, dba_ref, dal_ref, ddt_ref, dnw_ref, dstate):
        @pl.when(pl.program_id(0) == 0)
        def _():
            dstate[...] = jnp.zeros_like(dstate)
            dal_ref[...] = jnp.zeros_like(dal_ref)
            ddt_ref[...] = jnp.zeros_like(ddt_ref)
            dnw_ref[...] = jnp.zeros_like(dnw_ref)

        args = (st_ref[...], qkv_ref[...], gz_ref[...], ba_ref[...], al_ref[...], dt_ref[...], nw_ref[...])
        _, vjp = jax.vjp(_gdn_chunk, *args)
        dst, dqkv, dgz, dba, dal, ddt, dnw = vjp((dstate[...], do_ref[...]))
        dstate[...] = dst
        dqkv_ref[...] = dqkv
        dgz_ref[...] = dgz.astype(dgz_ref.dtype)
        dba_ref[...] = dba
        dal_ref[...] += dal
        ddt_ref[...] += ddt
        dnw_ref[...] += dnw

    rev = lambda c: n - 1 - c
    par = pl.BlockSpec((1, HEAD_W), lambda c: (0, 0))
    f = jax.ShapeDtypeStruct
    return pl.pallas_call(
        body, name="gdn_bwd", grid=(n,),
        in_specs=_gdn_specs(rev) + [pl.BlockSpec((None, N_PAIRS, 128, 128), lambda c: (rev(c), 0, 0, 0)),
                                    pl.BlockSpec((C, HEAD_W), lambda c: (rev(c), 0))],
        out_specs=[pl.BlockSpec((C, CONV_CH), lambda c: (rev(c), 0)), pl.BlockSpec((C, HEAD_W), lambda c: (rev(c), 0)),
                   pl.BlockSpec((C, 128), lambda c: (rev(c), 0)), par, par, par],
        out_shape=[f((S, CONV_CH), F32), f((S, HEAD_W), BF16), f((S, 128), F32)] + [f((1, HEAD_W), F32)] * 3,
        scratch_shapes=[pltpu.VMEM((N_PAIRS, 128, 128), F32)],
        compiler_params=pltpu.CompilerParams(dimension_semantics=("arbitrary",)),
    )(qkv, gdnp, ba, a_log_x, dt_x, norm_x, states, do)


def _local_step(x, p, target, W, P):
    S = x.shape[0]
    D = D_MODEL
    xb = x.astype(BF16)
    pb = p.astype(BF16)

    def ffn_fwd(tag, h, hb, w_in, w_out, g, b):
        a, gu = _ffn_up(f"{tag}_up", hb, w_in)
        f = _ffn_down(f"{tag}_down", a, w_out)

        def fn(h, f, g, b):
            r = ALPHA * h + 0.5 * f
            y = _ln(r, g, b)
            return (r, y, y), ()

        (r, y, yb), _ = _rowwise(f"{tag}_ln", fn, [(h, 0, D), (f, 0, D)], [g, b], [(D, F32), (D, F32), (D, BF16)])
        return a, gu, r, y, yb

    a1, gu1, r1, h1, h1b = ffn_fwd("ffn1", x, xb, W["ffn1_w_in"], W["ffn1_w_out"], P["ln1_g"], P["ln1_b"])

    wmix = W["w_mix_pad"]
    sbp = _mm("mix_sb", h1b, wmix, BF16, n=3 * HEAD_W, col_off=MIX_SB)
    gdnp = _mm("mix_gdn", h1b, wmix, F32, n=4 * HEAD_W, col_off=MIX_GDN)
    gates = _mm("mix_gates", h1b, wmix, F32, n=2 * D, col_off=MIX_GATES)
    ba = _mm("mix_ba", h1b, wmix, F32, n=128, col_off=MIX_BA)

    attb, sb_tot_a, sb_tot_b = _sb_fwd(sbp)
    y_sb = _mm_cols("sb_out", attb, W["w_branch_sb"], F32)

    conv_w = P["conv_w"]
    qkv = _conv_fwd("conv_fwd", gdnp, conv_w)
    a_log_x = jnp.repeat(P["a_log"], HEAD_DIM, axis=1)
    dt_x = jnp.repeat(P["dt_bias"], HEAD_DIM, axis=1)
    norm_x = jnp.tile(P["gdn_norm_w"], (1, N_HEADS))
    gob, states = _gdn_fwd(qkv, gdnp, ba, a_log_x, dt_x, norm_x)
    y_gdn = _mm_cols("gdn_out", gob, W["w_branch_gdn"], F32)

    def merge_fn(gs, gg, ys, yg, bs, bg):
        return ((_sigmoid(gs + bs) * ys + _sigmoid(gg + bg) * yg),), ()

    b_gate = P["b_gate"]
    bs, bg = b_gate[:, :D], b_gate[:, D:]
    (merged,), _ = _rowwise("mix_merge", merge_fn, [(gates, 0, D), (gates, 1, D), (y_sb, 0, D), (y_gdn, 0, D)], [bs, bg],
                            [(D, BF16)])
    mix = _mm("mix_out", merged, W["w_mix_out"], F32)

    def ln2_fn(h, f, g, b):
        r = ALPHA * h + f
        y = _ln(r, g, b)
        return (r, y, y), ()

    (r2, h2, h2b), _ = _rowwise("mix_ln", ln2_fn, [(h1, 0, D), (mix, 0, D)], [P["ln2_g"], P["ln2_b"]],
                                [(D, F32), (D, F32), (D, BF16)])

    a2, gu2, r3, h3, h3b = ffn_fwd("ffn2", h2, h2b, W["ffn2_w_in"], W["ffn2_w_out"], P["ln3_g"], P["ln3_b"])

    zg = _mm("ple_gate", h3b, W["w_ple_gate"], F32)
    pp = _mm_cols("ple_proj", pb, W["w_ple"], F32)

    def ple(h, zg, pp, bp, g, b):
        return _ln(ALPHA * h + _sigmoid(zg + bp) * pp, g, b)

    def head_fn(h, zg, pp, tgt, bp, g, b):
        y, vjp = jax.vjp(ple, h, zg, pp, bp, g, b)
        err = y - tgt
        dh, dzg, dpp, dbp, dg, db = vjp(err * (1.0 / D))
        loss = 0.5 * jnp.sum(jnp.sum(err * err, axis=1, keepdims=True), axis=0, keepdims=True) * (1.0 / D)
        return (dh, dzg, dpp), (loss, dbp, dg, db)

    (dh3_a, dzg, dpp), (loss, d_bple, d_ln4g, d_ln4b) = _rowwise(
        "ple_head", head_fn, [(h3, 0, D), (zg, 0, D), (pp, 0, D), (target, 0, D)],
        [P["b_ple_gate"], P["ln4_g"], P["ln4_b"]], [(D, F32), (D, BF16), (D, BF16)],
        [(1, 1), (1, D), (1, D), (1, D)])

    grads, small = {}, {"b_ple_gate": d_bple, "ln4_g": d_ln4g, "ln4_b": d_ln4b}
    grads["w_ple_gate"] = _mm_tn("d_w_ple_gate", h3b, dzg, BF16)
    grads["w_ple"] = _mm_tn_cols("d_w_ple", pb, dpp, BF16)
    dh3_b = _mm_nt("d_ple_gate_in", dzg, W["w_ple_gate"], F32)

    def ffn_bwd(tag, dy_parts, r, g, b, a, gu, hb_in, w_in, w_out):
        n_parts = len(dy_parts)

        def fn(*vals):
            dy = vals[0]
            for extra in vals[1:n_parts]:
                dy = dy + extra
            r, g, b = vals[n_parts:]
            dr, dg, db = _ln_bwd(r, g, b, dy)
            return (ALPHA * dr, 0.5 * dr), (dg, db)

        (dh_res, dfb), (dg, db) = _rowwise(f"{tag}_ln_bwd", fn, [(t, 0, D) for t in dy_parts] + [(r, 0, D)], [g, b],
                                           [(D, F32), (D, BF16)], [(1, D), (1, D)])
        dgu = _ffn_bwd_act(f"{tag}_act_bwd", dfb, w_out, gu)
        dgu8 = dgu.reshape(N_DEV, S, SHARD_FF)
        d_w_out = _ffn_d_w_out(f"d_{tag}_w_out", a, dfb)
        d_w_in = _ffn_d_w_in(f"d_{tag}_w_in", hb_in, dgu8)
        dh_ffn = _ffn_in_bwd(f"{tag}_in_bwd", dgu8, w_in)
        return dh_res, dh_ffn, d_w_in, d_w_out, dg, db

    dh2_a, dh2_b, grads["ffn2_w_in"], grads["ffn2_w_out"], small["ln3_g"], small["ln3_b"] = ffn_bwd(
        "ffn2", [dh3_a, dh3_b], r3, P["ln3_g"], P["ln3_b"], a2, gu2, h2b, W["ffn2_w_in"], W["ffn2_w_out"])

    def ln2_bwd_fn(d1, d2, r, g, b):
        dr, dg, db = _ln_bwd(r, g, b, d1 + d2)
        return (ALPHA * dr, dr), (dg, db)

    (dh1_a, dmixb), (small["ln2_g"], small["ln2_b"]) = _rowwise(
        "mix_ln_bwd", ln2_bwd_fn, [(dh2_a, 0, D), (dh2_b, 0, D), (r2, 0, D)], [P["ln2_g"], P["ln2_b"]],
        [(D, F32), (D, BF16)], [(1, D), (1, D)])
    grads["w_mix_out"] = _mm_tn("d_w_mix_out", merged, dmixb, BF16)
    dmerged = _mm_nt("mix_out_bwd", dmixb, W["w_mix_out"], F32)

    def merge_bwd_fn(dm, gs, gg, ys, yg, bs, bg):
        ss, sg = _sigmoid(gs + bs), _sigmoid(gg + bg)
        dgs = dm * ys * ss * (1.0 - ss)
        dgg = dm * yg * sg * (1.0 - sg)
        return (dgs, dgg, dm * ss, dm * sg), (jnp.sum(dgs, axis=0, keepdims=True), jnp.sum(dgg, axis=0, keepdims=True))

    (dgs, dgg, dy_sb, dy_gdn), (d_bs, d_bg) = _rowwise(
        "mix_merge_bwd", merge_bwd_fn, [(dmerged, 0, D), (gates, 0, D), (gates, 1, D), (y_sb, 0, D), (y_gdn, 0, D)],
        [bs, bg], [(D, BF16)] * 4, [(1, D), (1, D)])
    small["b_gate"] = jnp.concatenate([d_bs, d_bg], axis=1)

    grads["w_branch_sb"] = _mm_tn_cols("d_w_branch_sb", attb, dy_sb, BF16)
    datt = _mm_cols_nt("sb_out_bwd", dy_sb, W["w_branch_sb"], BF16)
    dsq, dsk, dsv = _sb_bwd(sbp, sb_tot_a, sb_tot_b, datt)

    grads["w_branch_gdn"] = _mm_tn_cols("d_w_branch_gdn", gob, dy_gdn, BF16)
    dgo = _mm_cols_nt("gdn_out_bwd", dy_gdn, W["w_branch_gdn"], F32)
    dqkv, dgz, dba, d_alog_x, d_dt_x, d_norm_x = _gdn_bwd(qkv, gdnp, ba, a_log_x, dt_x, norm_x, states, dgo)
    small["a_log"] = jnp.sum(d_alog_x.reshape(N_HEADS, HEAD_DIM), axis=1).reshape(1, N_HEADS)
    small["dt_bias"] = jnp.sum(d_dt_x.reshape(N_HEADS, HEAD_DIM), axis=1).reshape(1, N_HEADS)
    small["gdn_norm_w"] = jnp.sum(d_norm_x.reshape(N_HEADS, HEAD_DIM), axis=0).reshape(1, HEAD_DIM)
    dconv_in, d_conv_w = _conv_bwd("conv_bwd", gdnp, conv_w, dqkv)
    grads["conv_w"] = d_conv_w[:CONV_K]

    dproj = jnp.concatenate([dsq.astype(BF16), dsk.astype(BF16), dsv.astype(BF16), dconv_in, dgz, dgs, dgg,
                             dba.astype(BF16)], axis=1)
    d_wmix = _mm_tn("d_w_mix_in", h1b, dproj, BF16)
    grads["w_mix_in"] = jnp.concatenate([d_wmix[:, :MIX_GATES], d_wmix[:, MIX_BA:MIX_BA + 2 * N_HEADS],
                                         d_wmix[:, MIX_GATES:MIX_BA]], axis=1)
    dh1_b = _mm_nt("mix_in_bwd", dproj, wmix, F32)

    dx_a, dx_b, grads["ffn1_w_in"], grads["ffn1_w_out"], small["ln1_g"], small["ln1_b"] = ffn_bwd(
        "ffn1", [dh1_a, dh1_b], r1, P["ln1_g"], P["ln1_b"], a1, gu1, xb, W["ffn1_w_in"], W["ffn1_w_out"])

    (grad_x,), _ = _rowwise("grad_x", lambda a, b: ((a + b,), ()), [(dx_a, 0, D), (dx_b, 0, D)], [], [(D, F32)])
    return loss[0, 0], grad_x, grads, small


def _coords():
    return lax.axis_index("x"), lax.axis_index("y"), lax.axis_index("c")


def _all_gather(name, shards):
    n = len(shards)

    def body(*refs):
        x_refs, out_refs = refs[:n], refs[n:2 * n]
        send_sems, recv_sems, local_sems = refs[2 * n:]
        x, y, c = _coords()
        me, sibling = (x, y, c), (x, y, 1 - c)
        chips = [(1 - x, y), (x, 1 - y), (1 - x, 1 - y)]

        def copy(a, k, block, to, src=None):
            px, py, pc = block
            dst = out_refs[a].at[4 * px + 2 * py + pc]
            return pltpu.make_async_remote_copy(
                src_ref=dst if src is None else src, dst_ref=dst,
                send_sem=send_sems.at[a, k], recv_sem=recv_sems.at[a, k], device_id=to, device_id_type=MESH)

        mine = [pltpu.make_async_copy(x_refs[a], out_refs[a].at[4 * x + 2 * y + c], local_sems.at[a]) for a in range(n)]
        for cp in mine:
            cp.start()
        first = []
        for a in range(n):
            first.append(copy(a, 0, me, sibling, src=x_refs[a]))
            first += [copy(a, 1 + j, me, (*chip, c), src=x_refs[a]) for j, chip in enumerate(chips)]
        for cp in first:
            cp.start()
        passed = []
        for j, chip in enumerate(chips):
            for a in range(n):
                copy(a, 1 + j, (*chip, c), me).wait_recv()
                fwd = copy(a, 4 + j, (*chip, c), sibling)
                fwd.start()
                passed.append(fwd)
        for a in range(n):
            copy(a, 0, sibling, me).wait_recv()
            for j, chip in enumerate(chips):
                copy(a, 4 + j, (*chip, 1 - c), me).wait_recv()
        for cp in first + passed:
            cp.wait_send()
        for cp in mine:
            cp.wait()

    any_spec = pl.BlockSpec(memory_space=pl.ANY)
    return pl.pallas_call(
        body, name=name, out_shape=[jax.ShapeDtypeStruct((N_DEV, *s.shape), s.dtype) for s in shards],
        in_specs=[any_spec] * n, out_specs=[any_spec] * n,
        scratch_shapes=[pltpu.SemaphoreType.DMA((n, 7)), pltpu.SemaphoreType.DMA((n, 7)), pltpu.SemaphoreType.DMA((n,))],
    )(*shards)


def _all_to_all(name, srcs):
    n = len(srcs)

    def body(*refs):
        src_refs, dst_refs = refs[:n], refs[n:2 * n]
        send_sems, recv_sems, local_sems = refs[2 * n:]
        x, y, c = _coords()
        me = 4 * x + 2 * y + c
        mine = [pltpu.make_async_copy(src_refs[a].at[me], dst_refs[a].at[me], local_sems.at[a]) for a in range(n)]
        for cp in mine:
            cp.start()
        sends, recvs = [], []
        for k in range(1, N_DEV):
            px = 1 - x if k & 4 else x
            py = 1 - y if k & 2 else y
            pc = 1 - c if k & 1 else c
            peer = 4 * px + 2 * py + pc
            for a in range(n):
                sends.append(pltpu.make_async_remote_copy(
                    src_ref=src_refs[a].at[peer], dst_ref=dst_refs[a].at[me], send_sem=send_sems.at[a, k - 1],
                    recv_sem=recv_sems.at[a, k - 1], device_id=(px, py, pc), device_id_type=MESH))
                recvs.append(pltpu.make_async_remote_copy(
                    src_ref=src_refs[a].at[me], dst_ref=dst_refs[a].at[peer], send_sem=send_sems.at[a, k - 1],
                    recv_sem=recv_sems.at[a, k - 1], device_id=(px, py, pc), device_id_type=MESH))
        for cp in sends:
            cp.start()
        for cp in recvs:
            cp.wait_recv()
        for cp in sends:
            cp.wait_send()
        for cp in mine:
            cp.wait()

    any_spec = pl.BlockSpec(memory_space=pl.ANY)
    return pl.pallas_call(
        body, name=name, out_shape=[jax.ShapeDtypeStruct(s.shape, s.dtype) for s in srcs],
        in_specs=[any_spec] * n, out_specs=[any_spec] * n,
        scratch_shapes=[pltpu.SemaphoreType.DMA((n, 7)), pltpu.SemaphoreType.DMA((n, 7)), pltpu.SemaphoreType.DMA((n,))],
    )(*srcs)


def _adamw(name, parts, w, m, v):
    R, C = w.shape
    tr = _pick(R, (256, 176, 128, R))
    c1 = 1.0 - ADAM_B1 ** ADAM_STEP
    c2 = 1.0 - ADAM_B2 ** ADAM_STEP

    def body(p_ref, w_ref, m_ref, v_ref, g_ref, d_ref, nm_ref, nv_ref):
        g = p_ref[0].astype(F32)
        for d in range(1, N_DEV):
            g = g + p_ref[d].astype(F32)
        nm = ADAM_B1 * m_ref[...] + (1.0 - ADAM_B1) * g
        nv = ADAM_B2 * v_ref[...] + (1.0 - ADAM_B2) * (g * g)
        g_ref[...] = g
        nm_ref[...] = nm
        nv_ref[...] = nv
        d_ref[...] = -ADAM_LR * ((nm / c1) / (jnp.sqrt(nv / c2) + ADAM_EPS) + ADAM_WD * w_ref[...])

    t = pl.BlockSpec((tr, C), lambda i: (i, 0))
    o = jax.ShapeDtypeStruct((R, C), F32)
    return pl.pallas_call(
        body, name=name, grid=(R // tr,),
        in_specs=[pl.BlockSpec((N_DEV, tr, C), lambda i: (0, i, 0)), t, t, t],
        out_specs=[t, t, t, t], out_shape=[o, o, o, o],
        compiler_params=pltpu.CompilerParams(dimension_semantics=("parallel",)),
    )(parts, w, m, v)


def _pack_rows(flats, rows):
    cat = jnp.concatenate(flats, axis=-1)
    return jnp.pad(cat, [(0, rows * PACK_COLS - cat.shape[-1])]).reshape(rows, PACK_COLS)


def _col_shards(full):
    r, cdim = full.shape
    return full.reshape(r, N_DEV, cdim // N_DEV).transpose(1, 0, 2)


def _from_col_shards(sh):
    _, r, c = sh.shape
    return sh.transpose(1, 0, 2).reshape(r, N_DEV * c)


def kernel(x, p, ffn1_w_in, ffn1_w_out, ln1_g, ln1_b, w_mix_in, b_gate, conv_w, a_log, dt_bias, gdn_norm_w, w_branch_sb, w_branch_gdn, w_mix_out, ln2_g, ln2_b, ffn2_w_in, ffn2_w_out, ln3_g, ln3_b, w_ple_gate, b_ple_gate, w_ple, ln4_g, ln4_b, loss_target, m_ffn1_w_in, m_ffn1_w_out, m_ln1_g, m_ln1_b, m_w_mix_in, m_b_gate, m_conv_w, m_a_log, m_dt_bias, m_gdn_norm_w, m_w_branch_sb, m_w_branch_gdn, m_w_mix_out, m_ln2_g, m_ln2_b, m_ffn2_w_in, m_ffn2_w_out, m_ln3_g, m_ln3_b, m_w_ple_gate, m_b_ple_gate, m_w_ple, m_ln4_g, m_ln4_b, v_ffn1_w_in, v_ffn1_w_out, v_ln1_g, v_ln1_b, v_w_mix_in, v_b_gate, v_conv_w, v_a_log, v_dt_bias, v_gdn_norm_w, v_w_branch_sb, v_w_branch_gdn, v_w_mix_out, v_ln2_g, v_ln2_b, v_ffn2_w_in, v_ffn2_w_out, v_ln3_g, v_ln3_b, v_w_ple_gate, v_b_ple_gate, v_w_ple, v_ln4_g, v_ln4_b):
    given = dict(locals())
    w_loc = {n: given[n][0] for n in WEIGHTS}
    m_loc = {n: given["m_" + n][0] for n in WEIGHTS}
    v_loc = {n: given["v_" + n][0] for n in WEIGHTS}
    sizes = {n: w_loc[n].size for n in WEIGHTS}

    conv_hi = conv_w[0].astype(BF16)
    conv_lo = (conv_w[0] - conv_hi.astype(F32)).astype(BF16)
    gathered = _all_gather("gather_weights", [w_loc[n].astype(BF16) for n in BIG] + [conv_lo])
    W = dict(zip(BIG, gathered[:-1]))
    for n in ("ffn1_w_out", "ffn2_w_out"):
        W[n] = W[n].reshape(N_DEV // 2, SHARD_FF, D_MODEL)
    for n in ("w_mix_out", "w_ple_gate"):
        W[n] = W[n].reshape(D_MODEL, D_MODEL)
    wm = _from_col_shards(W.pop("w_mix_in"))
    W["w_mix_pad"] = jnp.concatenate([wm[:, :7 * HEAD_W], wm[:, 7 * HEAD_W + 2 * N_HEADS:],
                                      wm[:, 7 * HEAD_W:7 * HEAD_W + 2 * N_HEADS],
                                      jnp.zeros((D_MODEL, 128 - 2 * N_HEADS), BF16)], axis=1)
    P = {n: w_loc[n].reshape(1, -1) for n in SMALL}
    P["conv_w"] = _from_col_shards(W.pop("conv_w").astype(F32) + gathered[-1].astype(F32))

    loss, grad_x, grads, small = _local_step(x[0], p[0, 0], loss_target[0], W, P)
    loss = lax.psum(loss, ("x", "y", "c"))

    grads["w_mix_in"] = _col_shards(grads["w_mix_in"])
    grads["conv_w"] = _col_shards(grads["conv_w"]).astype(BF16)
    send = [grads[n].reshape(N_DEV, *w_loc[n].shape) for n in BIG]
    parts = dict(zip(BIG, _all_to_all("scatter_grads", send)))
    small_rows = 16
    (small_parts,) = _all_gather("gather_small_grads", [_pack_rows([small[n].reshape(-1) for n in SMALL], small_rows)])

    res = {n: _adamw(f"adamw_{n}", parts[n], w_loc[n], m_loc[n], v_loc[n]) for n in BIG}
    pack = lambda d: _pack_rows([d[n].reshape(-1) for n in SMALL], small_rows)
    small_out = _adamw("adamw_small", small_parts, pack(w_loc), pack(m_loc), pack(v_loc))
    off = 0
    for n in SMALL:
        res[n] = [o.reshape(-1)[off:off + sizes[n]] for o in small_out]
        off += sizes[n]
    outs = [[res[n][i].reshape(given[n].shape) for n in WEIGHTS] for i in range(4)]
    g_out, d_out, nm_out, nv_out = outs
    return (loss, grad_x[None], *g_out, *d_out, *nm_out, *nv_out)
```

```python
import functools
import math

import jax
import jax.numpy as jnp
from jax import lax
from jax.experimental import pallas as pl
from jax.experimental.pallas import tpu as pltpu

F32 = jnp.float32
BF16 = jnp.bfloat16
MESH = pl.DeviceIdType.MESH
HI = lax.Precision.HIGHEST

N_DEV = 8
D_MODEL = 1024
D_FF = 2816
PLE_DIM = 256
N_HEADS = 8
HEAD_DIM = 64
HEAD_W = N_HEADS * HEAD_DIM
GDN_CHUNK = 64
CONV_K = 4
CONV_CH = 3 * HEAD_W
ALPHA = 2.0 ** 0.25
LN_EPS = 1e-5
RMS_EPS = 1e-6
MIX_SB = 0
MIX_GDN = 3 * HEAD_W
MIX_GATES = MIX_GDN + 4 * HEAD_W
MIX_BA = MIX_GATES + 2 * D_MODEL
MIX_PAD = MIX_BA + 128
N_IN = 7 * HEAD_W + 2 * N_HEADS + 2 * D_MODEL

ADAM_LR = 0.001
ADAM_B1 = 0.9
ADAM_B2 = 0.999
ADAM_EPS = 1e-08
ADAM_WD = 0.01
ADAM_STEP = 10

SB_TQ = 512
SB_TK = 128
HALO = 8

BIG = ["ffn1_w_in", "ffn1_w_out", "w_mix_in", "conv_w", "w_branch_sb", "w_branch_gdn", "w_mix_out",
       "ffn2_w_in", "ffn2_w_out", "w_ple_gate", "w_ple"]
BIG_SHAPES = {
    "ffn1_w_in": ((D_MODEL, 2 * D_FF), 1), "ffn1_w_out": ((D_FF, D_MODEL), 0),
    "w_mix_in": ((D_MODEL, N_IN), 1), "conv_w": ((CONV_K, CONV_CH), 1),
    "w_branch_sb": ((HEAD_W, D_MODEL), 1), "w_branch_gdn": ((HEAD_W, D_MODEL), 1),
    "w_mix_out": ((D_MODEL, D_MODEL), 0),
    "ffn2_w_in": ((D_MODEL, 2 * D_FF), 1), "ffn2_w_out": ((D_FF, D_MODEL), 0),
    "w_ple_gate": ((D_MODEL, D_MODEL), 0), "w_ple": ((PLE_DIM, D_MODEL), 1),
}
SMALL = ["ln1_g", "ln1_b", "b_gate", "a_log", "dt_bias", "gdn_norm_w", "ln2_g", "ln2_b", "ln3_g", "ln3_b",
         "b_ple_gate", "ln4_g", "ln4_b"]
WEIGHTS = ["ffn1_w_in", "ffn1_w_out", "ln1_g", "ln1_b", "w_mix_in", "b_gate", "conv_w", "a_log", "dt_bias",
           "gdn_norm_w", "w_branch_sb", "w_branch_gdn", "w_mix_out", "ln2_g", "ln2_b", "ffn2_w_in", "ffn2_w_out",
           "ln3_g", "ln3_b", "w_ple_gate", "b_ple_gate", "w_ple", "ln4_g", "ln4_b"]
PACK_COLS = 1024
ADAM_ROWS = 208


def _pick(n, cands):
    for c in cands:
        if n % c == 0:
            return c
    raise ValueError(f"no tile for {n} in {cands}")


_NN = (((1,), (0,)), ((), ()))
_NT = (((1,), (1,)), ((), ()))
_TN = (((0,), (0,)), ((), ()))
SHARD_FF = 2 * D_FF // N_DEV
ROW_TILES = (1024, 512, 256, 128)


def _matmul(name, dims, a, b, a_spec, b_spec, out_sds, out_spec, out_block, grid, nk):
    def body(a_ref, b_ref, o_ref, acc_ref):
        k = pl.program_id(2)

        @pl.when(k == 0)
        def _():
            acc_ref[...] = jnp.zeros_like(acc_ref)

        acc_ref[...] += lax.dot_general(a_ref[...], b_ref[...], dims, preferred_element_type=F32)

        @pl.when(k == nk - 1)
        def _():
            o_ref[...] = acc_ref[...].astype(o_ref.dtype)

    return pl.pallas_call(
        body, name=name, grid=grid, in_specs=[a_spec, b_spec], out_specs=out_spec, out_shape=out_sds,
        scratch_shapes=[pltpu.VMEM(out_block, F32)],
        compiler_params=pltpu.CompilerParams(dimension_semantics=("parallel", "parallel", "arbitrary")),
    )(a, b)


def _mm(name, a, w, out_dtype, n=None, col_off=0):
    M, K = a.shape
    n = w.shape[1] if n is None else n
    tm = _pick(M, ROW_TILES)
    tn = _pick(n, (512, 384, 256, 128))
    tk = _pick(K, (1024, 512, 256))
    assert col_off % tn == 0
    cb, nk = col_off // tn, K // tk
    return _matmul(
        name, _NN, a, w, pl.BlockSpec((tm, tk), lambda i, j, k: (i, k)), pl.BlockSpec((tk, tn), lambda i, j, k: (k, j + cb)),
        jax.ShapeDtypeStruct((M, n), out_dtype), pl.BlockSpec((tm, tn), lambda i, j, k: (i, j)), (tm, tn),
        (M // tm, n // tn, nk), nk)


def _mm_nt(name, a, w, out_dtype):
    M, K = a.shape
    N = w.shape[0]
    tm = _pick(M, ROW_TILES)
    tn = _pick(N, (1024, 512, 256, 128))
    tk = _pick(K, (1024, 1152, 512, 256))
    nk = K // tk
    return _matmul(
        name, _NT, a, w, pl.BlockSpec((tm, tk), lambda i, j, k: (i, k)), pl.BlockSpec((tn, tk), lambda i, j, k: (j, k)),
        jax.ShapeDtypeStruct((M, N), out_dtype), pl.BlockSpec((tm, tn), lambda i, j, k: (i, j)), (tm, tn),
        (M // tm, N // tn, nk), nk)


def _mm_tn(name, a, b, out_dtype):
    S, M = a.shape
    N = b.shape[1]
    tm = _pick(M, (1024, 512, 256, 128))
    tn = _pick(N, (1024, 640, 512, 256, 128))
    tk = _pick(S, ROW_TILES)
    nk = S // tk
    return _matmul(
        name, _TN, a, b, pl.BlockSpec((tk, tm), lambda i, j, k: (k, i)), pl.BlockSpec((tk, tn), lambda i, j, k: (k, j)),
        jax.ShapeDtypeStruct((M, N), out_dtype), pl.BlockSpec((tm, tn), lambda i, j, k: (i, j)), (tm, tn),
        (M // tm, N // tn, nk), nk)


def _sigmoid(z):
    return 1.0 / (1.0 + jnp.exp(-z))


def _ffn_up(name, xb, w_in):
    S, D = xb.shape
    c = w_in.shape[2]
    ts = _pick(S, ROW_TILES)
    half = N_DEV // 2

    def body(x_ref, wg_ref, wu_ref, a_ref, gu_ref):
        x = x_ref[...]
        g = jnp.dot(x, wg_ref[...], preferred_element_type=F32)
        u = jnp.dot(x, wu_ref[...], preferred_element_type=F32)
        a_ref[...] = (g * _sigmoid(g) * u).astype(a_ref.dtype)
        gu_ref[0] = g.astype(gu_ref.dtype)
        gu_ref[1] = u.astype(gu_ref.dtype)

    return pl.pallas_call(
        body, name=name, grid=(S // ts, half),
        in_specs=[pl.BlockSpec((ts, D), lambda i, j: (i, 0)), pl.BlockSpec((None, D, c), lambda i, j: (j, 0, 0)),
                  pl.BlockSpec((None, D, c), lambda i, j: (j + half, 0, 0))],
        out_specs=[pl.BlockSpec((None, ts, c), lambda i, j: (j, i, 0)),
                   pl.BlockSpec((2, None, ts, c), lambda i, j: (0, j, i, 0))],
        out_shape=[jax.ShapeDtypeStruct((half, S, c), BF16), jax.ShapeDtypeStruct((2, half, S, c), BF16)],
        compiler_params=pltpu.CompilerParams(dimension_semantics=("parallel", "parallel")),
    )(xb, w_in, w_in)


def _ffn_down(name, a4, w_out4):
    n, S, c = a4.shape
    D = w_out4.shape[2]
    tm = _pick(S, ROW_TILES)
    tn = _pick(D, (1024, 512, 256, 128))
    return _matmul(
        name, _NN, a4, w_out4, pl.BlockSpec((None, tm, c), lambda i, j, k: (k, i, 0)),
        pl.BlockSpec((None, c, tn), lambda i, j, k: (k, 0, j)),
        jax.ShapeDtypeStruct((S, D), F32), pl.BlockSpec((tm, tn), lambda i, j, k: (i, j)), (tm, tn),
        (S // tm, D // tn, n), n)


def _ffn_bwd_act(name, dyb, w_out4, gu):
    S, D = dyb.shape
    n, c, _ = w_out4.shape
    ts = _pick(S, ROW_TILES)

    def body(dy_ref, w_ref, gu_ref, o_ref):
        da = lax.dot_general(dy_ref[...], w_ref[...], _NT, preferred_element_type=F32)
        g = gu_ref[0].astype(F32)
        u = gu_ref[1].astype(F32)
        sg = _sigmoid(g)
        o_ref[0] = (da * u * (sg * (1.0 + g * (1.0 - sg)))).astype(o_ref.dtype)
        o_ref[1] = (da * (g * sg)).astype(o_ref.dtype)

    blk = pl.BlockSpec((2, None, ts, c), lambda i, j: (0, j, i, 0))
    return pl.pallas_call(
        body, name=name, grid=(S // ts, n),
        in_specs=[pl.BlockSpec((ts, D), lambda i, j: (i, 0)), pl.BlockSpec((None, c, D), lambda i, j: (j, 0, 0)), blk],
        out_specs=blk, out_shape=jax.ShapeDtypeStruct((2, n, S, c), BF16),
        compiler_params=pltpu.CompilerParams(dimension_semantics=("parallel", "parallel")),
    )(dyb, w_out4, gu)


def _ffn_d_w_out(name, a4, dyb):
    n, S, c = a4.shape
    D = dyb.shape[1]
    tn = _pick(D, (1024, 512, 256, 128))
    tk = _pick(S, ROW_TILES)
    nk = S // tk
    return _matmul(
        name, _TN, a4, dyb, pl.BlockSpec((None, tk, c), lambda i, j, k: (i, k, 0)), pl.BlockSpec((tk, tn), lambda i, j, k: (k, j)),
        jax.ShapeDtypeStruct((n, c, D), BF16), pl.BlockSpec((None, c, tn), lambda i, j, k: (i, 0, j)), (c, tn),
        (n, D // tn, nk), nk)


def _ffn_d_w_in(name, hb, dgu8):
    S, D = hb.shape
    n, _, c = dgu8.shape
    tm = _pick(D, (1024, 512, 256, 128))
    tk = _pick(S, ROW_TILES)
    nk = S // tk
    return _matmul(
        name, _TN, hb, dgu8, pl.BlockSpec((tk, tm), lambda i, j, k: (k, j)), pl.BlockSpec((None, tk, c), lambda i, j, k: (i, k, 0)),
        jax.ShapeDtypeStruct((n, D, c), BF16), pl.BlockSpec((None, tm, c), lambda i, j, k: (i, j, 0)), (tm, c),
        (n, D // tm, nk), nk)


def _ffn_in_bwd(name, dgu8, w_in):
    n, S, c = dgu8.shape
    D = w_in.shape[1]
    tm = _pick(S, ROW_TILES)
    tn = _pick(D, (1024, 512, 256, 128))
    return _matmul(
        name, _NT, dgu8, w_in, pl.BlockSpec((None, tm, c), lambda i, j, k: (k, i, 0)),
        pl.BlockSpec((None, tn, c), lambda i, j, k: (k, j, 0)),
        jax.ShapeDtypeStruct((S, D), F32), pl.BlockSpec((tm, tn), lambda i, j, k: (i, j)), (tm, tn),
        (S // tm, D // tn, n), n)


def _rowwise(name, fn, tiled, params, outs, reds=(), ts=256):
    S = tiled[0][0].shape[0]
    ts = _pick(S, (ts, 128))
    n_t, n_p, n_o = len(tiled), len(params), len(outs)

    def body(*refs):
        vals = [r[...] for r in refs[:n_t + n_p]]
        res_o, res_r = fn(*vals)
        o_refs = refs[n_t + n_p:n_t + n_p + n_o]
        r_refs = refs[n_t + n_p + n_o:]
        for r, val in zip(o_refs, res_o, strict=True):
            r[...] = val.astype(r.dtype)
        if r_refs:
            i = pl.program_id(0)

            @pl.when(i == 0)
            def _():
                for r, val in zip(r_refs, res_r, strict=True):
                    r[...] = val.astype(F32)

            @pl.when(i > 0)
            def _():
                for r, val in zip(r_refs, res_r, strict=True):
                    r[...] += val.astype(F32)

    in_specs = [pl.BlockSpec((ts, w), functools.partial(lambda i, cb: (i, cb), cb=cb)) for _, cb, w in tiled]
    in_specs += [pl.BlockSpec(p.shape, lambda i: (0, 0)) for p in params]
    out_specs = [pl.BlockSpec((ts, w), lambda i: (i, 0)) for w, _ in outs]
    out_specs += [pl.BlockSpec(tuple(r), lambda i: (0, 0)) for r in reds]
    out_shape = [jax.ShapeDtypeStruct((S, w), dt) for w, dt in outs]
    out_shape += [jax.ShapeDtypeStruct(tuple(r), F32) for r in reds]
    res = pl.pallas_call(
        body, name=name, grid=(S // ts,), in_specs=in_specs, out_specs=out_specs, out_shape=out_shape,
        compiler_params=pltpu.CompilerParams(dimension_semantics=("arbitrary",)),
    )(*[t[0] for t in tiled], *params)
    return res[:n_o], res[n_o:]


def _ln(r, g, b):
    mu = jnp.mean(r, axis=-1, keepdims=True)
    xc = r - mu
    var = jnp.mean(xc * xc, axis=-1, keepdims=True)
    return xc * lax.rsqrt(var + LN_EPS) * g + b


def _ln_bwd(r, g, b, dy):
    _, vjp = jax.vjp(_ln, r, g, b)
    return vjp(dy)


def _shift_down(x, d):
    return x if d == 0 else pltpu.roll(x, d, 0)


def _conv_fwd(name, x, w):
    S = x.shape[0]
    ts = _pick(S, (256, 128))
    hb = ts // HALO

    def body(x_ref, prev_ref, w_ref, o_ref):
        i = pl.program_id(0)
        prev = jnp.where(i > 0, prev_ref[...], 0.0)
        xe = jnp.concatenate([prev, x_ref[...]], axis=0)
        y = jnp.zeros((ts + HALO, CONV_CH), F32)
        for j in range(CONV_K):
            y = y + w_ref[pl.ds(j, 1), :] * _shift_down(xe, CONV_K - 1 - j)
        y = y[HALO:, :]
        o_ref[...] = y * _sigmoid(y)

    return pl.pallas_call(
        body, name=name, grid=(S // ts,),
        in_specs=[pl.BlockSpec((ts, CONV_CH), lambda i: (i, 0)),
                  pl.BlockSpec((HALO, CONV_CH), lambda i: (jnp.maximum(i * hb - 1, 0), 0)),
                  pl.BlockSpec((CONV_K, CONV_CH), lambda i: (0, 0))],
        out_specs=pl.BlockSpec((ts, CONV_CH), lambda i: (i, 0)),
        out_shape=jax.ShapeDtypeStruct((S, CONV_CH), F32),
        compiler_params=pltpu.CompilerParams(dimension_semantics=("arbitrary",)),
    )(x, x, w)


def _conv_bwd(name, x, w, dout):
    S = x.shape[0]
    ts = _pick(S, (256, 128))
    hb = ts // HALO
    nt = S // ts
    n_ext = ts + 2 * HALO

    def body(x_ref, prev_ref, next_ref, w_ref, d_ref, dnext_ref, dx_ref, dw_ref):
        i = pl.program_id(0)
        prev = jnp.where(i > 0, prev_ref[...], 0.0)
        last = i == nt - 1
        nxt = jnp.where(last, 0.0, next_ref[...])
        dnxt = jnp.where(last, 0.0, dnext_ref[...])
        xe = jnp.concatenate([prev, x_ref[...], nxt], axis=0)
        de = jnp.concatenate([jnp.zeros((HALO, CONV_CH), F32), d_ref[...], dnxt], axis=0)
        y = jnp.zeros((n_ext, CONV_CH), F32)
        for j in range(CONV_K):
            y = y + w_ref[pl.ds(j, 1), :] * _shift_down(xe, CONV_K - 1 - j)
        sg = _sigmoid(y)
        dy = de * (sg * (1.0 + y * (1.0 - sg)))
        dx = jnp.zeros((n_ext, CONV_CH), F32)
        for j in range(CONV_K):
            m = CONV_K - 1 - j
            dx = dx + w_ref[pl.ds(j, 1), :] * (dy if m == 0 else pltpu.roll(dy, n_ext - m, 0))
        dx_ref[...] = dx[HALO:HALO + ts, :].astype(dx_ref.dtype)
        row = lax.broadcasted_iota(jnp.int32, (n_ext, 1), 0)
        dy_own = jnp.where((row >= HALO) & (row < HALO + ts), dy, 0.0)
        parts = [jnp.sum(dy_own * _shift_down(xe, CONV_K - 1 - j), axis=0, keepdims=True) for j in range(CONV_K)]
        dw = jnp.concatenate(parts + [jnp.zeros((HALO - CONV_K, CONV_CH), F32)], axis=0)

        @pl.when(i == 0)
        def _():
            dw_ref[...] = dw

        @pl.when(i > 0)
        def _():
            dw_ref[...] += dw

    tile = pl.BlockSpec((ts, CONV_CH), lambda i: (i, 0))
    prev = pl.BlockSpec((HALO, CONV_CH), lambda i: (jnp.maximum(i * hb - 1, 0), 0))
    nxt = pl.BlockSpec((HALO, CONV_CH), lambda i: (jnp.minimum((i + 1) * hb, nt * hb - 1), 0))
    return pl.pallas_call(
        body, name=name, grid=(nt,),
        in_specs=[tile, prev, nxt, pl.BlockSpec((CONV_K, CONV_CH), lambda i: (0, 0)), tile, nxt],
        out_specs=[tile, pl.BlockSpec((HALO, CONV_CH), lambda i: (0, 0))],
        out_shape=[jax.ShapeDtypeStruct((S, CONV_CH), BF16), jax.ShapeDtypeStruct((HALO, CONV_CH), F32)],
        compiler_params=pltpu.CompilerParams(dimension_semantics=("arbitrary",)),
    )(x, x, x, w, dout, dout)


def _softplus(z):
    return jnp.maximum(z, 0.0) + jnp.log(1.0 + jnp.exp(-jnp.abs(z)))


def _dot2(x, tri2):
    hi = x.astype(BF16)
    lo = (x - hi.astype(F32)).astype(BF16)
    return jnp.dot(jnp.concatenate([hi, lo], axis=1), tri2, preferred_element_type=F32)


def _pair_tri(tk, keep):
    r = lax.broadcasted_iota(jnp.int32, (4 * tk, 2 * tk), 0)
    c = lax.broadcasted_iota(jnp.int32, (4 * tk, 2 * tk), 1)
    same_head = ((r // tk) % 2) == (c // tk)
    return (same_head & keep(r % tk, c % tk)).astype(BF16)


def _pair_rows(x2, first):
    return jnp.concatenate([jnp.where(first, x2, 0), jnp.where(first, 0, x2)], axis=0)


def _pair_sum(x):
    h = x.shape[1] // 2
    return jnp.sum(x[:, :h], axis=1, keepdims=True), jnp.sum(x[:, h:], axis=1, keepdims=True)


def _sb_tiles(S):
    tq = _pick(S, (SB_TQ, 256, 128))
    return tq, SB_TK


N_PAIRS = N_HEADS // 2


def _sb_specs(S, tq):
    q = pl.BlockSpec((tq, 128), lambda p, i: (i, p))
    k = pl.BlockSpec((S, 128), lambda p, i: (0, N_PAIRS + p))
    v = pl.BlockSpec((S, 128), lambda p, i: (0, 2 * N_PAIRS + p))
    return q, k, v


def _scaled(q):
    return (q.astype(F32) * (HEAD_DIM ** -0.5)).astype(BF16)


def _sb_fwd(qkv):
    S = qkv.shape[0]
    tq, tk = _sb_tiles(S)
    nd = tq // tk

    def body(q_ref, k_ref, v_ref, o_ref, ta_ref, tb_ref):
        qi = pl.program_id(1)
        qs = _scaled(q_ref[...])
        row = qi * tq + lax.broadcasted_iota(jnp.int32, (tq, 2 * tk), 0)
        col = lax.broadcasted_iota(jnp.int32, (tq, 2 * tk), 1)
        left = col < tk
        col = col % tk
        later = _pair_tri(tk, lambda r, c: r > c)
        first = lax.broadcasted_iota(jnp.int32, (tk, 128), 1) < HEAD_DIM

        def tile(j, carry, masked):
            acc, suf_a, suf_b = carry
            off = pl.multiple_of(j * tk, tk)
            kc = _pair_rows(k_ref[pl.ds(off, tk), :], first)
            vc = _pair_rows(v_ref[pl.ds(off, tk), :], first)
            mask = (col + j * tk) < row
            z = lax.dot_general(qs, kc, _NT, preferred_element_type=F32)
            lf = -_softplus(z)
            if masked:
                lf = jnp.where(mask, lf, 0.0)
            w = jnp.exp(z + lf + _dot2(lf, later) + jnp.where(left, suf_a, suf_b))
            if masked:
                w = jnp.where(mask, w, 0.0)
            acc = acc + jnp.dot(w.astype(BF16), vc, preferred_element_type=F32)
            sum_a, sum_b = _pair_sum(lf)
            return acc, suf_a + sum_a, suf_b + sum_b

        zero1 = jnp.zeros((tq, 1), F32)
        carry = (jnp.zeros((tq, 128), F32), zero1, zero1)
        n_full = qi * nd
        carry = lax.fori_loop(0, nd, lambda d, c: tile(n_full + nd - 1 - d, c, True), carry)
        acc, suf_a, suf_b = lax.fori_loop(
            0, n_full // 2, lambda jj, c: tile(n_full - 2 - 2 * jj, tile(n_full - 1 - 2 * jj, c, False), False), carry)
        o_ref[...] = acc.astype(o_ref.dtype)
        ta_ref[...] = suf_a
        tb_ref[...] = suf_b

    tot = pl.BlockSpec((None, tq, 1), lambda p, i: (p, i, 0))
    tshape = jax.ShapeDtypeStruct((N_PAIRS, S, 1), F32)
    return pl.pallas_call(
        body, name="sb_fwd", grid=(N_PAIRS, S // tq), in_specs=list(_sb_specs(S, tq)),
        out_specs=[pl.BlockSpec((tq, 128), lambda p, i: (i, p)), tot, tot],
        out_shape=[jax.ShapeDtypeStruct((S, HEAD_W), BF16), tshape, tshape],
        compiler_params=pltpu.CompilerParams(dimension_semantics=("arbitrary", "arbitrary")),
    )(qkv, qkv, qkv)


def _sb_bwd(qkv, tot_a, tot_b, do):
    S = qkv.shape[0]
    tq, tk = _sb_tiles(S)
    nd = tq // tk
    scale = HEAD_DIM ** -0.5

    def body(q_ref, k_ref, v_ref, ta_ref, tb_ref, do_ref, dq_ref, dk_ref, dv_ref):
        qi = pl.program_id(1)

        @pl.when(qi == 0)
        def _():
            dk_ref[...] = jnp.zeros_like(dk_ref)
            dv_ref[...] = jnp.zeros_like(dv_ref)

        qs = _scaled(q_ref[...])
        do2 = do_ref[...]
        row = qi * tq + lax.broadcasted_iota(jnp.int32, (tq, 2 * tk), 0)
        col = lax.broadcasted_iota(jnp.int32, (tq, 2 * tk), 1)
        left = col < tk
        col = col % tk
        total = jnp.where(left, ta_ref[...], tb_ref[...])
        upto = _pair_tri(tk, lambda r, c: r <= c)
        first = lax.broadcasted_iota(jnp.int32, (tk, 128), 1) < HEAD_DIM

        def tile(j, carry, masked):
            dq, pl_a, pl_b, pg_a, pg_b = carry
            off = pl.multiple_of(j * tk, tk)
            kc = _pair_rows(k_ref[pl.ds(off, tk), :], first)
            vc = _pair_rows(v_ref[pl.ds(off, tk), :], first)
            mask = (col + j * tk) < row
            z = lax.dot_general(qs, kc, _NT, preferred_element_type=F32)
            lf = -_softplus(z)
            sig = jnp.exp(z + lf)
            if masked:
                lf = jnp.where(mask, lf, 0.0)
            w = jnp.exp(z + lf + (total - (jnp.where(left, pl_a, pl_b) + _dot2(lf, upto))))
            if masked:
                w = jnp.where(mask, w, 0.0)
            gl = lax.dot_general(do2, vc, _NT, preferred_element_type=F32) * w
            dz = gl - sig * (jnp.where(left, pg_a, pg_b) + _dot2(gl, upto))
            if masked:
                dz = jnp.where(mask, dz, 0.0)
            dzb = dz.astype(BF16)
            dq = dq + jnp.dot(dzb, kc, preferred_element_type=F32)
            dkc = lax.dot_general(dzb, qs, _TN, preferred_element_type=F32)
            dvc = lax.dot_general(w.astype(BF16), do2, _TN, preferred_element_type=F32)
            dk_ref[pl.ds(off, tk), :] += jnp.where(first, dkc[:tk], dkc[tk:])
            dv_ref[pl.ds(off, tk), :] += jnp.where(first, dvc[:tk], dvc[tk:])
            sl_a, sl_b = _pair_sum(lf)
            sg_a, sg_b = _pair_sum(gl)
            return dq, pl_a + sl_a, pl_b + sl_b, pg_a + sg_a, pg_b + sg_b

        zero1 = jnp.zeros((tq, 1), F32)
        carry = (jnp.zeros((tq, 128), F32), zero1, zero1, zero1, zero1)
        n_full = qi * nd
        carry = lax.fori_loop(0, n_full // 2, lambda j, c: tile(2 * j + 1, tile(2 * j, c, False), False), carry)
        dq = lax.fori_loop(0, nd, lambda d, c: tile(n_full + d, c, True), carry)[0]
        dq_ref[...] = dq * scale

    q_spec, k_spec, v_spec = _sb_specs(S, tq)
    tot = pl.BlockSpec((None, tq, 1), lambda p, i: (p, i, 0))
    whole = pl.BlockSpec((S, 128), lambda p, i: (0, p))
    big = jax.ShapeDtypeStruct((S, HEAD_W), F32)
    return pl.pallas_call(
        body, name="sb_bwd", grid=(N_PAIRS, S // tq),
        in_specs=[q_spec, k_spec, v_spec, tot, tot, q_spec],
        out_specs=[q_spec, whole, whole], out_shape=[big, big, big],
        compiler_params=pltpu.CompilerParams(dimension_semantics=("arbitrary", "arbitrary")),
    )(qkv, qkv, qkv, tot_a, tot_b, do)


GDN_PRECISION = lax.Precision.HIGH


def _bmm(a, b):
    return lax.dot_general(a, b, (((2,), (1,)), ((0,), (0,))), precision=GDN_PRECISION, preferred_element_type=F32)


def _bmm_nt(a, b):
    return lax.dot_general(a, b, (((2,), (2,)), ((0,), (0,))), precision=GDN_PRECISION, preferred_element_type=F32)


def _bmm_tn(a, b):
    return lax.dot_general(a, b, (((1,), (1,)), ((0,), (0,))), precision=GDN_PRECISION, preferred_element_type=F32)


def _tri_inv(lower):
    C = lower.shape[-1]
    ii = lax.broadcasted_iota(jnp.int32, (C, C), 0)
    jj = lax.broadcasted_iota(jnp.int32, (C, C), 1)
    eye = (ii == jj).astype(F32)[None]
    xd = jnp.where((ii // 8 == jj // 8)[None], -lower, 0.0)
    x2 = _bmm(xd, xd)
    x4 = _bmm(x2, x2)
    inv = eye + xd
    inv = inv + _bmm(inv, x2)
    inv = inv + _bmm(inv, x4)
    b = 8
    while b < C:
        off = jnp.where(((ii // (2 * b) == jj // (2 * b)) & (ii // b != jj // b))[None], lower, 0.0)
        inv = inv - _bmm(inv, _bmm(off, inv))
        b *= 2
    return inv


@jax.custom_vjp
def _tri_solve(lower, rhs):
    return _bmm(_tri_inv(lower), rhs)


def _tri_solve_fwd(lower, rhs):
    inv = _tri_inv(lower)
    sol = _bmm(inv, rhs)
    return sol, (inv, sol)


def _tri_solve_bwd(res, dsol):
    inv, sol = res
    drhs = _bmm_tn(inv, dsol)
    C = inv.shape[-1]
    ii = lax.broadcasted_iota(jnp.int32, (C, C), 0)
    jj = lax.broadcasted_iota(jnp.int32, (C, C), 1)
    return jnp.where((jj < ii)[None], -_bmm_nt(drhs, sol), 0.0), drhs


_tri_solve.defvjp(_tri_solve_fwd, _tri_solve_bwd)


def _pairs(x):
    return jnp.stack([x[:, 128 * p:128 * (p + 1)] for p in range(N_PAIRS)], axis=0)


def _unpairs(x):
    return jnp.concatenate([x[p] for p in range(N_PAIRS)], axis=1)


def _gdn_chunk(state, qkv, gz, ba, a_log_x, dt_x, norm_x):
    C = qkv.shape[0]
    lane = lax.broadcasted_iota(jnp.int32, (1, 1, 128), 2)
    first = lane < HEAD_DIM

    def split_heads(x2):
        return jnp.stack([jnp.where(first, x2, 0.0), jnp.where(first, 0.0, x2)], axis=1).reshape(N_HEADS, *x2.shape[1:])

    def merge_heads(xh):
        x = xh.reshape(N_PAIRS, 2, *xh.shape[1:])
        return x[:, 0] + x[:, 1]

    def head_cols(x2):
        a = jnp.sum(jnp.where(lane == 0, x2, 0.0), axis=-1, keepdims=True)
        b = jnp.sum(jnp.where(lane == HEAD_DIM, x2, 0.0), axis=-1, keepdims=True)
        return jnp.stack([a, b], axis=1).reshape(N_HEADS, *a.shape[1:])

    def to_pair(xh):
        x = xh.reshape(N_PAIRS, 2, *xh.shape[1:])
        return jnp.where(first, x[:, 0], x[:, 1])

    def head_sums(x2):
        a = jnp.sum(jnp.where(first, x2, 0.0), axis=-1, keepdims=True)
        b = jnp.sum(jnp.where(first, 0.0, x2), axis=-1, keepdims=True)
        return jnp.where(first, a, b)

    er = lax.broadcasted_iota(jnp.int32, (128, 2 * HEAD_W), 0)
    ec = lax.broadcasted_iota(jnp.int32, (128, 2 * HEAD_W), 1)
    spread = (er == ec // HEAD_DIM).astype(F32)
    bx = lax.dot_general(ba, spread, (((1,), (0,)), ((), ())), precision=HI, preferred_element_type=F32)
    beta2 = _pairs(_sigmoid(bx[:, :HEAD_W]))
    g2 = _pairs(-jnp.exp(a_log_x) * _softplus(bx[:, HEAD_W:] + dt_x))
    beta = head_cols(beta2)
    g = head_cols(g2)

    q2, k2, v2 = (_pairs(qkv[:, i * HEAD_W:(i + 1) * HEAD_W]) for i in range(3))
    qn2 = q2 * lax.rsqrt(head_sums(q2 * q2) + RMS_EPS) * (HEAD_DIM ** -0.5)
    kn2 = k2 * lax.rsqrt(head_sums(k2 * k2) + RMS_EPS)
    knh = split_heads(kn2)

    ii = lax.broadcasted_iota(jnp.int32, (C, C), 0)[None]
    jj = lax.broadcasted_iota(jnp.int32, (C, C), 1)[None]
    incl = jj <= ii
    g_row = jnp.sum(jnp.where(ii == jj, g, 0.0), axis=1, keepdims=True)
    gc_col = jnp.sum(jnp.where(incl, g_row, 0.0), axis=2, keepdims=True)
    gc_row = jnp.sum(jnp.where(ii <= jj, g, 0.0), axis=1, keepdims=True)
    decay = jnp.where(incl, jnp.exp(jnp.where(incl, gc_col - gc_row, 0.0)), 0.0)
    lower = jnp.where(jj < ii, beta * _bmm_nt(knh, knh) * decay, 0.0)
    gc2 = to_pair(gc_col)
    egc2 = jnp.exp(gc2)
    u2 = merge_heads(_tri_solve(lower, split_heads(v2 * beta2)))
    w2 = merge_heads(_tri_solve(lower, split_heads(kn2 * (beta2 * egc2))))
    qk = jnp.where(incl, _bmm_nt(jnp.repeat(qn2, 2, axis=0), knh) * decay, 0.0)
    g_last2 = to_pair(jnp.sum(g, axis=1, keepdims=True))
    v_new2 = u2 - _bmm(w2, state)
    o2 = _bmm(qn2 * egc2, state) + merge_heads(_bmm(qk, split_heads(v_new2)))
    sr = lax.broadcasted_iota(jnp.int32, (128, 128), 0)
    sc = lax.broadcasted_iota(jnp.int32, (128, 128), 1)
    same_head = ((sr < HEAD_DIM) == (sc < HEAD_DIM))[None]
    new_state = state * jnp.exp(g_last2) + jnp.where(same_head, _bmm_tn(kn2 * jnp.exp(g_last2 - gc2), v_new2), 0.0)
    o2 = o2 * lax.rsqrt(head_sums(o2 * o2) * (1.0 / HEAD_DIM) + RMS_EPS) * _pairs(norm_x)
    gz2 = _pairs(gz)
    return new_state, _unpairs(o2 * (gz2 * _sigmoid(gz2)))


def _gdn_specs(order):
    C = GDN_CHUNK
    par = pl.BlockSpec((1, HEAD_W), lambda c: (0, 0))
    return [pl.BlockSpec((C, CONV_CH), lambda c: (order(c), 0)), pl.BlockSpec((C, HEAD_W), lambda c: (order(c), 3)),
            pl.BlockSpec((C, 128), lambda c: (order(c), 0)), par, par, par]


def _gdn_fwd(qkv, gdnp, ba, a_log_x, dt_x, norm_x):
    S = qkv.shape[0]
    C = GDN_CHUNK
    n = S // C

    def body(qkv_ref, gz_ref, ba_ref, al_ref, dt_ref, nw_ref, o_ref, st_ref, state):
        @pl.when(pl.program_id(0) == 0)
        def _():
            state[...] = jnp.zeros_like(state)

        st = state[...]
        st_ref[...] = st
        new, o = _gdn_chunk(st, qkv_ref[...], gz_ref[...], ba_ref[...], al_ref[...], dt_ref[...], nw_ref[...])
        state[...] = new
        o_ref[...] = o.astype(o_ref.dtype)

    return pl.pallas_call(
        body, name="gdn_fwd", grid=(n,), in_specs=_gdn_specs(lambda c: c),
        out_specs=[pl.BlockSpec((C, HEAD_W), lambda c: (c, 0)),
                   pl.BlockSpec((None, N_PAIRS, 128, 128), lambda c: (c, 0, 0, 0))],
        out_shape=[jax.ShapeDtypeStruct((S, HEAD_W), BF16), jax.ShapeDtypeStruct((n, N_PAIRS, 128, 128), F32)],
        scratch_shapes=[pltpu.VMEM((N_PAIRS, 128, 128), F32)],
        compiler_params=pltpu.CompilerParams(dimension_semantics=("arbitrary",)),
    )(qkv, gdnp, ba, a_log_x, dt_x, norm_x)


def _gdn_bwd(qkv, gdnp, ba, a_log_x, dt_x, norm_x, states, do):
    S = qkv.shape[0]
    C = GDN_CHUNK
    n = S // C

    def body(qkv_ref, gz_ref, ba_ref, al_ref, dt_ref, nw_ref, st_ref, do_ref,
             dqkv_ref, dgz_ref, dba_ref, dal_ref, ddt_ref, dnw_ref, dstate):
        @pl.when(pl.program_id(0) == 0)
        def _():
            dstate[...] = jnp.zeros_like(dstate)
            dal_ref[...] = jnp.zeros_like(dal_ref)
            ddt_ref[...] = jnp.zeros_like(ddt_ref)
            dnw_ref[...] = jnp.zeros_like(dnw_ref)

        args = (st_ref[...], qkv_ref[...], gz_ref[...], ba_ref[...], al_ref[...], dt_ref[...], nw_ref[...])
        _, vjp = jax.vjp(_gdn_chunk, *args)
        dst, dqkv, dgz, dba, dal, ddt, dnw = vjp((dstate[...], do_ref[...]))
        dstate[...] = dst
        dqkv_ref[...] = dqkv
        dgz_ref[...] = dgz.astype(dgz_ref.dtype)
        dba_ref[...] = dba
        dal_ref[...] += dal
        ddt_ref[...] += ddt
        dnw_ref[...] += dnw

    rev = lambda c: n - 1 - c
    par = pl.BlockSpec((1, HEAD_W), lambda c: (0, 0))
    f = jax.ShapeDtypeStruct
    return pl.pallas_call(
        body, name="gdn_bwd", grid=(n,),
        in_specs=_gdn_specs(rev) + [pl.BlockSpec((None, N_PAIRS, 128, 128), lambda c: (rev(c), 0, 0, 0)),
                                    pl.BlockSpec((C, HEAD_W), lambda c: (rev(c), 0))],
        out_specs=[pl.BlockSpec((C, CONV_CH), lambda c: (rev(c), 0)), pl.BlockSpec((C, HEAD_W), lambda c: (rev(c), 0)),
                   pl.BlockSpec((C, 128), lambda c: (rev(c), 0)), par, par, par],
        out_shape=[f((S, CONV_CH), F32), f((S, HEAD_W), BF16), f((S, 128), F32)] + [f((1, HEAD_W), F32)] * 3,
        scratch_shapes=[pltpu.VMEM((N_PAIRS, 128, 128), F32)],
        compiler_params=pltpu.CompilerParams(dimension_semantics=("arbitrary",)),
    )(qkv, gdnp, ba, a_log_x, dt_x, norm_x, states, do)


def _local_step(x, p, target, W, P):
    S = x.shape[0]
    D = D_MODEL
    xb = x.astype(BF16)
    pb = p.astype(BF16)

    def ffn_fwd(tag, h, hb, w_in, w_out, g, b):
        a, gu = _ffn_up(f"{tag}_up", hb, w_in)
        f = _ffn_down(f"{tag}_down", a, w_out)

        def fn(h, f, g, b):
            r = ALPHA * h + 0.5 * f
            y = _ln(r, g, b)
            return (r, y, y), ()

        (r, y, yb), _ = _rowwise(f"{tag}_ln", fn, [(h, 0, D), (f, 0, D)], [g, b], [(D, F32), (D, F32), (D, BF16)])
        return a, gu, r, y, yb

    a1, gu1, r1, h1, h1b = ffn_fwd("ffn1", x, xb, W["ffn1_w_in"], W["ffn1_w_out"], P["ln1_g"], P["ln1_b"])

    wmix = W["w_mix_pad"]
    sbp = _mm("mix_sb", h1b, wmix, BF16, n=3 * HEAD_W, col_off=MIX_SB)
    gdnp = _mm("mix_gdn", h1b, wmix, F32, n=4 * HEAD_W, col_off=MIX_GDN)
    gates = _mm("mix_gates", h1b, wmix, F32, n=2 * D, col_off=MIX_GATES)
    ba = _mm("mix_ba", h1b, wmix, F32, n=128, col_off=MIX_BA)

    attb, sb_tot_a, sb_tot_b = _sb_fwd(sbp)
    y_sb = _mm("sb_out", attb, W["w_branch_sb"], F32)

    conv_w = P["conv_w"]
    qkv = _conv_fwd("conv_fwd", gdnp, conv_w)
    a_log_x = jnp.repeat(P["a_log"], HEAD_DIM, axis=1)
    dt_x = jnp.repeat(P["dt_bias"], HEAD_DIM, axis=1)
    norm_x = jnp.tile(P["gdn_norm_w"], (1, N_HEADS))
    gob, states = _gdn_fwd(qkv, gdnp, ba, a_log_x, dt_x, norm_x)
    y_gdn = _mm("gdn_out", gob, W["w_branch_gdn"], F32)

    def merge_fn(gs, gg, ys, yg, bs, bg):
        return ((_sigmoid(gs + bs) * ys + _sigmoid(gg + bg) * yg),), ()

    b_gate = P["b_gate"]
    bs, bg = b_gate[:, :D], b_gate[:, D:]
    (merged,), _ = _rowwise("mix_merge", merge_fn, [(gates, 0, D), (gates, 1, D), (y_sb, 0, D), (y_gdn, 0, D)], [bs, bg],
                            [(D, BF16)])
    mix = _mm("mix_out", merged, W["w_mix_out"], F32)

    def ln2_fn(h, f, g, b):
        r = ALPHA * h + f
        y = _ln(r, g, b)
        return (r, y, y), ()

    (r2, h2, h2b), _ = _rowwise("mix_ln", ln2_fn, [(h1, 0, D), (mix, 0, D)], [P["ln2_g"], P["ln2_b"]],
                                [(D, F32), (D, F32), (D, BF16)])

    a2, gu2, r3, h3, h3b = ffn_fwd("ffn2", h2, h2b, W["ffn2_w_in"], W["ffn2_w_out"], P["ln3_g"], P["ln3_b"])

    zg = _mm("ple_gate", h3b, W["w_ple_gate"], F32)
    pp = _mm("ple_proj", pb, W["w_ple"], F32)

    def ple(h, zg, pp, bp, g, b):
        return _ln(ALPHA * h + _sigmoid(zg + bp) * pp, g, b)

    def head_fn(h, zg, pp, tgt, bp, g, b):
        y, vjp = jax.vjp(ple, h, zg, pp, bp, g, b)
        err = y - tgt
        dh, dzg, dpp, dbp, dg, db = vjp(err * (1.0 / D))
        loss = 0.5 * jnp.sum(jnp.sum(err * err, axis=1, keepdims=True), axis=0, keepdims=True) * (1.0 / D)
        return (dh, dzg, dpp), (loss, dbp, dg, db)

    (dh3_a, dzg, dpp), (loss, d_bple, d_ln4g, d_ln4b) = _rowwise(
        "ple_head", head_fn, [(h3, 0, D), (zg, 0, D), (pp, 0, D), (target, 0, D)],
        [P["b_ple_gate"], P["ln4_g"], P["ln4_b"]], [(D, F32), (D, BF16), (D, BF16)],
        [(1, 1), (1, D), (1, D), (1, D)])

    grads, small = {}, {"b_ple_gate": d_bple, "ln4_g": d_ln4g, "ln4_b": d_ln4b}
    grads["w_ple_gate"] = _mm_tn("d_w_ple_gate", h3b, dzg, BF16)
    grads["w_ple"] = _mm_tn("d_w_ple", pb, dpp, BF16)
    dh3_b = _mm_nt("d_ple_gate_in", dzg, W["w_ple_gate"], F32)

    def ffn_bwd(tag, dy_parts, r, g, b, a, gu, hb_in, w_in, w_out):
        n_parts = len(dy_parts)

        def fn(*vals):
            dy = vals[0]
            for extra in vals[1:n_parts]:
                dy = dy + extra
            r, g, b = vals[n_parts:]
            dr, dg, db = _ln_bwd(r, g, b, dy)
            return (ALPHA * dr, 0.5 * dr), (dg, db)

        (dh_res, dfb), (dg, db) = _rowwise(f"{tag}_ln_bwd", fn, [(t, 0, D) for t in dy_parts] + [(r, 0, D)], [g, b],
                                           [(D, F32), (D, BF16)], [(1, D), (1, D)])
        dgu = _ffn_bwd_act(f"{tag}_act_bwd", dfb, w_out, gu)
        dgu8 = dgu.reshape(N_DEV, S, SHARD_FF)
        d_w_out = _ffn_d_w_out(f"d_{tag}_w_out", a, dfb)
        d_w_in = _ffn_d_w_in(f"d_{tag}_w_in", hb_in, dgu8)
        dh_ffn = _ffn_in_bwd(f"{tag}_in_bwd", dgu8, w_in)
        return dh_res, dh_ffn, d_w_in, d_w_out, dg, db

    dh2_a, dh2_b, grads["ffn2_w_in"], grads["ffn2_w_out"], small["ln3_g"], small["ln3_b"] = ffn_bwd(
        "ffn2", [dh3_a, dh3_b], r3, P["ln3_g"], P["ln3_b"], a2, gu2, h2b, W["ffn2_w_in"], W["ffn2_w_out"])

    def ln2_bwd_fn(d1, d2, r, g, b):
        dr, dg, db = _ln_bwd(r, g, b, d1 + d2)
        return (ALPHA * dr, dr), (dg, db)

    (dh1_a, dmixb), (small["ln2_g"], small["ln2_b"]) = _rowwise(
        "mix_ln_bwd", ln2_bwd_fn, [(dh2_a, 0, D), (dh2_b, 0, D), (r2, 0, D)], [P["ln2_g"], P["ln2_b"]],
        [(D, F32), (D, BF16)], [(1, D), (1, D)])
    grads["w_mix_out"] = _mm_tn("d_w_mix_out", merged, dmixb, BF16)
    dmerged = _mm_nt("mix_out_bwd", dmixb, W["w_mix_out"], F32)

    def merge_bwd_fn(dm, gs, gg, ys, yg, bs, bg):
        ss, sg = _sigmoid(gs + bs), _sigmoid(gg + bg)
        dgs = dm * ys * ss * (1.0 - ss)
        dgg = dm * yg * sg * (1.0 - sg)
        return (dgs, dgg, dm * ss, dm * sg), (jnp.sum(dgs, axis=0, keepdims=True), jnp.sum(dgg, axis=0, keepdims=True))

    (dgs, dgg, dy_sb, dy_gdn), (d_bs, d_bg) = _rowwise(
        "mix_merge_bwd", merge_bwd_fn, [(dmerged, 0, D), (gates, 0, D), (gates, 1, D), (y_sb, 0, D), (y_gdn, 0, D)],
        [bs, bg], [(D, BF16)] * 4, [(1, D), (1, D)])
    small["b_gate"] = jnp.concatenate([d_bs, d_bg], axis=1)

    grads["w_branch_sb"] = _mm_tn("d_w_branch_sb", attb, dy_sb, BF16)
    datt = _mm_nt("sb_out_bwd", dy_sb, W["w_branch_sb"], BF16)
    dsq, dsk, dsv = _sb_bwd(sbp, sb_tot_a, sb_tot_b, datt)

    grads["w_branch_gdn"] = _mm_tn("d_w_branch_gdn", gob, dy_gdn, BF16)
    dgo = _mm_nt("gdn_out_bwd", dy_gdn, W["w_branch_gdn"], F32)
    dqkv, dgz, dba, d_alog_x, d_dt_x, d_norm_x = _gdn_bwd(qkv, gdnp, ba, a_log_x, dt_x, norm_x, states, dgo)
    small["a_log"] = jnp.sum(d_alog_x.reshape(N_HEADS, HEAD_DIM), axis=1).reshape(1, N_HEADS)
    small["dt_bias"] = jnp.sum(d_dt_x.reshape(N_HEADS, HEAD_DIM), axis=1).reshape(1, N_HEADS)
    small["gdn_norm_w"] = jnp.sum(d_norm_x.reshape(N_HEADS, HEAD_DIM), axis=0).reshape(1, HEAD_DIM)
    dconv_in, d_conv_w = _conv_bwd("conv_bwd", gdnp, conv_w, dqkv)
    grads["conv_w"] = d_conv_w[:CONV_K]

    dproj = jnp.concatenate([dsq.astype(BF16), dsk.astype(BF16), dsv.astype(BF16), dconv_in, dgz, dgs, dgg,
                             dba.astype(BF16)], axis=1)
    d_wmix = _mm_tn("d_w_mix_in", h1b, dproj, BF16)
    grads["w_mix_in"] = jnp.concatenate([d_wmix[:, :MIX_GATES], d_wmix[:, MIX_BA:MIX_BA + 2 * N_HEADS],
                                         d_wmix[:, MIX_GATES:MIX_BA]], axis=1)
    dh1_b = _mm_nt("mix_in_bwd", dproj, wmix, F32)

    dx_a, dx_b, grads["ffn1_w_in"], grads["ffn1_w_out"], small["ln1_g"], small["ln1_b"] = ffn_bwd(
        "ffn1", [dh1_a, dh1_b], r1, P["ln1_g"], P["ln1_b"], a1, gu1, xb, W["ffn1_w_in"], W["ffn1_w_out"])

    (grad_x,), _ = _rowwise("grad_x", lambda a, b: ((a + b,), ()), [(dx_a, 0, D), (dx_b, 0, D)], [], [(D, F32)])
    return loss[0, 0], grad_x, grads, small


def _coords():
    return lax.axis_index("x"), lax.axis_index("y"), lax.axis_index("c")


def _all_gather(name, shards):
    n = len(shards)

    def body(*refs):
        x_refs, out_refs = refs[:n], refs[n:2 * n]
        send_sems, recv_sems, local_sems = refs[2 * n:]
        x, y, c = _coords()
        me, sibling = (x, y, c), (x, y, 1 - c)
        chips = [(1 - x, y), (x, 1 - y), (1 - x, 1 - y)]

        def copy(a, k, block, to, src=None):
            px, py, pc = block
            dst = out_refs[a].at[4 * px + 2 * py + pc]
            return pltpu.make_async_remote_copy(
                src_ref=dst if src is None else src, dst_ref=dst,
                send_sem=send_sems.at[a, k], recv_sem=recv_sems.at[a, k], device_id=to, device_id_type=MESH)

        mine = [pltpu.make_async_copy(x_refs[a], out_refs[a].at[4 * x + 2 * y + c], local_sems.at[a]) for a in range(n)]
        for cp in mine:
            cp.start()
        first = []
        for a in range(n):
            first.append(copy(a, 0, me, sibling, src=x_refs[a]))
            first += [copy(a, 1 + j, me, (*chip, c), src=x_refs[a]) for j, chip in enumerate(chips)]
        for cp in first:
            cp.start()
        passed = []
        for j, chip in enumerate(chips):
            for a in range(n):
                copy(a, 1 + j, (*chip, c), me).wait_recv()
                fwd = copy(a, 4 + j, (*chip, c), sibling)
                fwd.start()
                passed.append(fwd)
        for a in range(n):
            copy(a, 0, sibling, me).wait_recv()
            for j, chip in enumerate(chips):
                copy(a, 4 + j, (*chip, 1 - c), me).wait_recv()
        for cp in first + passed:
            cp.wait_send()
        for cp in mine:
            cp.wait()

    any_spec = pl.BlockSpec(memory_space=pl.ANY)
    return pl.pallas_call(
        body, name=name, out_shape=[jax.ShapeDtypeStruct((N_DEV, *s.shape), s.dtype) for s in shards],
        in_specs=[any_spec] * n, out_specs=[any_spec] * n,
        scratch_shapes=[pltpu.SemaphoreType.DMA((n, 7)), pltpu.SemaphoreType.DMA((n, 7)), pltpu.SemaphoreType.DMA((n,))],
    )(*shards)


def _all_to_all(name, srcs):
    n = len(srcs)

    def body(*refs):
        src_refs, dst_refs = refs[:n], refs[n:2 * n]
        send_sems, recv_sems, local_sems = refs[2 * n:]
        x, y, c = _coords()
        me = 4 * x + 2 * y + c
        mine = [pltpu.make_async_copy(src_refs[a].at[me], dst_refs[a].at[me], local_sems.at[a]) for a in range(n)]
        for cp in mine:
            cp.start()
        sends, recvs = [], []
        for k in range(1, N_DEV):
            px = 1 - x if k & 4 else x
            py = 1 - y if k & 2 else y
            pc = 1 - c if k & 1 else c
            peer = 4 * px + 2 * py + pc
            for a in range(n):
                sends.append(pltpu.make_async_remote_copy(
                    src_ref=src_refs[a].at[peer], dst_ref=dst_refs[a].at[me], send_sem=send_sems.at[a, k - 1],
                    recv_sem=recv_sems.at[a, k - 1], device_id=(px, py, pc), device_id_type=MESH))
                recvs.append(pltpu.make_async_remote_copy(
                    src_ref=src_refs[a].at[me], dst_ref=dst_refs[a].at[peer], send_sem=send_sems.at[a, k - 1],
                    recv_sem=recv_sems.at[a, k - 1], device_id=(px, py, pc), device_id_type=MESH))
        for cp in sends:
            cp.start()
        for cp in recvs:
            cp.wait_recv()
        for cp in sends:
            cp.wait_send()
        for cp in mine:
            cp.wait()

    any_spec = pl.BlockSpec(memory_space=pl.ANY)
    return pl.pallas_call(
        body, name=name, out_shape=[jax.ShapeDtypeStruct(s.shape, s.dtype) for s in srcs],
        in_specs=[any_spec] * n, out_specs=[any_spec] * n,
        scratch_shapes=[pltpu.SemaphoreType.DMA((n, 7)), pltpu.SemaphoreType.DMA((n, 7)), pltpu.SemaphoreType.DMA((n,))],
    )(*srcs)


def _adamw(name, parts, w, m, v):
    R, C = w.shape
    tr = _pick(R, (256, 176, 128, R))
    c1 = 1.0 - ADAM_B1 ** ADAM_STEP
    c2 = 1.0 - ADAM_B2 ** ADAM_STEP

    def body(p_ref, w_ref, m_ref, v_ref, g_ref, d_ref, nm_ref, nv_ref):
        g = p_ref[0].astype(F32)
        for d in range(1, N_DEV):
            g = g + p_ref[d].astype(F32)
        nm = ADAM_B1 * m_ref[...] + (1.0 - ADAM_B1) * g
        nv = ADAM_B2 * v_ref[...] + (1.0 - ADAM_B2) * (g * g)
        g_ref[...] = g
        nm_ref[...] = nm
        nv_ref[...] = nv
        d_ref[...] = -ADAM_LR * ((nm / c1) / (jnp.sqrt(nv / c2) + ADAM_EPS) + ADAM_WD * w_ref[...])

    t = pl.BlockSpec((tr, C), lambda i: (i, 0))
    o = jax.ShapeDtypeStruct((R, C), F32)
    return pl.pallas_call(
        body, name=name, grid=(R // tr,),
        in_specs=[pl.BlockSpec((N_DEV, tr, C), lambda i: (0, i, 0)), t, t, t],
        out_specs=[t, t, t, t], out_shape=[o, o, o, o],
        compiler_params=pltpu.CompilerParams(dimension_semantics=("parallel",)),
    )(parts, w, m, v)


def _pack_rows(flats, rows):
    cat = jnp.concatenate(flats, axis=-1)
    return jnp.pad(cat, [(0, rows * PACK_COLS - cat.shape[-1])]).reshape(rows, PACK_COLS)


def _col_shards(full):
    r, cdim = full.shape
    return full.reshape(r, N_DEV, cdim // N_DEV).transpose(1, 0, 2)


def _from_col_shards(sh):
    _, r, c = sh.shape
    return sh.transpose(1, 0, 2).reshape(r, N_DEV * c)


def kernel(x, p, ffn1_w_in, ffn1_w_out, ln1_g, ln1_b, w_mix_in, b_gate, conv_w, a_log, dt_bias, gdn_norm_w, w_branch_sb, w_branch_gdn, w_mix_out, ln2_g, ln2_b, ffn2_w_in, ffn2_w_out, ln3_g, ln3_b, w_ple_gate, b_ple_gate, w_ple, ln4_g, ln4_b, loss_target, m_ffn1_w_in, m_ffn1_w_out, m_ln1_g, m_ln1_b, m_w_mix_in, m_b_gate, m_conv_w, m_a_log, m_dt_bias, m_gdn_norm_w, m_w_branch_sb, m_w_branch_gdn, m_w_mix_out, m_ln2_g, m_ln2_b, m_ffn2_w_in, m_ffn2_w_out, m_ln3_g, m_ln3_b, m_w_ple_gate, m_b_ple_gate, m_w_ple, m_ln4_g, m_ln4_b, v_ffn1_w_in, v_ffn1_w_out, v_ln1_g, v_ln1_b, v_w_mix_in, v_b_gate, v_conv_w, v_a_log, v_dt_bias, v_gdn_norm_w, v_w_branch_sb, v_w_branch_gdn, v_w_mix_out, v_ln2_g, v_ln2_b, v_ffn2_w_in, v_ffn2_w_out, v_ln3_g, v_ln3_b, v_w_ple_gate, v_b_ple_gate, v_w_ple, v_ln4_g, v_ln4_b):
    given = dict(locals())
    w_loc = {n: given[n][0] for n in WEIGHTS}
    m_loc = {n: given["m_" + n][0] for n in WEIGHTS}
    v_loc = {n: given["v_" + n][0] for n in WEIGHTS}
    sizes = {n: w_loc[n].size for n in WEIGHTS}

    conv_hi = conv_w[0].astype(BF16)
    conv_lo = (conv_w[0] - conv_hi.astype(F32)).astype(BF16)
    gathered = _all_gather("gather_weights", [w_loc[n].astype(BF16) for n in BIG] + [conv_lo])
    W = dict(zip(BIG, gathered[:-1]))
    for n in ("ffn1_w_out", "ffn2_w_out"):
        W[n] = W[n].reshape(N_DEV // 2, SHARD_FF, D_MODEL)
    for n in ("w_mix_out", "w_ple_gate"):
        W[n] = W[n].reshape(D_MODEL, D_MODEL)
    for n in ("w_branch_sb", "w_branch_gdn", "w_ple"):
        W[n] = _from_col_shards(W[n])
    wm = _from_col_shards(W.pop("w_mix_in"))
    W["w_mix_pad"] = jnp.concatenate([wm[:, :7 * HEAD_W], wm[:, 7 * HEAD_W + 2 * N_HEADS:],
                                      wm[:, 7 * HEAD_W:7 * HEAD_W + 2 * N_HEADS],
                                      jnp.zeros((D_MODEL, 128 - 2 * N_HEADS), BF16)], axis=1)
    P = {n: w_loc[n].reshape(1, -1) for n in SMALL}
    P["conv_w"] = _from_col_shards(W.pop("conv_w").astype(F32) + gathered[-1].astype(F32))

    loss, grad_x, grads, small = _local_step(x[0], p[0, 0], loss_target[0], W, P)
    loss = lax.psum(loss, ("x", "y", "c"))

    for n in ("w_mix_in", "w_branch_sb", "w_branch_gdn", "w_ple"):
        grads[n] = _col_shards(grads[n])
    grads["conv_w"] = _col_shards(grads["conv_w"]).astype(BF16)
    send = [grads[n].reshape(N_DEV, *w_loc[n].shape) for n in BIG]
    parts = dict(zip(BIG, _all_to_all("scatter_grads", send)))
    small_rows = 16
    (small_parts,) = _all_gather("gather_small_grads", [_pack_rows([small[n].reshape(-1) for n in SMALL], small_rows)])

    res = {n: _adamw(f"adamw_{n}", parts[n], w_loc[n], m_loc[n], v_loc[n]) for n in BIG}
    pack = lambda d: _pack_rows([d[n].reshape(-1) for n in SMALL], small_rows)
    small_out = _adamw("adamw_small", small_parts, pack(w_loc), pack(m_loc), pack(v_loc))
    off = 0
    for n in SMALL:
        res[n] = [o.reshape(-1)[off:off + sizes[n]] for o in small_out]
        off += sizes[n]
    outs = [[res[n][i].reshape(given[n].shape) for n in WEIGHTS] for i in range(4)]
    g_out, d_out, nm_out, nv_out = outs
    return (loss, grad_x[None], *g_out, *d_out, *nm_out, *nv_out)
```

```python
import functools
import math

import jax
import jax.numpy as jnp
from jax import lax
from jax.experimental import pallas as pl
from jax.experimental.pallas import tpu as pltpu

F32 = jnp.float32
BF16 = jnp.bfloat16
MESH = pl.DeviceIdType.MESH
HI = lax.Precision.HIGHEST

N_DEV = 8
D_MODEL = 1024
D_FF = 2816
PLE_DIM = 256
N_HEADS = 8
HEAD_DIM = 64
HEAD_W = N_HEADS * HEAD_DIM
GDN_CHUNK = 64
CONV_K = 4
CONV_CH = 3 * HEAD_W
ALPHA = 2.0 ** 0.25
LN_EPS = 1e-5
RMS_EPS = 1e-6
MIX_SB = 0
MIX_GDN = 3 * HEAD_W
MIX_GATES = MIX_GDN + 4 * HEAD_W
MIX_BA = MIX_GATES + 2 * D_MODEL
MIX_PAD = MIX_BA + 128
N_IN = 7 * HEAD_W + 2 * N_HEADS + 2 * D_MODEL

ADAM_LR = 0.001
ADAM_B1 = 0.9
ADAM_B2 = 0.999
ADAM_EPS = 1e-08
ADAM_WD = 0.01
ADAM_STEP = 10

SB_TQ = 512
SB_TK = 128
SB_CUTOFF = -110.0
HALO = 8

BIG = ["ffn1_w_in", "ffn1_w_out", "w_mix_in", "conv_w", "w_branch_sb", "w_branch_gdn", "w_mix_out",
       "ffn2_w_in", "ffn2_w_out", "w_ple_gate", "w_ple"]
BIG_SHAPES = {
    "ffn1_w_in": ((D_MODEL, 2 * D_FF), 1), "ffn1_w_out": ((D_FF, D_MODEL), 0),
    "w_mix_in": ((D_MODEL, N_IN), 1), "conv_w": ((CONV_K, CONV_CH), 1),
    "w_branch_sb": ((HEAD_W, D_MODEL), 1), "w_branch_gdn": ((HEAD_W, D_MODEL), 1),
    "w_mix_out": ((D_MODEL, D_MODEL), 0),
    "ffn2_w_in": ((D_MODEL, 2 * D_FF), 1), "ffn2_w_out": ((D_FF, D_MODEL), 0),
    "w_ple_gate": ((D_MODEL, D_MODEL), 0), "w_ple": ((PLE_DIM, D_MODEL), 1),
}
SMALL = ["ln1_g", "ln1_b", "b_gate", "a_log", "dt_bias", "gdn_norm_w", "ln2_g", "ln2_b", "ln3_g", "ln3_b",
         "b_ple_gate", "ln4_g", "ln4_b"]
WEIGHTS = ["ffn1_w_in", "ffn1_w_out", "ln1_g", "ln1_b", "w_mix_in", "b_gate", "conv_w", "a_log", "dt_bias",
           "gdn_norm_w", "w_branch_sb", "w_branch_gdn", "w_mix_out", "ln2_g", "ln2_b", "ffn2_w_in", "ffn2_w_out",
           "ln3_g", "ln3_b", "w_ple_gate", "b_ple_gate", "w_ple", "ln4_g", "ln4_b"]
PACK_COLS = 1024
ADAM_ROWS = 208


def _pick(n, cands):
    for c in cands:
        if n % c == 0:
            return c
    raise ValueError(f"no tile for {n} in {cands}")


_NN = (((1,), (0,)), ((), ()))
_NT = (((1,), (1,)), ((), ()))
_TN = (((0,), (0,)), ((), ()))
SHARD_FF = 2 * D_FF // N_DEV
ROW_TILES = (1024, 512, 256, 128)


def _matmul(name, dims, a, b, a_spec, b_spec, out_sds, out_spec, out_block, grid, nk):
    def body(a_ref, b_ref, o_ref, acc_ref):
        k = pl.program_id(2)

        @pl.when(k == 0)
        def _():
            acc_ref[...] = jnp.zeros_like(acc_ref)

        acc_ref[...] += lax.dot_general(a_ref[...], b_ref[...], dims, preferred_element_type=F32)

        @pl.when(k == nk - 1)
        def _():
            o_ref[...] = acc_ref[...].astype(o_ref.dtype)

    return pl.pallas_call(
        body, name=name, grid=grid, in_specs=[a_spec, b_spec], out_specs=out_spec, out_shape=out_sds,
        scratch_shapes=[pltpu.VMEM(out_block, F32)],
        compiler_params=pltpu.CompilerParams(dimension_semantics=("parallel", "parallel", "arbitrary")),
    )(a, b)


def _mm(name, a, w, out_dtype, n=None, col_off=0):
    M, K = a.shape
    n = w.shape[1] if n is None else n
    tm = _pick(M, ROW_TILES)
    tn = _pick(n, (512, 384, 256, 128))
    tk = _pick(K, (1024, 512, 256))
    assert col_off % tn == 0
    cb, nk = col_off // tn, K // tk
    return _matmul(
        name, _NN, a, w, pl.BlockSpec((tm, tk), lambda i, j, k: (i, k)), pl.BlockSpec((tk, tn), lambda i, j, k: (k, j + cb)),
        jax.ShapeDtypeStruct((M, n), out_dtype), pl.BlockSpec((tm, tn), lambda i, j, k: (i, j)), (tm, tn),
        (M // tm, n // tn, nk), nk)


def _mm_nt(name, a, w, out_dtype):
    M, K = a.shape
    N = w.shape[0]
    tm = _pick(M, ROW_TILES)
    tn = _pick(N, (1024, 512, 256, 128))
    tk = _pick(K, (1024, 1152, 512, 256))
    nk = K // tk
    return _matmul(
        name, _NT, a, w, pl.BlockSpec((tm, tk), lambda i, j, k: (i, k)), pl.BlockSpec((tn, tk), lambda i, j, k: (j, k)),
        jax.ShapeDtypeStruct((M, N), out_dtype), pl.BlockSpec((tm, tn), lambda i, j, k: (i, j)), (tm, tn),
        (M // tm, N // tn, nk), nk)


def _mm_tn(name, a, b, out_dtype):
    S, M = a.shape
    N = b.shape[1]
    tm = _pick(M, (1024, 512, 256, 128))
    tn = _pick(N, (1024, 640, 512, 256, 128))
    tk = _pick(S, ROW_TILES)
    nk = S // tk
    return _matmul(
        name, _TN, a, b, pl.BlockSpec((tk, tm), lambda i, j, k: (k, i)), pl.BlockSpec((tk, tn), lambda i, j, k: (k, j)),
        jax.ShapeDtypeStruct((M, N), out_dtype), pl.BlockSpec((tm, tn), lambda i, j, k: (i, j)), (tm, tn),
        (M // tm, N // tn, nk), nk)


def _sigmoid(z):
    return 1.0 / (1.0 + jnp.exp(-z))


def _ffn_up(name, xb, w_in):
    S, D = xb.shape
    c = w_in.shape[2]
    ts = _pick(S, ROW_TILES)
    half = N_DEV // 2

    def body(x_ref, wg_ref, wu_ref, a_ref, gu_ref):
        x = x_ref[...]
        g = jnp.dot(x, wg_ref[...], preferred_element_type=F32)
        u = jnp.dot(x, wu_ref[...], preferred_element_type=F32)
        a_ref[...] = (g * _sigmoid(g) * u).astype(a_ref.dtype)
        gu_ref[0] = g.astype(gu_ref.dtype)
        gu_ref[1] = u.astype(gu_ref.dtype)

    return pl.pallas_call(
        body, name=name, grid=(S // ts, half),
        in_specs=[pl.BlockSpec((ts, D), lambda i, j: (i, 0)), pl.BlockSpec((None, D, c), lambda i, j: (j, 0, 0)),
                  pl.BlockSpec((None, D, c), lambda i, j: (j + half, 0, 0))],
        out_specs=[pl.BlockSpec((None, ts, c), lambda i, j: (j, i, 0)),
                   pl.BlockSpec((2, None, ts, c), lambda i, j: (0, j, i, 0))],
        out_shape=[jax.ShapeDtypeStruct((half, S, c), BF16), jax.ShapeDtypeStruct((2, half, S, c), BF16)],
        compiler_params=pltpu.CompilerParams(dimension_semantics=("parallel", "parallel")),
    )(xb, w_in, w_in)


def _ffn_down(name, a4, w_out4):
    n, S, c = a4.shape
    D = w_out4.shape[2]
    tm = _pick(S, ROW_TILES)
    tn = _pick(D, (1024, 512, 256, 128))
    return _matmul(
        name, _NN, a4, w_out4, pl.BlockSpec((None, tm, c), lambda i, j, k: (k, i, 0)),
        pl.BlockSpec((None, c, tn), lambda i, j, k: (k, 0, j)),
        jax.ShapeDtypeStruct((S, D), F32), pl.BlockSpec((tm, tn), lambda i, j, k: (i, j)), (tm, tn),
        (S // tm, D // tn, n), n)


def _ffn_bwd_act(name, dyb, w_out4, gu):
    S, D = dyb.shape
    n, c, _ = w_out4.shape
    ts = _pick(S, ROW_TILES)

    def body(dy_ref, w_ref, gu_ref, o_ref):
        da = lax.dot_general(dy_ref[...], w_ref[...], _NT, preferred_element_type=F32)
        g = gu_ref[0].astype(F32)
        u = gu_ref[1].astype(F32)
        sg = _sigmoid(g)
        o_ref[0] = (da * u * (sg * (1.0 + g * (1.0 - sg)))).astype(o_ref.dtype)
        o_ref[1] = (da * (g * sg)).astype(o_ref.dtype)

    blk = pl.BlockSpec((2, None, ts, c), lambda i, j: (0, j, i, 0))
    return pl.pallas_call(
        body, name=name, grid=(S // ts, n),
        in_specs=[pl.BlockSpec((ts, D), lambda i, j: (i, 0)), pl.BlockSpec((None, c, D), lambda i, j: (j, 0, 0)), blk],
        out_specs=blk, out_shape=jax.ShapeDtypeStruct((2, n, S, c), BF16),
        compiler_params=pltpu.CompilerParams(dimension_semantics=("parallel", "parallel")),
    )(dyb, w_out4, gu)


def _ffn_d_w_out(name, a4, dyb):
    n, S, c = a4.shape
    D = dyb.shape[1]
    tn = _pick(D, (1024, 512, 256, 128))
    tk = _pick(S, ROW_TILES)
    nk = S // tk
    return _matmul(
        name, _TN, a4, dyb, pl.BlockSpec((None, tk, c), lambda i, j, k: (i, k, 0)), pl.BlockSpec((tk, tn), lambda i, j, k: (k, j)),
        jax.ShapeDtypeStruct((n, c, D), BF16), pl.BlockSpec((None, c, tn), lambda i, j, k: (i, 0, j)), (c, tn),
        (n, D // tn, nk), nk)


def _ffn_d_w_in(name, hb, dgu8):
    S, D = hb.shape
    n, _, c = dgu8.shape
    tm = _pick(D, (1024, 512, 256, 128))
    tk = _pick(S, ROW_TILES)
    nk = S // tk
    return _matmul(
        name, _TN, hb, dgu8, pl.BlockSpec((tk, tm), lambda i, j, k: (k, j)), pl.BlockSpec((None, tk, c), lambda i, j, k: (i, k, 0)),
        jax.ShapeDtypeStruct((n, D, c), BF16), pl.BlockSpec((None, tm, c), lambda i, j, k: (i, j, 0)), (tm, c),
        (n, D // tm, nk), nk)


def _ffn_in_bwd(name, dgu8, w_in):
    n, S, c = dgu8.shape
    D = w_in.shape[1]
    tm = _pick(S, ROW_TILES)
    tn = _pick(D, (1024, 512, 256, 128))
    return _matmul(
        name, _NT, dgu8, w_in, pl.BlockSpec((None, tm, c), lambda i, j, k: (k, i, 0)),
        pl.BlockSpec((None, tn, c), lambda i, j, k: (k, j, 0)),
        jax.ShapeDtypeStruct((S, D), F32), pl.BlockSpec((tm, tn), lambda i, j, k: (i, j)), (tm, tn),
        (S // tm, D // tn, n), n)


def _rowwise(name, fn, tiled, params, outs, reds=(), ts=256):
    S = tiled[0][0].shape[0]
    ts = _pick(S, (ts, 128))
    n_t, n_p, n_o = len(tiled), len(params), len(outs)

    def body(*refs):
        vals = [r[...] for r in refs[:n_t + n_p]]
        res_o, res_r = fn(*vals)
        o_refs = refs[n_t + n_p:n_t + n_p + n_o]
        r_refs = refs[n_t + n_p + n_o:]
        for r, val in zip(o_refs, res_o, strict=True):
            r[...] = val.astype(r.dtype)
        if r_refs:
            i = pl.program_id(0)

            @pl.when(i == 0)
            def _():
                for r, val in zip(r_refs, res_r, strict=True):
                    r[...] = val.astype(F32)

            @pl.when(i > 0)
            def _():
                for r, val in zip(r_refs, res_r, strict=True):
                    r[...] += val.astype(F32)

    in_specs = [pl.BlockSpec((ts, w), functools.partial(lambda i, cb: (i, cb), cb=cb)) for _, cb, w in tiled]
    in_specs += [pl.BlockSpec(p.shape, lambda i: (0, 0)) for p in params]
    out_specs = [pl.BlockSpec((ts, w), lambda i: (i, 0)) for w, _ in outs]
    out_specs += [pl.BlockSpec(tuple(r), lambda i: (0, 0)) for r in reds]
    out_shape = [jax.ShapeDtypeStruct((S, w), dt) for w, dt in outs]
    out_shape += [jax.ShapeDtypeStruct(tuple(r), F32) for r in reds]
    res = pl.pallas_call(
        body, name=name, grid=(S // ts,), in_specs=in_specs, out_specs=out_specs, out_shape=out_shape,
        compiler_params=pltpu.CompilerParams(dimension_semantics=("arbitrary",)),
    )(*[t[0] for t in tiled], *params)
    return res[:n_o], res[n_o:]


def _ln(r, g, b):
    mu = jnp.mean(r, axis=-1, keepdims=True)
    xc = r - mu
    var = jnp.mean(xc * xc, axis=-1, keepdims=True)
    return xc * lax.rsqrt(var + LN_EPS) * g + b


def _ln_bwd(r, g, b, dy):
    _, vjp = jax.vjp(_ln, r, g, b)
    return vjp(dy)


def _shift_down(x, d):
    return x if d == 0 else pltpu.roll(x, d, 0)


def _conv_fwd(name, x, w):
    S = x.shape[0]
    ts = _pick(S, (256, 128))
    hb = ts // HALO

    def body(x_ref, prev_ref, w_ref, o_ref):
        i = pl.program_id(0)
        prev = jnp.where(i > 0, prev_ref[...], 0.0)
        xe = jnp.concatenate([prev, x_ref[...]], axis=0)
        y = jnp.zeros((ts + HALO, CONV_CH), F32)
        for j in range(CONV_K):
            y = y + w_ref[pl.ds(j, 1), :] * _shift_down(xe, CONV_K - 1 - j)
        y = y[HALO:, :]
        o_ref[...] = y * _sigmoid(y)

    return pl.pallas_call(
        body, name=name, grid=(S // ts,),
        in_specs=[pl.BlockSpec((ts, CONV_CH), lambda i: (i, 0)),
                  pl.BlockSpec((HALO, CONV_CH), lambda i: (jnp.maximum(i * hb - 1, 0), 0)),
                  pl.BlockSpec((CONV_K, CONV_CH), lambda i: (0, 0))],
        out_specs=pl.BlockSpec((ts, CONV_CH), lambda i: (i, 0)),
        out_shape=jax.ShapeDtypeStruct((S, CONV_CH), F32),
        compiler_params=pltpu.CompilerParams(dimension_semantics=("arbitrary",)),
    )(x, x, w)


def _conv_bwd(name, x, w, dout):
    S = x.shape[0]
    ts = _pick(S, (256, 128))
    hb = ts // HALO
    nt = S // ts
    n_ext = ts + 2 * HALO

    def body(x_ref, prev_ref, next_ref, w_ref, d_ref, dnext_ref, dx_ref, dw_ref):
        i = pl.program_id(0)
        prev = jnp.where(i > 0, prev_ref[...], 0.0)
        last = i == nt - 1
        nxt = jnp.where(last, 0.0, next_ref[...])
        dnxt = jnp.where(last, 0.0, dnext_ref[...])
        xe = jnp.concatenate([prev, x_ref[...], nxt], axis=0)
        de = jnp.concatenate([jnp.zeros((HALO, CONV_CH), F32), d_ref[...], dnxt], axis=0)
        y = jnp.zeros((n_ext, CONV_CH), F32)
        for j in range(CONV_K):
            y = y + w_ref[pl.ds(j, 1), :] * _shift_down(xe, CONV_K - 1 - j)
        sg = _sigmoid(y)
        dy = de * (sg * (1.0 + y * (1.0 - sg)))
        dx = jnp.zeros((n_ext, CONV_CH), F32)
        for j in range(CONV_K):
            m = CONV_K - 1 - j
            dx = dx + w_ref[pl.ds(j, 1), :] * (dy if m == 0 else pltpu.roll(dy, n_ext - m, 0))
        dx_ref[...] = dx[HALO:HALO + ts, :].astype(dx_ref.dtype)
        row = lax.broadcasted_iota(jnp.int32, (n_ext, 1), 0)
        dy_own = jnp.where((row >= HALO) & (row < HALO + ts), dy, 0.0)
        parts = [jnp.sum(dy_own * _shift_down(xe, CONV_K - 1 - j), axis=0, keepdims=True) for j in range(CONV_K)]
        dw = jnp.concatenate(parts + [jnp.zeros((HALO - CONV_K, CONV_CH), F32)], axis=0)

        @pl.when(i == 0)
        def _():
            dw_ref[...] = dw

        @pl.when(i > 0)
        def _():
            dw_ref[...] += dw

    tile = pl.BlockSpec((ts, CONV_CH), lambda i: (i, 0))
    prev = pl.BlockSpec((HALO, CONV_CH), lambda i: (jnp.maximum(i * hb - 1, 0), 0))
    nxt = pl.BlockSpec((HALO, CONV_CH), lambda i: (jnp.minimum((i + 1) * hb, nt * hb - 1), 0))
    return pl.pallas_call(
        body, name=name, grid=(nt,),
        in_specs=[tile, prev, nxt, pl.BlockSpec((CONV_K, CONV_CH), lambda i: (0, 0)), tile, nxt],
        out_specs=[tile, pl.BlockSpec((HALO, CONV_CH), lambda i: (0, 0))],
        out_shape=[jax.ShapeDtypeStruct((S, CONV_CH), BF16), jax.ShapeDtypeStruct((HALO, CONV_CH), F32)],
        compiler_params=pltpu.CompilerParams(dimension_semantics=("arbitrary",)),
    )(x, x, x, w, dout, dout)


def _softplus(z):
    return jnp.maximum(z, 0.0) + jnp.log(1.0 + jnp.exp(-jnp.abs(z)))


def _dot2(x, tri2):
    hi = x.astype(BF16)
    lo = (x - hi.astype(F32)).astype(BF16)
    return jnp.dot(jnp.concatenate([hi, lo], axis=1), tri2, preferred_element_type=F32)


def _pair_tri(tk, keep):
    r = lax.broadcasted_iota(jnp.int32, (4 * tk, 2 * tk), 0)
    c = lax.broadcasted_iota(jnp.int32, (4 * tk, 2 * tk), 1)
    same_head = ((r // tk) % 2) == (c // tk)
    return (same_head & keep(r % tk, c % tk)).astype(BF16)


def _pair_rows(x2, first):
    return jnp.concatenate([jnp.where(first, x2, 0), jnp.where(first, 0, x2)], axis=0)


def _pair_sum(x):
    h = x.shape[1] // 2
    return jnp.sum(x[:, :h], axis=1, keepdims=True), jnp.sum(x[:, h:], axis=1, keepdims=True)


def _sb_tiles(S):
    tq = _pick(S, (SB_TQ, 256, 128))
    return tq, SB_TK


N_PAIRS = N_HEADS // 2


def _sb_specs(S, tq):
    q = pl.BlockSpec((tq, 128), lambda p, i: (i, p))
    k = pl.BlockSpec((S, 128), lambda p, i: (0, N_PAIRS + p))
    v = pl.BlockSpec((S, 128), lambda p, i: (0, 2 * N_PAIRS + p))
    return q, k, v


def _scaled(q):
    return (q.astype(F32) * (HEAD_DIM ** -0.5)).astype(BF16)


def _sb_fwd(qkv):
    S = qkv.shape[0]
    tq, tk = _sb_tiles(S)
    nd = tq // tk

    def body(q_ref, k_ref, v_ref, o_ref, ta_ref, tb_ref, cnt_ref):
        qi = pl.program_id(1)
        qs = _scaled(q_ref[...])
        row = qi * tq + lax.broadcasted_iota(jnp.int32, (tq, 2 * tk), 0)
        col = lax.broadcasted_iota(jnp.int32, (tq, 2 * tk), 1)
        left = col < tk
        col = col % tk
        later = _pair_tri(tk, lambda r, c: r > c)
        first = lax.broadcasted_iota(jnp.int32, (tk, 128), 1) < HEAD_DIM

        def tile(j, carry, masked):
            acc, suf_a, suf_b = carry
            off = pl.multiple_of(j * tk, tk)
            kc = _pair_rows(k_ref[pl.ds(off, tk), :], first)
            vc = _pair_rows(v_ref[pl.ds(off, tk), :], first)
            mask = (col + j * tk) < row
            z = lax.dot_general(qs, kc, _NT, preferred_element_type=F32)
            lf = -_softplus(z)
            if masked:
                lf = jnp.where(mask, lf, 0.0)
            w = jnp.exp(z + lf + _dot2(lf, later) + jnp.where(left, suf_a, suf_b))
            if masked:
                w = jnp.where(mask, w, 0.0)
            acc = acc + jnp.dot(w.astype(BF16), vc, preferred_element_type=F32)
            sum_a, sum_b = _pair_sum(lf)
            return acc, suf_a + sum_a, suf_b + sum_b

        zero1 = jnp.zeros((tq, 1), F32)
        carry = (jnp.zeros((tq, 128), F32), zero1, zero1)
        n_full = qi * nd
        acc, suf_a, suf_b = lax.fori_loop(0, nd, lambda d, c: tile(n_full + nd - 1 - d, c, True), carry)

        def alive(sa, sb):
            return jnp.maximum(jnp.max(sa), jnp.max(sb)) > SB_CUTOFF

        def step(c):
            jj, acc, sa, sb, _ = c
            acc, sa, sb = tile(n_full - 1 - jj, (acc, sa, sb), False)
            return jj + 1, acc, sa, sb, alive(sa, sb)

        n_used, acc, suf_a, suf_b, _ = lax.while_loop(
            lambda c: (c[0] < n_full) & c[4], step, (0, acc, suf_a, suf_b, alive(suf_a, suf_b)))
        o_ref[...] = acc.astype(o_ref.dtype)
        ta_ref[...] = suf_a
        tb_ref[...] = suf_b
        cnt_ref[pl.program_id(0), qi] = n_used

    tot = pl.BlockSpec((None, tq, 1), lambda p, i: (p, i, 0))
    tshape = jax.ShapeDtypeStruct((N_PAIRS, S, 1), F32)
    return pl.pallas_call(
        body, name="sb_fwd", grid=(N_PAIRS, S // tq), in_specs=list(_sb_specs(S, tq)),
        out_specs=[pl.BlockSpec((tq, 128), lambda p, i: (i, p)), tot, tot, pl.BlockSpec(memory_space=pltpu.SMEM)],
        out_shape=[jax.ShapeDtypeStruct((S, HEAD_W), BF16), tshape, tshape,
                   jax.ShapeDtypeStruct((N_PAIRS, S // tq), jnp.int32)],
        compiler_params=pltpu.CompilerParams(dimension_semantics=("arbitrary", "arbitrary")),
    )(qkv, qkv, qkv)


def _sb_bwd(qkv, tot_a, tot_b, visited, do):
    S = qkv.shape[0]
    tq, tk = _sb_tiles(S)
    nd = tq // tk
    scale = HEAD_DIM ** -0.5

    def body(q_ref, k_ref, v_ref, ta_ref, tb_ref, cnt_ref, do_ref, dq_ref, dk_ref, dv_ref):
        qi = pl.program_id(1)

        @pl.when(qi == 0)
        def _():
            dk_ref[...] = jnp.zeros_like(dk_ref)
            dv_ref[...] = jnp.zeros_like(dv_ref)

        qs = _scaled(q_ref[...])
        do2 = do_ref[...]
        row = qi * tq + lax.broadcasted_iota(jnp.int32, (tq, 2 * tk), 0)
        col = lax.broadcasted_iota(jnp.int32, (tq, 2 * tk), 1)
        left = col < tk
        col = col % tk
        total = jnp.where(left, ta_ref[...], tb_ref[...])
        upto = _pair_tri(tk, lambda r, c: r <= c)
        first = lax.broadcasted_iota(jnp.int32, (tk, 128), 1) < HEAD_DIM

        def tile(j, carry, masked):
            dq, pl_a, pl_b, pg_a, pg_b = carry
            off = pl.multiple_of(j * tk, tk)
            kc = _pair_rows(k_ref[pl.ds(off, tk), :], first)
            vc = _pair_rows(v_ref[pl.ds(off, tk), :], first)
            mask = (col + j * tk) < row
            z = lax.dot_general(qs, kc, _NT, preferred_element_type=F32)
            lf = -_softplus(z)
            sig = jnp.exp(z + lf)
            if masked:
                lf = jnp.where(mask, lf, 0.0)
            w = jnp.exp(z + lf + (total - (jnp.where(left, pl_a, pl_b) + _dot2(lf, upto))))
            if masked:
                w = jnp.where(mask, w, 0.0)
            gl = lax.dot_general(do2, vc, _NT, preferred_element_type=F32) * w
            dz = gl - sig * (jnp.where(left, pg_a, pg_b) + _dot2(gl, upto))
            if masked:
                dz = jnp.where(mask, dz, 0.0)
            dzb = dz.astype(BF16)
            dq = dq + jnp.dot(dzb, kc, preferred_element_type=F32)
            dkc = lax.dot_general(dzb, qs, _TN, preferred_element_type=F32)
            dvc = lax.dot_general(w.astype(BF16), do2, _TN, preferred_element_type=F32)
            dk_ref[pl.ds(off, tk), :] += jnp.where(first, dkc[:tk], dkc[tk:])
            dv_ref[pl.ds(off, tk), :] += jnp.where(first, dvc[:tk], dvc[tk:])
            sl_a, sl_b = _pair_sum(lf)
            sg_a, sg_b = _pair_sum(gl)
            return dq, pl_a + sl_a, pl_b + sl_b, pg_a + sg_a, pg_b + sg_b

        zero1 = jnp.zeros((tq, 1), F32)
        carry = (jnp.zeros((tq, 128), F32), zero1, zero1, zero1, zero1)
        n_full = qi * nd
        n_used = cnt_ref[pl.program_id(0), qi]
        first_tile = n_full - n_used
        carry = lax.fori_loop(0, n_used, lambda t, c: tile(first_tile + t, c, False), carry)
        dq = lax.fori_loop(0, nd, lambda d, c: tile(n_full + d, c, True), carry)[0]
        dq_ref[...] = dq * scale

    q_spec, k_spec, v_spec = _sb_specs(S, tq)
    tot = pl.BlockSpec((None, tq, 1), lambda p, i: (p, i, 0))
    whole = pl.BlockSpec((S, 128), lambda p, i: (0, p))
    big = jax.ShapeDtypeStruct((S, HEAD_W), F32)
    return pl.pallas_call(
        body, name="sb_bwd", grid=(N_PAIRS, S // tq),
        in_specs=[q_spec, k_spec, v_spec, tot, tot, pl.BlockSpec(memory_space=pltpu.SMEM), q_spec],
        out_specs=[q_spec, whole, whole], out_shape=[big, big, big],
        compiler_params=pltpu.CompilerParams(dimension_semantics=("arbitrary", "arbitrary")),
    )(qkv, qkv, qkv, tot_a, tot_b, visited, do)


GDN_PRECISION = lax.Precision.HIGH


def _bmm(a, b):
    return lax.dot_general(a, b, (((2,), (1,)), ((0,), (0,))), precision=GDN_PRECISION, preferred_element_type=F32)


def _bmm_nt(a, b):
    return lax.dot_general(a, b, (((2,), (2,)), ((0,), (0,))), precision=GDN_PRECISION, preferred_element_type=F32)


def _bmm_tn(a, b):
    return lax.dot_general(a, b, (((1,), (1,)), ((0,), (0,))), precision=GDN_PRECISION, preferred_element_type=F32)


def _tri_inv(lower):
    C = lower.shape[-1]
    ii = lax.broadcasted_iota(jnp.int32, (C, C), 0)
    jj = lax.broadcasted_iota(jnp.int32, (C, C), 1)
    eye = (ii == jj).astype(F32)[None]
    xd = jnp.where((ii // 8 == jj // 8)[None], -lower, 0.0)
    x2 = _bmm(xd, xd)
    x4 = _bmm(x2, x2)
    inv = eye + xd
    inv = inv + _bmm(inv, x2)
    inv = inv + _bmm(inv, x4)
    b = 8
    while b < C:
        off = jnp.where(((ii // (2 * b) == jj // (2 * b)) & (ii // b != jj // b))[None], lower, 0.0)
        inv = inv - _bmm(inv, _bmm(off, inv))
        b *= 2
    return inv


@jax.custom_vjp
def _tri_solve(lower, rhs):
    return _bmm(_tri_inv(lower), rhs)


def _tri_solve_fwd(lower, rhs):
    inv = _tri_inv(lower)
    sol = _bmm(inv, rhs)
    return sol, (inv, sol)


def _tri_solve_bwd(res, dsol):
    inv, sol = res
    drhs = _bmm_tn(inv, dsol)
    C = inv.shape[-1]
    ii = lax.broadcasted_iota(jnp.int32, (C, C), 0)
    jj = lax.broadcasted_iota(jnp.int32, (C, C), 1)
    return jnp.where((jj < ii)[None], -_bmm_nt(drhs, sol), 0.0), drhs


_tri_solve.defvjp(_tri_solve_fwd, _tri_solve_bwd)


def _pairs(x):
    return jnp.stack([x[:, 128 * p:128 * (p + 1)] for p in range(N_PAIRS)], axis=0)


def _unpairs(x):
    return jnp.concatenate([x[p] for p in range(N_PAIRS)], axis=1)


def _gdn_chunk(state, qkv, gz, ba, a_log_x, dt_x, norm_x):
    C = qkv.shape[0]
    lane = lax.broadcasted_iota(jnp.int32, (1, 1, 128), 2)
    first = lane < HEAD_DIM

    def split_heads(x2):
        return jnp.stack([jnp.where(first, x2, 0.0), jnp.where(first, 0.0, x2)], axis=1).reshape(N_HEADS, *x2.shape[1:])

    def merge_heads(xh):
        x = xh.reshape(N_PAIRS, 2, *xh.shape[1:])
        return x[:, 0] + x[:, 1]

    def head_cols(x2):
        a = jnp.sum(jnp.where(lane == 0, x2, 0.0), axis=-1, keepdims=True)
        b = jnp.sum(jnp.where(lane == HEAD_DIM, x2, 0.0), axis=-1, keepdims=True)
        return jnp.stack([a, b], axis=1).reshape(N_HEADS, *a.shape[1:])

    def to_pair(xh):
        x = xh.reshape(N_PAIRS, 2, *xh.shape[1:])
        return jnp.where(first, x[:, 0], x[:, 1])

    def head_sums(x2):
        a = jnp.sum(jnp.where(first, x2, 0.0), axis=-1, keepdims=True)
        b = jnp.sum(jnp.where(first, 0.0, x2), axis=-1, keepdims=True)
        return jnp.where(first, a, b)

    er = lax.broadcasted_iota(jnp.int32, (128, 2 * HEAD_W), 0)
    ec = lax.broadcasted_iota(jnp.int32, (128, 2 * HEAD_W), 1)
    spread = (er == ec // HEAD_DIM).astype(F32)
    bx = lax.dot_general(ba, spread, (((1,), (0,)), ((), ())), precision=HI, preferred_element_type=F32)
    beta2 = _pairs(_sigmoid(bx[:, :HEAD_W]))
    g2 = _pairs(-jnp.exp(a_log_x) * _softplus(bx[:, HEAD_W:] + dt_x))
    beta = head_cols(beta2)
    g = head_cols(g2)

    q2, k2, v2 = (_pairs(qkv[:, i * HEAD_W:(i + 1) * HEAD_W]) for i in range(3))
    qn2 = q2 * lax.rsqrt(head_sums(q2 * q2) + RMS_EPS) * (HEAD_DIM ** -0.5)
    kn2 = k2 * lax.rsqrt(head_sums(k2 * k2) + RMS_EPS)
    knh = split_heads(kn2)

    ii = lax.broadcasted_iota(jnp.int32, (C, C), 0)[None]
    jj = lax.broadcasted_iota(jnp.int32, (C, C), 1)[None]
    incl = jj <= ii
    g_row = jnp.sum(jnp.where(ii == jj, g, 0.0), axis=1, keepdims=True)
    gc_col = jnp.sum(jnp.where(incl, g_row, 0.0), axis=2, keepdims=True)
    gc_row = jnp.sum(jnp.where(ii <= jj, g, 0.0), axis=1, keepdims=True)
    decay = jnp.where(incl, jnp.exp(jnp.where(incl, gc_col - gc_row, 0.0)), 0.0)
    lower = jnp.where(jj < ii, beta * _bmm_nt(knh, knh) * decay, 0.0)
    gc2 = to_pair(gc_col)
    egc2 = jnp.exp(gc2)
    u2 = merge_heads(_tri_solve(lower, split_heads(v2 * beta2)))
    w2 = merge_heads(_tri_solve(lower, split_heads(kn2 * (beta2 * egc2))))
    qk = jnp.where(incl, _bmm_nt(jnp.repeat(qn2, 2, axis=0), knh) * decay, 0.0)
    g_last2 = to_pair(jnp.sum(g, axis=1, keepdims=True))
    v_new2 = u2 - _bmm(w2, state)
    o2 = _bmm(qn2 * egc2, state) + merge_heads(_bmm(qk, split_heads(v_new2)))
    sr = lax.broadcasted_iota(jnp.int32, (128, 128), 0)
    sc = lax.broadcasted_iota(jnp.int32, (128, 128), 1)
    same_head = ((sr < HEAD_DIM) == (sc < HEAD_DIM))[None]
    new_state = state * jnp.exp(g_last2) + jnp.where(same_head, _bmm_tn(kn2 * jnp.exp(g_last2 - gc2), v_new2), 0.0)
    o2 = o2 * lax.rsqrt(head_sums(o2 * o2) * (1.0 / HEAD_DIM) + RMS_EPS) * _pairs(norm_x)
    gz2 = _pairs(gz)
    return new_state, _unpairs(o2 * (gz2 * _sigmoid(gz2)))


def _gdn_specs(order):
    C = GDN_CHUNK
    par = pl.BlockSpec((1, HEAD_W), lambda c: (0, 0))
    return [pl.BlockSpec((C, CONV_CH), lambda c: (order(c), 0)), pl.BlockSpec((C, HEAD_W), lambda c: (order(c), 3)),
            pl.BlockSpec((C, 128), lambda c: (order(c), 0)), par, par, par]


def _gdn_fwd(qkv, gdnp, ba, a_log_x, dt_x, norm_x):
    S = qkv.shape[0]
    C = GDN_CHUNK
    n = S // C

    def body(qkv_ref, gz_ref, ba_ref, al_ref, dt_ref, nw_ref, o_ref, st_ref, state):
        @pl.when(pl.program_id(0) == 0)
        def _():
            state[...] = jnp.zeros_like(state)

        st = state[...]
        st_ref[...] = st
        new, o = _gdn_chunk(st, qkv_ref[...], gz_ref[...], ba_ref[...], al_ref[...], dt_ref[...], nw_ref[...])
        state[...] = new
        o_ref[...] = o.astype(o_ref.dtype)

    return pl.pallas_call(
        body, name="gdn_fwd", grid=(n,), in_specs=_gdn_specs(lambda c: c),
        out_specs=[pl.BlockSpec((C, HEAD_W), lambda c: (c, 0)),
                   pl.BlockSpec((None, N_PAIRS, 128, 128), lambda c: (c, 0, 0, 0))],
        out_shape=[jax.ShapeDtypeStruct((S, HEAD_W), BF16), jax.ShapeDtypeStruct((n, N_PAIRS, 128, 128), F32)],
        scratch_shapes=[pltpu.VMEM((N_PAIRS, 128, 128), F32)],
        compiler_params=pltpu.CompilerParams(dimension_semantics=("arbitrary",)),
    )(qkv, gdnp, ba, a_log_x, dt_x, norm_x)


def _gdn_bwd(qkv, gdnp, ba, a_log_x, dt_x, norm_x, states, do):
    S = qkv.shape[0]
    C = GDN_CHUNK
    n = S // C

    def body(qkv_ref, gz_ref, ba_ref, al_ref, dt_ref, nw_ref, st_ref, do_ref,
             dqkv_ref, dgz_ref, dba_ref, dal_ref, ddt_ref, dnw_ref, dstate):
        @pl.when(pl.program_id(0) == 0)
        def _():
            dstate[...] = jnp.zeros_like(dstate)
            dal_ref[...] = jnp.zeros_like(dal_ref)
            ddt_ref[...] = jnp.zeros_like(ddt_ref)
            dnw_ref[...] = jnp.zeros_like(dnw_ref)

        args = (st_ref[...], qkv_ref[...], gz_ref[...], ba_ref[...], al_ref[...], dt_ref[...], nw_ref[...])
        _, vjp = jax.vjp(_gdn_chunk, *args)
        dst, dqkv, dgz, dba, dal, ddt, dnw = vjp((dstate[...], do_ref[...]))
        dstate[...] = dst
        dqkv_ref[...] = dqkv
        dgz_ref[...] = dgz.astype(dgz_ref.dtype)
        dba_ref[...] = dba
        dal_ref[...] += dal
        ddt_ref[...] += ddt
        dnw_ref[...] += dnw

    rev = lambda c: n - 1 - c
    par = pl.BlockSpec((1, HEAD_W), lambda c: (0, 0))
    f = jax.ShapeDtypeStruct
    return pl.pallas_call(
        body, name="gdn_bwd", grid=(n,),
        in_specs=_gdn_specs(rev) + [pl.BlockSpec((None, N_PAIRS, 128, 128), lambda c: (rev(c), 0, 0, 0)),
                                    pl.BlockSpec((C, HEAD_W), lambda c: (rev(c), 0))],
        out_specs=[pl.BlockSpec((C, CONV_CH), lambda c: (rev(c), 0)), pl.BlockSpec((C, HEAD_W), lambda c: (rev(c), 0)),
                   pl.BlockSpec((C, 128), lambda c: (rev(c), 0)), par, par, par],
        out_shape=[f((S, CONV_CH), F32), f((S, HEAD_W), BF16), f((S, 128), F32)] + [f((1, HEAD_W), F32)] * 3,
        scratch_shapes=[pltpu.VMEM((N_PAIRS, 128, 128), F32)],
        compiler_params=pltpu.CompilerParams(dimension_semantics=("arbitrary",)),
    )(qkv, gdnp, ba, a_log_x, dt_x, norm_x, states, do)


def _local_step(x, p, target, W, P):
    S = x.shape[0]
    D = D_MODEL
    xb = x.astype(BF16)
    pb = p.astype(BF16)

    def ffn_fwd(tag, h, hb, w_in, w_out, g, b):
        a, gu = _ffn_up(f"{tag}_up", hb, w_in)
        f = _ffn_down(f"{tag}_down", a, w_out)

        def fn(h, f, g, b):
            r = ALPHA * h + 0.5 * f
            y = _ln(r, g, b)
            return (r, y, y), ()

        (r, y, yb), _ = _rowwise(f"{tag}_ln", fn, [(h, 0, D), (f, 0, D)], [g, b], [(D, F32), (D, F32), (D, BF16)])
        return a, gu, r, y, yb

    a1, gu1, r1, h1, h1b = ffn_fwd("ffn1", x, xb, W["ffn1_w_in"], W["ffn1_w_out"], P["ln1_g"], P["ln1_b"])

    wmix = W["w_mix_pad"]
    sbp = _mm("mix_sb", h1b, wmix, BF16, n=3 * HEAD_W, col_off=MIX_SB)
    gdnp = _mm("mix_gdn", h1b, wmix, F32, n=4 * HEAD_W, col_off=MIX_GDN)
    gates = _mm("mix_gates", h1b, wmix, F32, n=2 * D, col_off=MIX_GATES)
    ba = _mm("mix_ba", h1b, wmix, F32, n=128, col_off=MIX_BA)

    attb, sb_tot_a, sb_tot_b, sb_visited = _sb_fwd(sbp)
    y_sb = _mm("sb_out", attb, W["w_branch_sb"], F32)

    conv_w = P["conv_w"]
    qkv = _conv_fwd("conv_fwd", gdnp, conv_w)
    a_log_x = jnp.repeat(P["a_log"], HEAD_DIM, axis=1)
    dt_x = jnp.repeat(P["dt_bias"], HEAD_DIM, axis=1)
    norm_x = jnp.tile(P["gdn_norm_w"], (1, N_HEADS))
    gob, states = _gdn_fwd(qkv, gdnp, ba, a_log_x, dt_x, norm_x)
    y_gdn = _mm("gdn_out", gob, W["w_branch_gdn"], F32)

    def merge_fn(gs, gg, ys, yg, bs, bg):
        return ((_sigmoid(gs + bs) * ys + _sigmoid(gg + bg) * yg),), ()

    b_gate = P["b_gate"]
    bs, bg = b_gate[:, :D], b_gate[:, D:]
    (merged,), _ = _rowwise("mix_merge", merge_fn, [(gates, 0, D), (gates, 1, D), (y_sb, 0, D), (y_gdn, 0, D)], [bs, bg],
                            [(D, BF16)])
    mix = _mm("mix_out", merged, W["w_mix_out"], F32)

    def ln2_fn(h, f, g, b):
        r = ALPHA * h + f
        y = _ln(r, g, b)
        return (r, y, y), ()

    (r2, h2, h2b), _ = _rowwise("mix_ln", ln2_fn, [(h1, 0, D), (mix, 0, D)], [P["ln2_g"], P["ln2_b"]],
                                [(D, F32), (D, F32), (D, BF16)])

    a2, gu2, r3, h3, h3b = ffn_fwd("ffn2", h2, h2b, W["ffn2_w_in"], W["ffn2_w_out"], P["ln3_g"], P["ln3_b"])

    zg = _mm("ple_gate", h3b, W["w_ple_gate"], F32)
    pp = _mm("ple_proj", pb, W["w_ple"], F32)

    def ple(h, zg, pp, bp, g, b):
        return _ln(ALPHA * h + _sigmoid(zg + bp) * pp, g, b)

    def head_fn(h, zg, pp, tgt, bp, g, b):
        y, vjp = jax.vjp(ple, h, zg, pp, bp, g, b)
        err = y - tgt
        dh, dzg, dpp, dbp, dg, db = vjp(err * (1.0 / D))
        loss = 0.5 * jnp.sum(jnp.sum(err * err, axis=1, keepdims=True), axis=0, keepdims=True) * (1.0 / D)
        return (dh, dzg, dpp), (loss, dbp, dg, db)

    (dh3_a, dzg, dpp), (loss, d_bple, d_ln4g, d_ln4b) = _rowwise(
        "ple_head", head_fn, [(h3, 0, D), (zg, 0, D), (pp, 0, D), (target, 0, D)],
        [P["b_ple_gate"], P["ln4_g"], P["ln4_b"]], [(D, F32), (D, BF16), (D, BF16)],
        [(1, 1), (1, D), (1, D), (1, D)])

    grads, small = {}, {"b_ple_gate": d_bple, "ln4_g": d_ln4g, "ln4_b": d_ln4b}
    grads["w_ple_gate"] = _mm_tn("d_w_ple_gate", h3b, dzg, BF16)
    grads["w_ple"] = _mm_tn("d_w_ple", pb, dpp, BF16)
    dh3_b = _mm_nt("d_ple_gate_in", dzg, W["w_ple_gate"], F32)

    def ffn_bwd(tag, dy_parts, r, g, b, a, gu, hb_in, w_in, w_out):
        n_parts = len(dy_parts)

        def fn(*vals):
            dy = vals[0]
            for extra in vals[1:n_parts]:
                dy = dy + extra
            r, g, b = vals[n_parts:]
            dr, dg, db = _ln_bwd(r, g, b, dy)
            return (ALPHA * dr, 0.5 * dr), (dg, db)

        (dh_res, dfb), (dg, db) = _rowwise(f"{tag}_ln_bwd", fn, [(t, 0, D) for t in dy_parts] + [(r, 0, D)], [g, b],
                                           [(D, F32), (D, BF16)], [(1, D), (1, D)])
        dgu = _ffn_bwd_act(f"{tag}_act_bwd", dfb, w_out, gu)
        dgu8 = dgu.reshape(N_DEV, S, SHARD_FF)
        d_w_out = _ffn_d_w_out(f"d_{tag}_w_out", a, dfb)
        d_w_in = _ffn_d_w_in(f"d_{tag}_w_in", hb_in, dgu8)
        dh_ffn = _ffn_in_bwd(f"{tag}_in_bwd", dgu8, w_in)
        return dh_res, dh_ffn, d_w_in, d_w_out, dg, db

    dh2_a, dh2_b, grads["ffn2_w_in"], grads["ffn2_w_out"], small["ln3_g"], small["ln3_b"] = ffn_bwd(
        "ffn2", [dh3_a, dh3_b], r3, P["ln3_g"], P["ln3_b"], a2, gu2, h2b, W["ffn2_w_in"], W["ffn2_w_out"])

    def ln2_bwd_fn(d1, d2, r, g, b):
        dr, dg, db = _ln_bwd(r, g, b, d1 + d2)
        return (ALPHA * dr, dr), (dg, db)

    (dh1_a, dmixb), (small["ln2_g"], small["ln2_b"]) = _rowwise(
        "mix_ln_bwd", ln2_bwd_fn, [(dh2_a, 0, D), (dh2_b, 0, D), (r2, 0, D)], [P["ln2_g"], P["ln2_b"]],
        [(D, F32), (D, BF16)], [(1, D), (1, D)])
    grads["w_mix_out"] = _mm_tn("d_w_mix_out", merged, dmixb, BF16)
    dmerged = _mm_nt("mix_out_bwd", dmixb, W["w_mix_out"], F32)

    def merge_bwd_fn(dm, gs, gg, ys, yg, bs, bg):
        ss, sg = _sigmoid(gs + bs), _sigmoid(gg + bg)
        dgs = dm * ys * ss * (1.0 - ss)
        dgg = dm * yg * sg * (1.0 - sg)
        return (dgs, dgg, dm * ss, dm * sg), (jnp.sum(dgs, axis=0, keepdims=True), jnp.sum(dgg, axis=0, keepdims=True))

    (dgs, dgg, dy_sb, dy_gdn), (d_bs, d_bg) = _rowwise(
        "mix_merge_bwd", merge_bwd_fn, [(dmerged, 0, D), (gates, 0, D), (gates, 1, D), (y_sb, 0, D), (y_gdn, 0, D)],
        [bs, bg], [(D, BF16)] * 4, [(1, D), (1, D)])
    small["b_gate"] = jnp.concatenate([d_bs, d_bg], axis=1)

    grads["w_branch_sb"] = _mm_tn("d_w_branch_sb", attb, dy_sb, BF16)
    datt = _mm_nt("sb_out_bwd", dy_sb, W["w_branch_sb"], BF16)
    dsq, dsk, dsv = _sb_bwd(sbp, sb_tot_a, sb_tot_b, sb_visited, datt)

    grads["w_branch_gdn"] = _mm_tn("d_w_branch_gdn", gob, dy_gdn, BF16)
    dgo = _mm_nt("gdn_out_bwd", dy_gdn, W["w_branch_gdn"], F32)
    dqkv, dgz, dba, d_alog_x, d_dt_x, d_norm_x = _gdn_bwd(qkv, gdnp, ba, a_log_x, dt_x, norm_x, states, dgo)
    small["a_log"] = jnp.sum(d_alog_x.reshape(N_HEADS, HEAD_DIM), axis=1).reshape(1, N_HEADS)
    small["dt_bias"] = jnp.sum(d_dt_x.reshape(N_HEADS, HEAD_DIM), axis=1).reshape(1, N_HEADS)
    small["gdn_norm_w"] = jnp.sum(d_norm_x.reshape(N_HEADS, HEAD_DIM), axis=0).reshape(1, HEAD_DIM)
    dconv_in, d_conv_w = _conv_bwd("conv_bwd", gdnp, conv_w, dqkv)
    grads["conv_w"] = d_conv_w[:CONV_K]

    dproj = jnp.concatenate([dsq.astype(BF16), dsk.astype(BF16), dsv.astype(BF16), dconv_in, dgz, dgs, dgg,
                             dba.astype(BF16)], axis=1)
    d_wmix = _mm_tn("d_w_mix_in", h1b, dproj, BF16)
    grads["w_mix_in"] = jnp.concatenate([d_wmix[:, :MIX_GATES], d_wmix[:, MIX_BA:MIX_BA + 2 * N_HEADS],
                                         d_wmix[:, MIX_GATES:MIX_BA]], axis=1)
    dh1_b = _mm_nt("mix_in_bwd", dproj, wmix, F32)

    dx_a, dx_b, grads["ffn1_w_in"], grads["ffn1_w_out"], small["ln1_g"], small["ln1_b"] = ffn_bwd(
        "ffn1", [dh1_a, dh1_b], r1, P["ln1_g"], P["ln1_b"], a1, gu1, xb, W["ffn1_w_in"], W["ffn1_w_out"])

    (grad_x,), _ = _rowwise("grad_x", lambda a, b: ((a + b,), ()), [(dx_a, 0, D), (dx_b, 0, D)], [], [(D, F32)])
    return loss[0, 0], grad_x, grads, small


def _coords():
    return lax.axis_index("x"), lax.axis_index("y"), lax.axis_index("c")


def _all_gather(name, shards):
    n = len(shards)

    def body(*refs):
        x_refs, out_refs = refs[:n], refs[n:2 * n]
        send_sems, recv_sems, local_sems = refs[2 * n:]
        x, y, c = _coords()
        me, sibling = (x, y, c), (x, y, 1 - c)
        chips = [(1 - x, y), (x, 1 - y), (1 - x, 1 - y)]

        def copy(a, k, block, to, src=None):
            px, py, pc = block
            dst = out_refs[a].at[4 * px + 2 * py + pc]
            return pltpu.make_async_remote_copy(
                src_ref=dst if src is None else src, dst_ref=dst,
                send_sem=send_sems.at[a, k], recv_sem=recv_sems.at[a, k], device_id=to, device_id_type=MESH)

        mine = [pltpu.make_async_copy(x_refs[a], out_refs[a].at[4 * x + 2 * y + c], local_sems.at[a]) for a in range(n)]
        for cp in mine:
            cp.start()
        first = []
        for a in range(n):
            first.append(copy(a, 0, me, sibling, src=x_refs[a]))
            first += [copy(a, 1 + j, me, (*chip, c), src=x_refs[a]) for j, chip in enumerate(chips)]
        for cp in first:
            cp.start()
        passed = []
        for j, chip in enumerate(chips):
            for a in range(n):
                copy(a, 1 + j, (*chip, c), me).wait_recv()
                fwd = copy(a, 4 + j, (*chip, c), sibling)
                fwd.start()
                passed.append(fwd)
        for a in range(n):
            copy(a, 0, sibling, me).wait_recv()
            for j, chip in enumerate(chips):
                copy(a, 4 + j, (*chip, 1 - c), me).wait_recv()
        for cp in first + passed:
            cp.wait_send()
        for cp in mine:
            cp.wait()

    any_spec = pl.BlockSpec(memory_space=pl.ANY)
    return pl.pallas_call(
        body, name=name, out_shape=[jax.ShapeDtypeStruct((N_DEV, *s.shape), s.dtype) for s in shards],
        in_specs=[any_spec] * n, out_specs=[any_spec] * n,
        scratch_shapes=[pltpu.SemaphoreType.DMA((n, 7)), pltpu.SemaphoreType.DMA((n, 7)), pltpu.SemaphoreType.DMA((n,))],
    )(*shards)


def _all_to_all(name, srcs):
    n = len(srcs)

    def body(*refs):
        src_refs, dst_refs = refs[:n], refs[n:2 * n]
        send_sems, recv_sems, local_sems = refs[2 * n:]
        x, y, c = _coords()
        me = 4 * x + 2 * y + c
        mine = [pltpu.make_async_copy(src_refs[a].at[me], dst_refs[a].at[me], local_sems.at[a]) for a in range(n)]
        for cp in mine:
            cp.start()
        sends, recvs = [], []
        for k in range(1, N_DEV):
            px = 1 - x if k & 4 else x
            py = 1 - y if k & 2 else y
            pc = 1 - c if k & 1 else c
            peer = 4 * px + 2 * py + pc
            for a in range(n):
                sends.append(pltpu.make_async_remote_copy(
                    src_ref=src_refs[a].at[peer], dst_ref=dst_refs[a].at[me], send_sem=send_sems.at[a, k - 1],
                    recv_sem=recv_sems.at[a, k - 1], device_id=(px, py, pc), device_id_type=MESH))
                recvs.append(pltpu.make_async_remote_copy(
                    src_ref=src_refs[a].at[me], dst_ref=dst_refs[a].at[peer], send_sem=send_sems.at[a, k - 1],
                    recv_sem=recv_sems.at[a, k - 1], device_id=(px, py, pc), device_id_type=MESH))
        for cp in sends:
            cp.start()
        for cp in recvs:
            cp.wait_recv()
        for cp in sends:
            cp.wait_send()
        for cp in mine:
            cp.wait()

    any_spec = pl.BlockSpec(memory_space=pl.ANY)
    return pl.pallas_call(
        body, name=name, out_shape=[jax.ShapeDtypeStruct(s.shape, s.dtype) for s in srcs],
        in_specs=[any_spec] * n, out_specs=[any_spec] * n,
        scratch_shapes=[pltpu.SemaphoreType.DMA((n, 7)), pltpu.SemaphoreType.DMA((n, 7)), pltpu.SemaphoreType.DMA((n,))],
    )(*srcs)


def _adamw(name, parts, w, m, v):
    R, C = w.shape
    tr = _pick(R, (256, 176, 128, R))
    c1 = 1.0 - ADAM_B1 ** ADAM_STEP
    c2 = 1.0 - ADAM_B2 ** ADAM_STEP

    def body(p_ref, w_ref, m_ref, v_ref, g_ref, d_ref, nm_ref, nv_ref):
        g = p_ref[0].astype(F32)
        for d in range(1, N_DEV):
            g = g + p_ref[d].astype(F32)
        nm = ADAM_B1 * m_ref[...] + (1.0 - ADAM_B1) * g
        nv = ADAM_B2 * v_ref[...] + (1.0 - ADAM_B2) * (g * g)
        g_ref[...] = g
        nm_ref[...] = nm
        nv_ref[...] = nv
        d_ref[...] = -ADAM_LR * ((nm / c1) / (jnp.sqrt(nv / c2) + ADAM_EPS) + ADAM_WD * w_ref[...])

    t = pl.BlockSpec((tr, C), lambda i: (i, 0))
    o = jax.ShapeDtypeStruct((R, C), F32)
    return pl.pallas_call(
        body, name=name, grid=(R // tr,),
        in_specs=[pl.BlockSpec((N_DEV, tr, C), lambda i: (0, i, 0)), t, t, t],
        out_specs=[t, t, t, t], out_shape=[o, o, o, o],
        compiler_params=pltpu.CompilerParams(dimension_semantics=("parallel",)),
    )(parts, w, m, v)


def _pack_rows(flats, rows):
    cat = jnp.concatenate(flats, axis=-1)
    return jnp.pad(cat, [(0, rows * PACK_COLS - cat.shape[-1])]).reshape(rows, PACK_COLS)


def _col_shards(full):
    r, cdim = full.shape
    return full.reshape(r, N_DEV, cdim // N_DEV).transpose(1, 0, 2)


def _from_col_shards(sh):
    _, r, c = sh.shape
    return sh.transpose(1, 0, 2).reshape(r, N_DEV * c)


def kernel(x, p, ffn1_w_in, ffn1_w_out, ln1_g, ln1_b, w_mix_in, b_gate, conv_w, a_log, dt_bias, gdn_norm_w, w_branch_sb, w_branch_gdn, w_mix_out, ln2_g, ln2_b, ffn2_w_in, ffn2_w_out, ln3_g, ln3_b, w_ple_gate, b_ple_gate, w_ple, ln4_g, ln4_b, loss_target, m_ffn1_w_in, m_ffn1_w_out, m_ln1_g, m_ln1_b, m_w_mix_in, m_b_gate, m_conv_w, m_a_log, m_dt_bias, m_gdn_norm_w, m_w_branch_sb, m_w_branch_gdn, m_w_mix_out, m_ln2_g, m_ln2_b, m_ffn2_w_in, m_ffn2_w_out, m_ln3_g, m_ln3_b, m_w_ple_gate, m_b_ple_gate, m_w_ple, m_ln4_g, m_ln4_b, v_ffn1_w_in, v_ffn1_w_out, v_ln1_g, v_ln1_b, v_w_mix_in, v_b_gate, v_conv_w, v_a_log, v_dt_bias, v_gdn_norm_w, v_w_branch_sb, v_w_branch_gdn, v_w_mix_out, v_ln2_g, v_ln2_b, v_ffn2_w_in, v_ffn2_w_out, v_ln3_g, v_ln3_b, v_w_ple_gate, v_b_ple_gate, v_w_ple, v_ln4_g, v_ln4_b):
    given = dict(locals())
    w_loc = {n: given[n][0] for n in WEIGHTS}
    m_loc = {n: given["m_" + n][0] for n in WEIGHTS}
    v_loc = {n: given["v_" + n][0] for n in WEIGHTS}
    sizes = {n: w_loc[n].size for n in WEIGHTS}

    conv_hi = conv_w[0].astype(BF16)
    conv_lo = (conv_w[0] - conv_hi.astype(F32)).astype(BF16)
    gathered = _all_gather("gather_weights", [w_loc[n].astype(BF16) for n in BIG] + [conv_lo])
    W = dict(zip(BIG, gathered[:-1]))
    for n in ("ffn1_w_out", "ffn2_w_out"):
        W[n] = W[n].reshape(N_DEV // 2, SHARD_FF, D_MODEL)
    for n in ("w_mix_out", "w_ple_gate"):
        W[n] = W[n].reshape(D_MODEL, D_MODEL)
    for n in ("w_branch_sb", "w_branch_gdn", "w_ple"):
        W[n] = _from_col_shards(W[n])
    wm = _from_col_shards(W.pop("w_mix_in"))
    W["w_mix_pad"] = jnp.concatenate([wm[:, :7 * HEAD_W], wm[:, 7 * HEAD_W + 2 * N_HEADS:],
                                      wm[:, 7 * HEAD_W:7 * HEAD_W + 2 * N_HEADS],
                                      jnp.zeros((D_MODEL, 128 - 2 * N_HEADS), BF16)], axis=1)
    P = {n: w_loc[n].reshape(1, -1) for n in SMALL}
    P["conv_w"] = _from_col_shards(W.pop("conv_w").astype(F32) + gathered[-1].astype(F32))

    loss, grad_x, grads, small = _local_step(x[0], p[0, 0], loss_target[0], W, P)
    loss = lax.psum(loss, ("x", "y", "c"))

    for n in ("w_mix_in", "w_branch_sb", "w_branch_gdn", "w_ple"):
        grads[n] = _col_shards(grads[n])
    grads["conv_w"] = _col_shards(grads["conv_w"]).astype(BF16)
    send = [grads[n].reshape(N_DEV, *w_loc[n].shape) for n in BIG]
    parts = dict(zip(BIG, _all_to_all("scatter_grads", send)))
    small_rows = 16
    (small_parts,) = _all_gather("gather_small_grads", [_pack_rows([small[n].reshape(-1) for n in SMALL], small_rows)])

    res = {n: _adamw(f"adamw_{n}", parts[n], w_loc[n], m_loc[n], v_loc[n]) for n in BIG}
    pack = lambda d: _pack_rows([d[n].reshape(-1) for n in SMALL], small_rows)
    small_out = _adamw("adamw_small", small_parts, pack(w_loc), pack(m_loc), pack(v_loc))
    off = 0
    for n in SMALL:
        res[n] = [o.reshape(-1)[off:off + sizes[n]] for o in small_out]
        off += sizes[n]
    outs = [[res[n][i].reshape(given[n].shape) for n in WEIGHTS] for i in range(4)]
    g_out, d_out, nm_out, nv_out = outs
    return (loss, grad_x[None], *g_out, *d_out, *nm_out, *nv_out)
```

```python
import functools
import math

import jax
import jax.numpy as jnp
from jax import lax
from jax.experimental import pallas as pl
from jax.experimental.pallas import tpu as pltpu

F32 = jnp.float32
BF16 = jnp.bfloat16
MESH = pl.DeviceIdType.MESH
HI = lax.Precision.HIGHEST

N_DEV = 8
D_MODEL = 1024
D_FF = 2816
PLE_DIM = 256
N_HEADS = 8
HEAD_DIM = 64
HEAD_W = N_HEADS * HEAD_DIM
GDN_CHUNK = 64
CONV_K = 4
CONV_CH = 3 * HEAD_W
ALPHA = 2.0 ** 0.25
LN_EPS = 1e-5
RMS_EPS = 1e-6
MIX_SB = 0
MIX_GDN = 3 * HEAD_W
MIX_GATES = MIX_GDN + 4 * HEAD_W
MIX_BA = MIX_GATES + 2 * D_MODEL
MIX_PAD = MIX_BA + 128
N_IN = 7 * HEAD_W + 2 * N_HEADS + 2 * D_MODEL

ADAM_LR = 0.001
ADAM_B1 = 0.9
ADAM_B2 = 0.999
ADAM_EPS = 1e-08
ADAM_WD = 0.01
ADAM_STEP = 10

SB_TQ = 512
SB_TK = 128
SB_CUTOFF = -110.0
HALO = 8

BIG = ["ffn1_w_in", "ffn1_w_out", "w_mix_in", "conv_w", "w_branch_sb", "w_branch_gdn", "w_mix_out",
       "ffn2_w_in", "ffn2_w_out", "w_ple_gate", "w_ple"]
BIG_SHAPES = {
    "ffn1_w_in": ((D_MODEL, 2 * D_FF), 1), "ffn1_w_out": ((D_FF, D_MODEL), 0),
    "w_mix_in": ((D_MODEL, N_IN), 1), "conv_w": ((CONV_K, CONV_CH), 1),
    "w_branch_sb": ((HEAD_W, D_MODEL), 1), "w_branch_gdn": ((HEAD_W, D_MODEL), 1),
    "w_mix_out": ((D_MODEL, D_MODEL), 0),
    "ffn2_w_in": ((D_MODEL, 2 * D_FF), 1), "ffn2_w_out": ((D_FF, D_MODEL), 0),
    "w_ple_gate": ((D_MODEL, D_MODEL), 0), "w_ple": ((PLE_DIM, D_MODEL), 1),
}
SMALL = ["ln1_g", "ln1_b", "b_gate", "a_log", "dt_bias", "gdn_norm_w", "ln2_g", "ln2_b", "ln3_g", "ln3_b",
         "b_ple_gate", "ln4_g", "ln4_b"]
WEIGHTS = ["ffn1_w_in", "ffn1_w_out", "ln1_g", "ln1_b", "w_mix_in", "b_gate", "conv_w", "a_log", "dt_bias",
           "gdn_norm_w", "w_branch_sb", "w_branch_gdn", "w_mix_out", "ln2_g", "ln2_b", "ffn2_w_in", "ffn2_w_out",
           "ln3_g", "ln3_b", "w_ple_gate", "b_ple_gate", "w_ple", "ln4_g", "ln4_b"]
PACK_COLS = 1024
ADAM_ROWS = 208


def _pick(n, cands):
    for c in cands:
        if n % c == 0:
            return c
    raise ValueError(f"no tile for {n} in {cands}")


_NN = (((1,), (0,)), ((), ()))
_NT = (((1,), (1,)), ((), ()))
_TN = (((0,), (0,)), ((), ()))
SHARD_FF = 2 * D_FF // N_DEV
ROW_TILES = (1024, 512, 256, 128)


def _matmul(name, dims, a, b, a_spec, b_spec, out_sds, out_spec, out_block, grid, nk):
    def body(a_ref, b_ref, o_ref, acc_ref):
        k = pl.program_id(2)

        @pl.when(k == 0)
        def _():
            acc_ref[...] = jnp.zeros_like(acc_ref)

        acc_ref[...] += lax.dot_general(a_ref[...], b_ref[...], dims, preferred_element_type=F32)

        @pl.when(k == nk - 1)
        def _():
            o_ref[...] = acc_ref[...].astype(o_ref.dtype)

    return pl.pallas_call(
        body, name=name, grid=grid, in_specs=[a_spec, b_spec], out_specs=out_spec, out_shape=out_sds,
        scratch_shapes=[pltpu.VMEM(out_block, F32)],
        compiler_params=pltpu.CompilerParams(dimension_semantics=("parallel", "parallel", "arbitrary")),
    )(a, b)


def _mm(name, a, w, out_dtype, n=None, col_off=0):
    M, K = a.shape
    n = w.shape[1] if n is None else n
    tm = _pick(M, ROW_TILES)
    tn = _pick(n, (512, 384, 256, 128))
    tk = _pick(K, (1024, 512, 256))
    assert col_off % tn == 0
    cb, nk = col_off // tn, K // tk
    return _matmul(
        name, _NN, a, w, pl.BlockSpec((tm, tk), lambda i, j, k: (i, k)), pl.BlockSpec((tk, tn), lambda i, j, k: (k, j + cb)),
        jax.ShapeDtypeStruct((M, n), out_dtype), pl.BlockSpec((tm, tn), lambda i, j, k: (i, j)), (tm, tn),
        (M // tm, n // tn, nk), nk)


def _mm_nt(name, a, w, out_dtype):
    M, K = a.shape
    N = w.shape[0]
    tm = _pick(M, ROW_TILES)
    tn = _pick(N, (1024, 512, 256, 128))
    tk = _pick(K, (1024, 1152, 512, 256))
    nk = K // tk
    return _matmul(
        name, _NT, a, w, pl.BlockSpec((tm, tk), lambda i, j, k: (i, k)), pl.BlockSpec((tn, tk), lambda i, j, k: (j, k)),
        jax.ShapeDtypeStruct((M, N), out_dtype), pl.BlockSpec((tm, tn), lambda i, j, k: (i, j)), (tm, tn),
        (M // tm, N // tn, nk), nk)


def _mm_tn(name, a, b, out_dtype):
    S, M = a.shape
    N = b.shape[1]
    tm = _pick(M, (1024, 512, 256, 128))
    tn = _pick(N, (1024, 640, 512, 256, 128))
    tk = _pick(S, ROW_TILES)
    nk = S // tk
    return _matmul(
        name, _TN, a, b, pl.BlockSpec((tk, tm), lambda i, j, k: (k, i)), pl.BlockSpec((tk, tn), lambda i, j, k: (k, j)),
        jax.ShapeDtypeStruct((M, N), out_dtype), pl.BlockSpec((tm, tn), lambda i, j, k: (i, j)), (tm, tn),
        (M // tm, N // tn, nk), nk)


def _sigmoid(z):
    return 1.0 / (1.0 + jnp.exp(-z))


def _ffn_up(name, xb, w_in):
    S, D = xb.shape
    c = w_in.shape[2]
    ts = _pick(S, ROW_TILES)
    half = N_DEV // 2

    def body(x_ref, wg_ref, wu_ref, a_ref, gu_ref):
        x = x_ref[...]
        g = jnp.dot(x, wg_ref[...], preferred_element_type=F32)
        u = jnp.dot(x, wu_ref[...], preferred_element_type=F32)
        a_ref[...] = (g * _sigmoid(g) * u).astype(a_ref.dtype)
        gu_ref[0] = g.astype(gu_ref.dtype)
        gu_ref[1] = u.astype(gu_ref.dtype)

    return pl.pallas_call(
        body, name=name, grid=(S // ts, half),
        in_specs=[pl.BlockSpec((ts, D), lambda i, j: (i, 0)), pl.BlockSpec((None, D, c), lambda i, j: (j, 0, 0)),
                  pl.BlockSpec((None, D, c), lambda i, j: (j + half, 0, 0))],
        out_specs=[pl.BlockSpec((None, ts, c), lambda i, j: (j, i, 0)),
                   pl.BlockSpec((2, None, ts, c), lambda i, j: (0, j, i, 0))],
        out_shape=[jax.ShapeDtypeStruct((half, S, c), BF16), jax.ShapeDtypeStruct((2, half, S, c), BF16)],
        compiler_params=pltpu.CompilerParams(dimension_semantics=("parallel", "parallel")),
    )(xb, w_in, w_in)


def _ffn_down(name, a4, w_out4):
    n, S, c = a4.shape
    D = w_out4.shape[2]
    tm = _pick(S, ROW_TILES)
    tn = _pick(D, (1024, 512, 256, 128))
    return _matmul(
        name, _NN, a4, w_out4, pl.BlockSpec((None, tm, c), lambda i, j, k: (k, i, 0)),
        pl.BlockSpec((None, c, tn), lambda i, j, k: (k, 0, j)),
        jax.ShapeDtypeStruct((S, D), F32), pl.BlockSpec((tm, tn), lambda i, j, k: (i, j)), (tm, tn),
        (S // tm, D // tn, n), n)


def _ffn_bwd_act(name, dyb, w_out4, gu):
    S, D = dyb.shape
    n, c, _ = w_out4.shape
    ts = _pick(S, ROW_TILES)

    def body(dy_ref, w_ref, gu_ref, o_ref):
        da = lax.dot_general(dy_ref[...], w_ref[...], _NT, preferred_element_type=F32)
        g = gu_ref[0].astype(F32)
        u = gu_ref[1].astype(F32)
        sg = _sigmoid(g)
        o_ref[0] = (da * u * (sg * (1.0 + g * (1.0 - sg)))).astype(o_ref.dtype)
        o_ref[1] = (da * (g * sg)).astype(o_ref.dtype)

    blk = pl.BlockSpec((2, None, ts, c), lambda i, j: (0, j, i, 0))
    return pl.pallas_call(
        body, name=name, grid=(S // ts, n),
        in_specs=[pl.BlockSpec((ts, D), lambda i, j: (i, 0)), pl.BlockSpec((None, c, D), lambda i, j: (j, 0, 0)), blk],
        out_specs=blk, out_shape=jax.ShapeDtypeStruct((2, n, S, c), BF16),
        compiler_params=pltpu.CompilerParams(dimension_semantics=("parallel", "parallel")),
    )(dyb, w_out4, gu)


def _ffn_d_w_out(name, a4, dyb):
    n, S, c = a4.shape
    D = dyb.shape[1]
    tn = _pick(D, (1024, 512, 256, 128))
    tk = _pick(S, ROW_TILES)
    nk = S // tk
    return _matmul(
        name, _TN, a4, dyb, pl.BlockSpec((None, tk, c), lambda i, j, k: (i, k, 0)), pl.BlockSpec((tk, tn), lambda i, j, k: (k, j)),
        jax.ShapeDtypeStruct((n, c, D), BF16), pl.BlockSpec((None, c, tn), lambda i, j, k: (i, 0, j)), (c, tn),
        (n, D // tn, nk), nk)


def _ffn_d_w_in(name, hb, dgu8):
    S, D = hb.shape
    n, _, c = dgu8.shape
    tm = _pick(D, (1024, 512, 256, 128))
    tk = _pick(S, ROW_TILES)
    nk = S // tk
    return _matmul(
        name, _TN, hb, dgu8, pl.BlockSpec((tk, tm), lambda i, j, k: (k, j)), pl.BlockSpec((None, tk, c), lambda i, j, k: (i, k, 0)),
        jax.ShapeDtypeStruct((n, D, c), BF16), pl.BlockSpec((None, tm, c), lambda i, j, k: (i, j, 0)), (tm, c),
        (n, D // tm, nk), nk)


def _ffn_in_bwd(name, dgu8, w_in):
    n, S, c = dgu8.shape
    D = w_in.shape[1]
    tm = _pick(S, ROW_TILES)
    tn = _pick(D, (1024, 512, 256, 128))
    return _matmul(
        name, _NT, dgu8, w_in, pl.BlockSpec((None, tm, c), lambda i, j, k: (k, i, 0)),
        pl.BlockSpec((None, tn, c), lambda i, j, k: (k, j, 0)),
        jax.ShapeDtypeStruct((S, D), F32), pl.BlockSpec((tm, tn), lambda i, j, k: (i, j)), (tm, tn),
        (S // tm, D // tn, n), n)


def _rowwise(name, fn, tiled, params, outs, reds=(), ts=256):
    S = tiled[0][0].shape[0]
    ts = _pick(S, (ts, 128))
    n_t, n_p, n_o = len(tiled), len(params), len(outs)

    def body(*refs):
        vals = [r[...] for r in refs[:n_t + n_p]]
        res_o, res_r = fn(*vals)
        o_refs = refs[n_t + n_p:n_t + n_p + n_o]
        r_refs = refs[n_t + n_p + n_o:]
        for r, val in zip(o_refs, res_o, strict=True):
            r[...] = val.astype(r.dtype)
        if r_refs:
            i = pl.program_id(0)

            @pl.when(i == 0)
            def _():
                for r, val in zip(r_refs, res_r, strict=True):
                    r[...] = val.astype(F32)

            @pl.when(i > 0)
            def _():
                for r, val in zip(r_refs, res_r, strict=True):
                    r[...] += val.astype(F32)

    in_specs = [pl.BlockSpec((ts, w), functools.partial(lambda i, cb: (i, cb), cb=cb)) for _, cb, w in tiled]
    in_specs += [pl.BlockSpec(p.shape, lambda i: (0, 0)) for p in params]
    out_specs = [pl.BlockSpec((ts, w), lambda i: (i, 0)) for w, _ in outs]
    out_specs += [pl.BlockSpec(tuple(r), lambda i: (0, 0)) for r in reds]
    out_shape = [jax.ShapeDtypeStruct((S, w), dt) for w, dt in outs]
    out_shape += [jax.ShapeDtypeStruct(tuple(r), F32) for r in reds]
    res = pl.pallas_call(
        body, name=name, grid=(S // ts,), in_specs=in_specs, out_specs=out_specs, out_shape=out_shape,
        compiler_params=pltpu.CompilerParams(dimension_semantics=("arbitrary",)),
    )(*[t[0] for t in tiled], *params)
    return res[:n_o], res[n_o:]


def _ln(r, g, b):
    mu = jnp.mean(r, axis=-1, keepdims=True)
    xc = r - mu
    var = jnp.mean(xc * xc, axis=-1, keepdims=True)
    return xc * lax.rsqrt(var + LN_EPS) * g + b


def _ln_bwd(r, g, b, dy):
    _, vjp = jax.vjp(_ln, r, g, b)
    return vjp(dy)


def _shift_down(x, d):
    return x if d == 0 else pltpu.roll(x, d, 0)


def _conv_fwd(name, x, w):
    S = x.shape[0]
    ts = _pick(S, (256, 128))
    hb = ts // HALO

    def body(x_ref, prev_ref, w_ref, o_ref):
        i = pl.program_id(0)
        prev = jnp.where(i > 0, prev_ref[...], 0.0)
        xe = jnp.concatenate([prev, x_ref[...]], axis=0)
        y = jnp.zeros((ts + HALO, CONV_CH), F32)
        for j in range(CONV_K):
            y = y + w_ref[pl.ds(j, 1), :] * _shift_down(xe, CONV_K - 1 - j)
        y = y[HALO:, :]
        o_ref[...] = y * _sigmoid(y)

    return pl.pallas_call(
        body, name=name, grid=(S // ts,),
        in_specs=[pl.BlockSpec((ts, CONV_CH), lambda i: (i, 0)),
                  pl.BlockSpec((HALO, CONV_CH), lambda i: (jnp.maximum(i * hb - 1, 0), 0)),
                  pl.BlockSpec((CONV_K, CONV_CH), lambda i: (0, 0))],
        out_specs=pl.BlockSpec((ts, CONV_CH), lambda i: (i, 0)),
        out_shape=jax.ShapeDtypeStruct((S, CONV_CH), F32),
        compiler_params=pltpu.CompilerParams(dimension_semantics=("arbitrary",)),
    )(x, x, w)


def _conv_bwd(name, x, w, dout):
    S = x.shape[0]
    ts = _pick(S, (256, 128))
    hb = ts // HALO
    nt = S // ts
    n_ext = ts + 2 * HALO

    def body(x_ref, prev_ref, next_ref, w_ref, d_ref, dnext_ref, dx_ref, dw_ref):
        i = pl.program_id(0)
        prev = jnp.where(i > 0, prev_ref[...], 0.0)
        last = i == nt - 1
        nxt = jnp.where(last, 0.0, next_ref[...])
        dnxt = jnp.where(last, 0.0, dnext_ref[...])
        xe = jnp.concatenate([prev, x_ref[...], nxt], axis=0)
        de = jnp.concatenate([jnp.zeros((HALO, CONV_CH), F32), d_ref[...], dnxt], axis=0)
        y = jnp.zeros((n_ext, CONV_CH), F32)
        for j in range(CONV_K):
            y = y + w_ref[pl.ds(j, 1), :] * _shift_down(xe, CONV_K - 1 - j)
        sg = _sigmoid(y)
        dy = de * (sg * (1.0 + y * (1.0 - sg)))
        dx = jnp.zeros((n_ext, CONV_CH), F32)
        for j in range(CONV_K):
            m = CONV_K - 1 - j
            dx = dx + w_ref[pl.ds(j, 1), :] * (dy if m == 0 else pltpu.roll(dy, n_ext - m, 0))
        dx_ref[...] = dx[HALO:HALO + ts, :].astype(dx_ref.dtype)
        row = lax.broadcasted_iota(jnp.int32, (n_ext, 1), 0)
        dy_own = jnp.where((row >= HALO) & (row < HALO + ts), dy, 0.0)
        parts = [jnp.sum(dy_own * _shift_down(xe, CONV_K - 1 - j), axis=0, keepdims=True) for j in range(CONV_K)]
        dw = jnp.concatenate(parts + [jnp.zeros((HALO - CONV_K, CONV_CH), F32)], axis=0)

        @pl.when(i == 0)
        def _():
            dw_ref[...] = dw

        @pl.when(i > 0)
        def _():
            dw_ref[...] += dw

    tile = pl.BlockSpec((ts, CONV_CH), lambda i: (i, 0))
    prev = pl.BlockSpec((HALO, CONV_CH), lambda i: (jnp.maximum(i * hb - 1, 0), 0))
    nxt = pl.BlockSpec((HALO, CONV_CH), lambda i: (jnp.minimum((i + 1) * hb, nt * hb - 1), 0))
    return pl.pallas_call(
        body, name=name, grid=(nt,),
        in_specs=[tile, prev, nxt, pl.BlockSpec((CONV_K, CONV_CH), lambda i: (0, 0)), tile, nxt],
        out_specs=[tile, pl.BlockSpec((HALO, CONV_CH), lambda i: (0, 0))],
        out_shape=[jax.ShapeDtypeStruct((S, CONV_CH), BF16), jax.ShapeDtypeStruct((HALO, CONV_CH), F32)],
        compiler_params=pltpu.CompilerParams(dimension_semantics=("arbitrary",)),
    )(x, x, x, w, dout, dout)


def _softplus(z):
    return jnp.maximum(z, 0.0) + jnp.log(1.0 + jnp.exp(-jnp.abs(z)))


def _dot2(x, tri2):
    hi = x.astype(BF16)
    lo = (x - hi.astype(F32)).astype(BF16)
    return jnp.dot(jnp.concatenate([hi, lo], axis=1), tri2, preferred_element_type=F32)


def _pair_tri(tk, keep):
    r = lax.broadcasted_iota(jnp.int32, (4 * tk, 2 * tk), 0)
    c = lax.broadcasted_iota(jnp.int32, (4 * tk, 2 * tk), 1)
    same_head = ((r // tk) % 2) == (c // tk)
    return (same_head & keep(r % tk, c % tk)).astype(BF16)


def _pair_rows(x2, first):
    return jnp.concatenate([jnp.where(first, x2, 0), jnp.where(first, 0, x2)], axis=0)


def _pair_sum(x):
    h = x.shape[1] // 2
    return jnp.sum(x[:, :h], axis=1, keepdims=True), jnp.sum(x[:, h:], axis=1, keepdims=True)


def _sb_tiles(S):
    tq = _pick(S, (SB_TQ, 256, 128))
    return tq, SB_TK


N_PAIRS = N_HEADS // 2


def _sb_specs(S, tq):
    q = pl.BlockSpec((tq, 128), lambda p, i: (i, p))
    k = pl.BlockSpec((S, 128), lambda p, i: (0, N_PAIRS + p))
    v = pl.BlockSpec((S, 128), lambda p, i: (0, 2 * N_PAIRS + p))
    return q, k, v


def _scaled(q):
    return (q.astype(F32) * (HEAD_DIM ** -0.5)).astype(BF16)


def _sb_fwd(qkv):
    S = qkv.shape[0]
    tq, tk = _sb_tiles(S)
    nd = tq // tk

    def body(q_ref, k_ref, v_ref, o_ref, ta_ref, tb_ref, cnt_ref):
        qi = pl.program_id(1)
        qs = _scaled(q_ref[...])
        row = qi * tq + lax.broadcasted_iota(jnp.int32, (tq, 2 * tk), 0)
        col = lax.broadcasted_iota(jnp.int32, (tq, 2 * tk), 1)
        left = col < tk
        col = col % tk
        later = _pair_tri(tk, lambda r, c: r > c)
        first = lax.broadcasted_iota(jnp.int32, (tk, 128), 1) < HEAD_DIM

        def tile(j, carry, masked):
            acc, suf_a, suf_b = carry
            off = pl.multiple_of(j * tk, tk)
            kc = _pair_rows(k_ref[pl.ds(off, tk), :], first)
            vc = _pair_rows(v_ref[pl.ds(off, tk), :], first)
            mask = (col + j * tk) < row
            z = lax.dot_general(qs, kc, _NT, preferred_element_type=F32)
            lf = -_softplus(z)
            if masked:
                lf = jnp.where(mask, lf, 0.0)
            w = jnp.exp(z + lf + _dot2(lf, later) + jnp.where(left, suf_a, suf_b))
            if masked:
                w = jnp.where(mask, w, 0.0)
            acc = acc + jnp.dot(w.astype(BF16), vc, preferred_element_type=F32)
            sum_a, sum_b = _pair_sum(lf)
            return acc, suf_a + sum_a, suf_b + sum_b

        zero1 = jnp.zeros((tq, 1), F32)
        carry = (jnp.zeros((tq, 128), F32), zero1, zero1)
        n_full = qi * nd
        acc, suf_a, suf_b = lax.fori_loop(0, nd, lambda d, c: tile(n_full + nd - 1 - d, c, True), carry)

        def alive(sa, sb):
            return jnp.maximum(jnp.max(sa), jnp.max(sb)) > SB_CUTOFF

        def step(c):
            jj, acc, sa, sb, _ = c
            acc, sa, sb = tile(n_full - 1 - jj, (acc, sa, sb), False)
            return jj + 1, acc, sa, sb, alive(sa, sb)

        n_used, acc, suf_a, suf_b, _ = lax.while_loop(
            lambda c: (c[0] < n_full) & c[4], step, (0, acc, suf_a, suf_b, alive(suf_a, suf_b)))
        o_ref[...] = acc.astype(o_ref.dtype)
        ta_ref[...] = suf_a
        tb_ref[...] = suf_b
        cnt_ref[pl.program_id(0), qi] = n_used

    tot = pl.BlockSpec((None, tq, 1), lambda p, i: (p, i, 0))
    tshape = jax.ShapeDtypeStruct((N_PAIRS, S, 1), F32)
    return pl.pallas_call(
        body, name="sb_fwd", grid=(N_PAIRS, S // tq), in_specs=list(_sb_specs(S, tq)),
        out_specs=[pl.BlockSpec((tq, 128), lambda p, i: (i, p)), tot, tot, pl.BlockSpec(memory_space=pltpu.SMEM)],
        out_shape=[jax.ShapeDtypeStruct((S, HEAD_W), BF16), tshape, tshape,
                   jax.ShapeDtypeStruct((N_PAIRS, S // tq), jnp.int32)],
        compiler_params=pltpu.CompilerParams(dimension_semantics=("arbitrary", "arbitrary")),
    )(qkv, qkv, qkv)


def _sb_bwd(qkv, tot_a, tot_b, visited, do):
    S = qkv.shape[0]
    tq, tk = _sb_tiles(S)
    nd = tq // tk
    scale = HEAD_DIM ** -0.5

    def body(q_ref, k_ref, v_ref, ta_ref, tb_ref, cnt_ref, do_ref, dq_ref, dk_ref, dv_ref):
        qi = pl.program_id(1)

        @pl.when(qi == 0)
        def _():
            dk_ref[...] = jnp.zeros_like(dk_ref)
            dv_ref[...] = jnp.zeros_like(dv_ref)

        qs = _scaled(q_ref[...])
        do2 = do_ref[...]
        row = qi * tq + lax.broadcasted_iota(jnp.int32, (tq, 2 * tk), 0)
        col = lax.broadcasted_iota(jnp.int32, (tq, 2 * tk), 1)
        left = col < tk
        col = col % tk
        total = jnp.where(left, ta_ref[...], tb_ref[...])
        upto = _pair_tri(tk, lambda r, c: r <= c)
        first = lax.broadcasted_iota(jnp.int32, (tk, 128), 1) < HEAD_DIM

        def tile(j, carry, masked):
            dq, pl_a, pl_b, pg_a, pg_b = carry
            off = pl.multiple_of(j * tk, tk)
            kc = _pair_rows(k_ref[pl.ds(off, tk), :], first)
            vc = _pair_rows(v_ref[pl.ds(off, tk), :], first)
            mask = (col + j * tk) < row
            z = lax.dot_general(qs, kc, _NT, preferred_element_type=F32)
            lf = -_softplus(z)
            sig = jnp.exp(z + lf)
            if masked:
                lf = jnp.where(mask, lf, 0.0)
            w = jnp.exp(z + lf + (total - (jnp.where(left, pl_a, pl_b) + _dot2(lf, upto))))
            if masked:
                w = jnp.where(mask, w, 0.0)
            gl = lax.dot_general(do2, vc, _NT, preferred_element_type=F32) * w
            dz = gl - sig * (jnp.where(left, pg_a, pg_b) + _dot2(gl, upto))
            if masked:
                dz = jnp.where(mask, dz, 0.0)
            dzb = dz.astype(BF16)
            dq = dq + jnp.dot(dzb, kc, preferred_element_type=F32)
            dkc = lax.dot_general(dzb, qs, _TN, preferred_element_type=F32)
            dvc = lax.dot_general(w.astype(BF16), do2, _TN, preferred_element_type=F32)
            dk_ref[pl.ds(off, tk), :] += jnp.where(first, dkc[:tk], dkc[tk:])
            dv_ref[pl.ds(off, tk), :] += jnp.where(first, dvc[:tk], dvc[tk:])
            sl_a, sl_b = _pair_sum(lf)
            sg_a, sg_b = _pair_sum(gl)
            return dq, pl_a + sl_a, pl_b + sl_b, pg_a + sg_a, pg_b + sg_b

        zero1 = jnp.zeros((tq, 1), F32)
        carry = (jnp.zeros((tq, 128), F32), zero1, zero1, zero1, zero1)
        n_full = qi * nd
        n_used = cnt_ref[pl.program_id(0), qi]
        first_tile = n_full - n_used
        carry = lax.fori_loop(0, n_used, lambda t, c: tile(first_tile + t, c, False), carry)
        dq = lax.fori_loop(0, nd, lambda d, c: tile(n_full + d, c, True), carry)[0]
        dq_ref[...] = dq * scale

    q_spec, k_spec, v_spec = _sb_specs(S, tq)
    tot = pl.BlockSpec((None, tq, 1), lambda p, i: (p, i, 0))
    whole = pl.BlockSpec((S, 128), lambda p, i: (0, p))
    big = jax.ShapeDtypeStruct((S, HEAD_W), F32)
    return pl.pallas_call(
        body, name="sb_bwd", grid=(N_PAIRS, S // tq),
        in_specs=[q_spec, k_spec, v_spec, tot, tot, pl.BlockSpec(memory_space=pltpu.SMEM), q_spec],
        out_specs=[q_spec, whole, whole], out_shape=[big, big, big],
        compiler_params=pltpu.CompilerParams(dimension_semantics=("arbitrary", "arbitrary")),
    )(qkv, qkv, qkv, tot_a, tot_b, visited, do)


GDN_PRECISION = lax.Precision.HIGH


def _bmm(a, b):
    return lax.dot_general(a, b, (((2,), (1,)), ((0,), (0,))), precision=GDN_PRECISION, preferred_element_type=F32)


def _bmm_nt(a, b):
    return lax.dot_general(a, b, (((2,), (2,)), ((0,), (0,))), precision=GDN_PRECISION, preferred_element_type=F32)


def _bmm_tn(a, b):
    return lax.dot_general(a, b, (((1,), (1,)), ((0,), (0,))), precision=GDN_PRECISION, preferred_element_type=F32)


def _tri_inv(lower):
    C = lower.shape[-1]
    ii = lax.broadcasted_iota(jnp.int32, (C, C), 0)
    jj = lax.broadcasted_iota(jnp.int32, (C, C), 1)
    eye = (ii == jj).astype(F32)[None]
    xd = jnp.where((ii // 8 == jj // 8)[None], -lower, 0.0)
    x2 = _bmm(xd, xd)
    x4 = _bmm(x2, x2)
    inv = eye + xd
    inv = inv + _bmm(inv, x2)
    inv = inv + _bmm(inv, x4)
    b = 8
    while b < C:
        off = jnp.where(((ii // (2 * b) == jj // (2 * b)) & (ii // b != jj // b))[None], lower, 0.0)
        inv = inv - _bmm(inv, _bmm(off, inv))
        b *= 2
    return inv


@jax.custom_vjp
def _tri_solve(lower, inv, rhs):
    return _bmm(inv, rhs)


def _tri_solve_fwd(lower, inv, rhs):
    sol = _bmm(inv, rhs)
    return sol, (inv, sol)


def _tri_solve_bwd(res, dsol):
    inv, sol = res
    drhs = _bmm_tn(inv, dsol)
    C = inv.shape[-1]
    ii = lax.broadcasted_iota(jnp.int32, (C, C), 0)
    jj = lax.broadcasted_iota(jnp.int32, (C, C), 1)
    return jnp.where((jj < ii)[None], -_bmm_nt(drhs, sol), 0.0), jnp.zeros_like(inv), drhs


_tri_solve.defvjp(_tri_solve_fwd, _tri_solve_bwd)


def _pairs(x):
    return jnp.stack([x[:, 128 * p:128 * (p + 1)] for p in range(N_PAIRS)], axis=0)


def _unpairs(x):
    return jnp.concatenate([x[p] for p in range(N_PAIRS)], axis=1)


def _gdn_chunk(state, qkv, gz, ba, a_log_x, dt_x, norm_x, inv=None):
    C = qkv.shape[0]
    lane = lax.broadcasted_iota(jnp.int32, (1, 1, 128), 2)
    first = lane < HEAD_DIM

    def split_heads(x2):
        return jnp.stack([jnp.where(first, x2, 0.0), jnp.where(first, 0.0, x2)], axis=1).reshape(N_HEADS, *x2.shape[1:])

    def merge_heads(xh):
        x = xh.reshape(N_PAIRS, 2, *xh.shape[1:])
        return x[:, 0] + x[:, 1]

    def head_cols(x2):
        a = jnp.sum(jnp.where(lane == 0, x2, 0.0), axis=-1, keepdims=True)
        b = jnp.sum(jnp.where(lane == HEAD_DIM, x2, 0.0), axis=-1, keepdims=True)
        return jnp.stack([a, b], axis=1).reshape(N_HEADS, *a.shape[1:])

    def to_pair(xh):
        x = xh.reshape(N_PAIRS, 2, *xh.shape[1:])
        return jnp.where(first, x[:, 0], x[:, 1])

    def head_sums(x2):
        a = jnp.sum(jnp.where(first, x2, 0.0), axis=-1, keepdims=True)
        b = jnp.sum(jnp.where(first, 0.0, x2), axis=-1, keepdims=True)
        return jnp.where(first, a, b)

    er = lax.broadcasted_iota(jnp.int32, (128, 2 * HEAD_W), 0)
    ec = lax.broadcasted_iota(jnp.int32, (128, 2 * HEAD_W), 1)
    spread = (er == ec // HEAD_DIM).astype(F32)
    bx = lax.dot_general(ba, spread, (((1,), (0,)), ((), ())), precision=HI, preferred_element_type=F32)
    beta2 = _pairs(_sigmoid(bx[:, :HEAD_W]))
    g2 = _pairs(-jnp.exp(a_log_x) * _softplus(bx[:, HEAD_W:] + dt_x))
    beta = head_cols(beta2)
    g = head_cols(g2)

    q2, k2, v2 = (_pairs(qkv[:, i * HEAD_W:(i + 1) * HEAD_W]) for i in range(3))
    qn2 = q2 * lax.rsqrt(head_sums(q2 * q2) + RMS_EPS) * (HEAD_DIM ** -0.5)
    kn2 = k2 * lax.rsqrt(head_sums(k2 * k2) + RMS_EPS)
    knh = split_heads(kn2)

    ii = lax.broadcasted_iota(jnp.int32, (C, C), 0)[None]
    jj = lax.broadcasted_iota(jnp.int32, (C, C), 1)[None]
    incl = jj <= ii
    g_row = jnp.sum(jnp.where(ii == jj, g, 0.0), axis=1, keepdims=True)
    gc_col = jnp.sum(jnp.where(incl, g_row, 0.0), axis=2, keepdims=True)
    gc_row = jnp.sum(jnp.where(ii <= jj, g, 0.0), axis=1, keepdims=True)
    decay = jnp.where(incl, jnp.exp(jnp.where(incl, gc_col - gc_row, 0.0)), 0.0)
    lower = jnp.where(jj < ii, beta * _bmm_nt(knh, knh) * decay, 0.0)
    gc2 = to_pair(gc_col)
    egc2 = jnp.exp(gc2)
    if inv is None:
        inv = _tri_inv(lower)
    u2 = merge_heads(_tri_solve(lower, inv, split_heads(v2 * beta2)))
    w2 = merge_heads(_tri_solve(lower, inv, split_heads(kn2 * (beta2 * egc2))))
    qk = jnp.where(incl, _bmm_nt(jnp.repeat(qn2, 2, axis=0), knh) * decay, 0.0)
    g_last2 = to_pair(jnp.sum(g, axis=1, keepdims=True))
    v_new2 = u2 - _bmm(w2, state)
    o2 = _bmm(qn2 * egc2, state) + merge_heads(_bmm(qk, split_heads(v_new2)))
    sr = lax.broadcasted_iota(jnp.int32, (128, 128), 0)
    sc = lax.broadcasted_iota(jnp.int32, (128, 128), 1)
    same_head = ((sr < HEAD_DIM) == (sc < HEAD_DIM))[None]
    new_state = state * jnp.exp(g_last2) + jnp.where(same_head, _bmm_tn(kn2 * jnp.exp(g_last2 - gc2), v_new2), 0.0)
    o2 = o2 * lax.rsqrt(head_sums(o2 * o2) * (1.0 / HEAD_DIM) + RMS_EPS) * _pairs(norm_x)
    gz2 = _pairs(gz)
    return new_state, _unpairs(o2 * (gz2 * _sigmoid(gz2))), inv


def _gdn_specs(order):
    C = GDN_CHUNK
    par = pl.BlockSpec((1, HEAD_W), lambda c: (0, 0))
    return [pl.BlockSpec((C, CONV_CH), lambda c: (order(c), 0)), pl.BlockSpec((C, HEAD_W), lambda c: (order(c), 3)),
            pl.BlockSpec((C, 128), lambda c: (order(c), 0)), par, par, par]


def _gdn_fwd(qkv, gdnp, ba, a_log_x, dt_x, norm_x):
    S = qkv.shape[0]
    C = GDN_CHUNK
    n = S // C

    def body(qkv_ref, gz_ref, ba_ref, al_ref, dt_ref, nw_ref, o_ref, st_ref, inv_ref, state):
        @pl.when(pl.program_id(0) == 0)
        def _():
            state[...] = jnp.zeros_like(state)

        st = state[...]
        st_ref[...] = st
        new, o, inv = _gdn_chunk(st, qkv_ref[...], gz_ref[...], ba_ref[...], al_ref[...], dt_ref[...], nw_ref[...])
        state[...] = new
        o_ref[...] = o.astype(o_ref.dtype)
        inv_ref[...] = inv

    return pl.pallas_call(
        body, name="gdn_fwd", grid=(n,), in_specs=_gdn_specs(lambda c: c),
        out_specs=[pl.BlockSpec((C, HEAD_W), lambda c: (c, 0)),
                   pl.BlockSpec((None, N_PAIRS, 128, 128), lambda c: (c, 0, 0, 0)),
                   pl.BlockSpec((None, N_HEADS, C, C), lambda c: (c, 0, 0, 0))],
        out_shape=[jax.ShapeDtypeStruct((S, HEAD_W), BF16), jax.ShapeDtypeStruct((n, N_PAIRS, 128, 128), F32),
                   jax.ShapeDtypeStruct((n, N_HEADS, C, C), F32)],
        scratch_shapes=[pltpu.VMEM((N_PAIRS, 128, 128), F32)],
        compiler_params=pltpu.CompilerParams(dimension_semantics=("arbitrary",)),
    )(qkv, gdnp, ba, a_log_x, dt_x, norm_x)


def _gdn_bwd(qkv, gdnp, ba, a_log_x, dt_x, norm_x, states, invs, do):
    S = qkv.shape[0]
    C = GDN_CHUNK
    n = S // C

    def body(qkv_ref, gz_ref, ba_ref, al_ref, dt_ref, nw_ref, st_ref, inv_ref, do_ref,
             dqkv_ref, dgz_ref, dba_ref, dal_ref, ddt_ref, dnw_ref, dstate):
        @pl.when(pl.program_id(0) == 0)
        def _():
            dstate[...] = jnp.zeros_like(dstate)
            dal_ref[...] = jnp.zeros_like(dal_ref)
            ddt_ref[...] = jnp.zeros_like(ddt_ref)
            dnw_ref[...] = jnp.zeros_like(dnw_ref)

        args = (st_ref[...], qkv_ref[...], gz_ref[...], ba_ref[...], al_ref[...], dt_ref[...], nw_ref[...])
        inv = inv_ref[...]
        _, vjp = jax.vjp(lambda *a: _gdn_chunk(*a, inv=inv)[:2], *args)
        dst, dqkv, dgz, dba, dal, ddt, dnw = vjp((dstate[...], do_ref[...]))
        dstate[...] = dst
        dqkv_ref[...] = dqkv
        dgz_ref[...] = dgz.astype(dgz_ref.dtype)
        dba_ref[...] = dba
        dal_ref[...] += dal
        ddt_ref[...] += ddt
        dnw_ref[...] += dnw

    rev = lambda c: n - 1 - c
    par = pl.BlockSpec((1, HEAD_W), lambda c: (0, 0))
    f = jax.ShapeDtypeStruct
    return pl.pallas_call(
        body, name="gdn_bwd", grid=(n,),
        in_specs=_gdn_specs(rev) + [pl.BlockSpec((None, N_PAIRS, 128, 128), lambda c: (rev(c), 0, 0, 0)),
                                    pl.BlockSpec((None, N_HEADS, C, C), lambda c: (rev(c), 0, 0, 0)),
                                    pl.BlockSpec((C, HEAD_W), lambda c: (rev(c), 0))],
        out_specs=[pl.BlockSpec((C, CONV_CH), lambda c: (rev(c), 0)), pl.BlockSpec((C, HEAD_W), lambda c: (rev(c), 0)),
                   pl.BlockSpec((C, 128), lambda c: (rev(c), 0)), par, par, par],
        out_shape=[f((S, CONV_CH), F32), f((S, HEAD_W), BF16), f((S, 128), F32)] + [f((1, HEAD_W), F32)] * 3,
        scratch_shapes=[pltpu.VMEM((N_PAIRS, 128, 128), F32)],
        compiler_params=pltpu.CompilerParams(dimension_semantics=("arbitrary",)),
    )(qkv, gdnp, ba, a_log_x, dt_x, norm_x, states, invs, do)


def _local_step(x, p, target, W, P):
    S = x.shape[0]
    D = D_MODEL
    xb = x.astype(BF16)
    pb = p.astype(BF16)

    def ffn_fwd(tag, h, hb, w_in, w_out, g, b):
        a, gu = _ffn_up(f"{tag}_up", hb, w_in)
        f = _ffn_down(f"{tag}_down", a, w_out)

        def fn(h, f, g, b):
            r = ALPHA * h + 0.5 * f
            y = _ln(r, g, b)
            return (r, y, y), ()

        (r, y, yb), _ = _rowwise(f"{tag}_ln", fn, [(h, 0, D), (f, 0, D)], [g, b], [(D, F32), (D, F32), (D, BF16)])
        return a, gu, r, y, yb

    a1, gu1, r1, h1, h1b = ffn_fwd("ffn1", x, xb, W["ffn1_w_in"], W["ffn1_w_out"], P["ln1_g"], P["ln1_b"])

    wmix = W["w_mix_pad"]
    sbp = _mm("mix_sb", h1b, wmix, BF16, n=3 * HEAD_W, col_off=MIX_SB)
    gdnp = _mm("mix_gdn", h1b, wmix, F32, n=4 * HEAD_W, col_off=MIX_GDN)
    gates = _mm("mix_gates", h1b, wmix, F32, n=2 * D, col_off=MIX_GATES)
    ba = _mm("mix_ba", h1b, wmix, F32, n=128, col_off=MIX_BA)

    attb, sb_tot_a, sb_tot_b, sb_visited = _sb_fwd(sbp)
    y_sb = _mm("sb_out", attb, W["w_branch_sb"], F32)

    conv_w = P["conv_w"]
    qkv = _conv_fwd("conv_fwd", gdnp, conv_w)
    a_log_x = jnp.repeat(P["a_log"], HEAD_DIM, axis=1)
    dt_x = jnp.repeat(P["dt_bias"], HEAD_DIM, axis=1)
    norm_x = jnp.tile(P["gdn_norm_w"], (1, N_HEADS))
    gob, states, invs = _gdn_fwd(qkv, gdnp, ba, a_log_x, dt_x, norm_x)
    y_gdn = _mm("gdn_out", gob, W["w_branch_gdn"], F32)

    def merge_fn(gs, gg, ys, yg, bs, bg):
        return ((_sigmoid(gs + bs) * ys + _sigmoid(gg + bg) * yg),), ()

    b_gate = P["b_gate"]
    bs, bg = b_gate[:, :D], b_gate[:, D:]
    (merged,), _ = _rowwise("mix_merge", merge_fn, [(gates, 0, D), (gates, 1, D), (y_sb, 0, D), (y_gdn, 0, D)], [bs, bg],
                            [(D, BF16)])
    mix = _mm("mix_out", merged, W["w_mix_out"], F32)

    def ln2_fn(h, f, g, b):
        r = ALPHA * h + f
        y = _ln(r, g, b)
        return (r, y, y), ()

    (r2, h2, h2b), _ = _rowwise("mix_ln", ln2_fn, [(h1, 0, D), (mix, 0, D)], [P["ln2_g"], P["ln2_b"]],
                                [(D, F32), (D, F32), (D, BF16)])

    a2, gu2, r3, h3, h3b = ffn_fwd("ffn2", h2, h2b, W["ffn2_w_in"], W["ffn2_w_out"], P["ln3_g"], P["ln3_b"])

    zg = _mm("ple_gate", h3b, W["w_ple_gate"], F32)
    pp = _mm("ple_proj", pb, W["w_ple"], F32)

    def ple(h, zg, pp, bp, g, b):
        return _ln(ALPHA * h + _sigmoid(zg + bp) * pp, g, b)

    def head_fn(h, zg, pp, tgt, bp, g, b):
        y, vjp = jax.vjp(ple, h, zg, pp, bp, g, b)
        err = y - tgt
        dh, dzg, dpp, dbp, dg, db = vjp(err * (1.0 / D))
        loss = 0.5 * jnp.sum(jnp.sum(err * err, axis=1, keepdims=True), axis=0, keepdims=True) * (1.0 / D)
        return (dh, dzg, dpp), (loss, dbp, dg, db)

    (dh3_a, dzg, dpp), (loss, d_bple, d_ln4g, d_ln4b) = _rowwise(
        "ple_head", head_fn, [(h3, 0, D), (zg, 0, D), (pp, 0, D), (target, 0, D)],
        [P["b_ple_gate"], P["ln4_g"], P["ln4_b"]], [(D, F32), (D, BF16), (D, BF16)],
        [(1, 1), (1, D), (1, D), (1, D)])

    grads, small = {}, {"b_ple_gate": d_bple, "ln4_g": d_ln4g, "ln4_b": d_ln4b}
    grads["w_ple_gate"] = _mm_tn("d_w_ple_gate", h3b, dzg, BF16)
    grads["w_ple"] = _mm_tn("d_w_ple", pb, dpp, BF16)
    dh3_b = _mm_nt("d_ple_gate_in", dzg, W["w_ple_gate"], F32)

    def ffn_bwd(tag, dy_parts, r, g, b, a, gu, hb_in, w_in, w_out):
        n_parts = len(dy_parts)

        def fn(*vals):
            dy = vals[0]
            for extra in vals[1:n_parts]:
                dy = dy + extra
            r, g, b = vals[n_parts:]
            dr, dg, db = _ln_bwd(r, g, b, dy)
            return (ALPHA * dr, 0.5 * dr), (dg, db)

        (dh_res, dfb), (dg, db) = _rowwise(f"{tag}_ln_bwd", fn, [(t, 0, D) for t in dy_parts] + [(r, 0, D)], [g, b],
                                           [(D, F32), (D, BF16)], [(1, D), (1, D)])
        dgu = _ffn_bwd_act(f"{tag}_act_bwd", dfb, w_out, gu)
        dgu8 = dgu.reshape(N_DEV, S, SHARD_FF)
        d_w_out = _ffn_d_w_out(f"d_{tag}_w_out", a, dfb)
        d_w_in = _ffn_d_w_in(f"d_{tag}_w_in", hb_in, dgu8)
        dh_ffn = _ffn_in_bwd(f"{tag}_in_bwd", dgu8, w_in)
        return dh_res, dh_ffn, d_w_in, d_w_out, dg, db

    dh2_a, dh2_b, grads["ffn2_w_in"], grads["ffn2_w_out"], small["ln3_g"], small["ln3_b"] = ffn_bwd(
        "ffn2", [dh3_a, dh3_b], r3, P["ln3_g"], P["ln3_b"], a2, gu2, h2b, W["ffn2_w_in"], W["ffn2_w_out"])

    def ln2_bwd_fn(d1, d2, r, g, b):
        dr, dg, db = _ln_bwd(r, g, b, d1 + d2)
        return (ALPHA * dr, dr), (dg, db)

    (dh1_a, dmixb), (small["ln2_g"], small["ln2_b"]) = _rowwise(
        "mix_ln_bwd", ln2_bwd_fn, [(dh2_a, 0, D), (dh2_b, 0, D), (r2, 0, D)], [P["ln2_g"], P["ln2_b"]],
        [(D, F32), (D, BF16)], [(1, D), (1, D)])
    grads["w_mix_out"] = _mm_tn("d_w_mix_out", merged, dmixb, BF16)
    dmerged = _mm_nt("mix_out_bwd", dmixb, W["w_mix_out"], F32)

    def merge_bwd_fn(dm, gs, gg, ys, yg, bs, bg):
        ss, sg = _sigmoid(gs + bs), _sigmoid(gg + bg)
        dgs = dm * ys * ss * (1.0 - ss)
        dgg = dm * yg * sg * (1.0 - sg)
        return (dgs, dgg, dm * ss, dm * sg), (jnp.sum(dgs, axis=0, keepdims=True), jnp.sum(dgg, axis=0, keepdims=True))

    (dgs, dgg, dy_sb, dy_gdn), (d_bs, d_bg) = _rowwise(
        "mix_merge_bwd", merge_bwd_fn, [(dmerged, 0, D), (gates, 0, D), (gates, 1, D), (y_sb, 0, D), (y_gdn, 0, D)],
        [bs, bg], [(D, BF16)] * 4, [(1, D), (1, D)])
    small["b_gate"] = jnp.concatenate([d_bs, d_bg], axis=1)

    grads["w_branch_sb"] = _mm_tn("d_w_branch_sb", attb, dy_sb, BF16)
    datt = _mm_nt("sb_out_bwd", dy_sb, W["w_branch_sb"], BF16)
    dsq, dsk, dsv = _sb_bwd(sbp, sb_tot_a, sb_tot_b, sb_visited, datt)

    grads["w_branch_gdn"] = _mm_tn("d_w_branch_gdn", gob, dy_gdn, BF16)
    dgo = _mm_nt("gdn_out_bwd", dy_gdn, W["w_branch_gdn"], F32)
    dqkv, dgz, dba, d_alog_x, d_dt_x, d_norm_x = _gdn_bwd(qkv, gdnp, ba, a_log_x, dt_x, norm_x, states, invs, dgo)
    small["a_log"] = jnp.sum(d_alog_x.reshape(N_HEADS, HEAD_DIM), axis=1).reshape(1, N_HEADS)
    small["dt_bias"] = jnp.sum(d_dt_x.reshape(N_HEADS, HEAD_DIM), axis=1).reshape(1, N_HEADS)
    small["gdn_norm_w"] = jnp.sum(d_norm_x.reshape(N_HEADS, HEAD_DIM), axis=0).reshape(1, HEAD_DIM)
    dconv_in, d_conv_w = _conv_bwd("conv_bwd", gdnp, conv_w, dqkv)
    grads["conv_w"] = d_conv_w[:CONV_K]

    dproj = jnp.concatenate([dsq.astype(BF16), dsk.astype(BF16), dsv.astype(BF16), dconv_in, dgz, dgs, dgg,
                             dba.astype(BF16)], axis=1)
    d_wmix = _mm_tn("d_w_mix_in", h1b, dproj, BF16)
    grads["w_mix_in"] = jnp.concatenate([d_wmix[:, :MIX_GATES], d_wmix[:, MIX_BA:MIX_BA + 2 * N_HEADS],
                                         d_wmix[:, MIX_GATES:MIX_BA]], axis=1)
    dh1_b = _mm_nt("mix_in_bwd", dproj, wmix, F32)

    dx_a, dx_b, grads["ffn1_w_in"], grads["ffn1_w_out"], small["ln1_g"], small["ln1_b"] = ffn_bwd(
        "ffn1", [dh1_a, dh1_b], r1, P["ln1_g"], P["ln1_b"], a1, gu1, xb, W["ffn1_w_in"], W["ffn1_w_out"])

    (grad_x,), _ = _rowwise("grad_x", lambda a, b: ((a + b,), ()), [(dx_a, 0, D), (dx_b, 0, D)], [], [(D, F32)])
    return loss[0, 0], grad_x, grads, small


def _coords():
    return lax.axis_index("x"), lax.axis_index("y"), lax.axis_index("c")


def _all_gather(name, shards):
    n = len(shards)

    def body(*refs):
        x_refs, out_refs = refs[:n], refs[n:2 * n]
        send_sems, recv_sems, local_sems = refs[2 * n:]
        x, y, c = _coords()
        me, sibling = (x, y, c), (x, y, 1 - c)
        chips = [(1 - x, y), (x, 1 - y), (1 - x, 1 - y)]

        def copy(a, k, block, to, src=None):
            px, py, pc = block
            dst = out_refs[a].at[4 * px + 2 * py + pc]
            return pltpu.make_async_remote_copy(
                src_ref=dst if src is None else src, dst_ref=dst,
                send_sem=send_sems.at[a, k], recv_sem=recv_sems.at[a, k], device_id=to, device_id_type=MESH)

        mine = [pltpu.make_async_copy(x_refs[a], out_refs[a].at[4 * x + 2 * y + c], local_sems.at[a]) for a in range(n)]
        for cp in mine:
            cp.start()
        first = []
        for a in range(n):
            first.append(copy(a, 0, me, sibling, src=x_refs[a]))
            first += [copy(a, 1 + j, me, (*chip, c), src=x_refs[a]) for j, chip in enumerate(chips)]
        for cp in first:
            cp.start()
        passed = []
        for j, chip in enumerate(chips):
            for a in range(n):
                copy(a, 1 + j, (*chip, c), me).wait_recv()
                fwd = copy(a, 4 + j, (*chip, c), sibling)
                fwd.start()
                passed.append(fwd)
        for a in range(n):
            copy(a, 0, sibling, me).wait_recv()
            for j, chip in enumerate(chips):
                copy(a, 4 + j, (*chip, 1 - c), me).wait_recv()
        for cp in first + passed:
            cp.wait_send()
        for cp in mine:
            cp.wait()

    any_spec = pl.BlockSpec(memory_space=pl.ANY)
    return pl.pallas_call(
        body, name=name, out_shape=[jax.ShapeDtypeStruct((N_DEV, *s.shape), s.dtype) for s in shards],
        in_specs=[any_spec] * n, out_specs=[any_spec] * n,
        scratch_shapes=[pltpu.SemaphoreType.DMA((n, 7)), pltpu.SemaphoreType.DMA((n, 7)), pltpu.SemaphoreType.DMA((n,))],
    )(*shards)


N_CHIPS = N_DEV // 2


def _sibling_exchange(name, srcs):
    n = len(srcs)

    def body(*refs):
        src_refs, dst_refs = refs[:n], refs[n:2 * n]
        send_sems, recv_sems = refs[2 * n:]
        x, y, c = _coords()
        copies = []
        for a in range(n):
            for k in range(N_CHIPS):
                copies.append(pltpu.make_async_remote_copy(
                    src_ref=src_refs[a].at[2 * k + (1 - c)], dst_ref=dst_refs[a].at[k], send_sem=send_sems.at[a, k],
                    recv_sem=recv_sems.at[a, k], device_id=(x, y, 1 - c), device_id_type=MESH))
        for cp in copies:
            cp.start()
        for cp in copies:
            cp.wait_recv()
        for cp in copies:
            cp.wait_send()

    any_spec = pl.BlockSpec(memory_space=pl.ANY)
    return pl.pallas_call(
        body, name=name, out_shape=[jax.ShapeDtypeStruct((N_CHIPS, *s.shape[1:]), s.dtype) for s in srcs],
        in_specs=[any_spec] * n, out_specs=[any_spec] * n,
        scratch_shapes=[pltpu.SemaphoreType.DMA((n, N_CHIPS)), pltpu.SemaphoreType.DMA((n, N_CHIPS))],
    )(*srcs)


def _pair_add(name, core, g, got):
    _, R, C = g.shape
    tr = _pick(R, (256, 176, 128, R))

    def body(c_ref, a_ref, b_ref, o_ref):
        o_ref[...] = (a_ref[...].astype(F32) + b_ref[...].astype(F32)).astype(o_ref.dtype)

    grid_spec = pltpu.PrefetchScalarGridSpec(
        num_scalar_prefetch=1, grid=(N_CHIPS, R // tr),
        in_specs=[pl.BlockSpec((None, None, tr, C), lambda k, i, c: (k, c[0], i, 0)),
                  pl.BlockSpec((None, tr, C), lambda k, i, c: (k, i, 0))],
        out_specs=pl.BlockSpec((None, tr, C), lambda k, i, c: (k, i, 0)))
    return pl.pallas_call(body, name=name, grid_spec=grid_spec, out_shape=jax.ShapeDtypeStruct((N_CHIPS, R, C), g.dtype),
                          )(core, g.reshape(N_CHIPS, 2, R, C), got)


def _chip_exchange(name, srcs):
    n = len(srcs)

    def body(*refs):
        src_refs, dst_refs = refs[:n], refs[n:2 * n]
        send_sems, recv_sems, local_sems = refs[2 * n:]
        x, y, c = _coords()
        me = 2 * x + y
        mine = [pltpu.make_async_copy(src_refs[a].at[me], dst_refs[a].at[me], local_sems.at[a]) for a in range(n)]
        for cp in mine:
            cp.start()
        sends, recvs = [], []
        for r in range(1, N_CHIPS):
            px = 1 - x if r & 2 else x
            py = 1 - y if r & 1 else y
            peer = 2 * px + py
            for a in range(n):
                sends.append(pltpu.make_async_remote_copy(
                    src_ref=src_refs[a].at[peer], dst_ref=dst_refs[a].at[me], send_sem=send_sems.at[a, r - 1],
                    recv_sem=recv_sems.at[a, r - 1], device_id=(px, py, c), device_id_type=MESH))
                recvs.append(pltpu.make_async_remote_copy(
                    src_ref=src_refs[a].at[me], dst_ref=dst_refs[a].at[peer], send_sem=send_sems.at[a, r - 1],
                    recv_sem=recv_sems.at[a, r - 1], device_id=(px, py, c), device_id_type=MESH))
        for cp in sends:
            cp.start()
        for cp in recvs:
            cp.wait_recv()
        for cp in sends:
            cp.wait_send()
        for cp in mine:
            cp.wait()

    any_spec = pl.BlockSpec(memory_space=pl.ANY)
    return pl.pallas_call(
        body, name=name, out_shape=[jax.ShapeDtypeStruct(s.shape, s.dtype) for s in srcs],
        in_specs=[any_spec] * n, out_specs=[any_spec] * n,
        scratch_shapes=[pltpu.SemaphoreType.DMA((n, N_CHIPS - 1)), pltpu.SemaphoreType.DMA((n, N_CHIPS - 1)),
                        pltpu.SemaphoreType.DMA((n,))],
    )(*srcs)


def _adamw(name, parts, w, m, v):
    n_parts = parts.shape[0]
    R, C = w.shape
    tr = _pick(R, (256, 176, 128, R))
    c1 = 1.0 - ADAM_B1 ** ADAM_STEP
    c2 = 1.0 - ADAM_B2 ** ADAM_STEP

    def body(p_ref, w_ref, m_ref, v_ref, g_ref, d_ref, nm_ref, nv_ref):
        g = p_ref[0].astype(F32)
        for d in range(1, n_parts):
            g = g + p_ref[d].astype(F32)
        nm = ADAM_B1 * m_ref[...] + (1.0 - ADAM_B1) * g
        nv = ADAM_B2 * v_ref[...] + (1.0 - ADAM_B2) * (g * g)
        g_ref[...] = g
        nm_ref[...] = nm
        nv_ref[...] = nv
        d_ref[...] = -ADAM_LR * ((nm / c1) / (jnp.sqrt(nv / c2) + ADAM_EPS) + ADAM_WD * w_ref[...])

    t = pl.BlockSpec((tr, C), lambda i: (i, 0))
    o = jax.ShapeDtypeStruct((R, C), F32)
    return pl.pallas_call(
        body, name=name, grid=(R // tr,),
        in_specs=[pl.BlockSpec((n_parts, tr, C), lambda i: (0, i, 0)), t, t, t],
        out_specs=[t, t, t, t], out_shape=[o, o, o, o],
        compiler_params=pltpu.CompilerParams(dimension_semantics=("parallel",)),
    )(parts, w, m, v)


def _pack_rows(flats, rows):
    cat = jnp.concatenate(flats, axis=-1)
    return jnp.pad(cat, [(0, rows * PACK_COLS - cat.shape[-1])]).reshape(rows, PACK_COLS)


def _col_shards(full):
    r, cdim = full.shape
    return full.reshape(r, N_DEV, cdim // N_DEV).transpose(1, 0, 2)


def _from_col_shards(sh):
    _, r, c = sh.shape
    return sh.transpose(1, 0, 2).reshape(r, N_DEV * c)


def kernel(x, p, ffn1_w_in, ffn1_w_out, ln1_g, ln1_b, w_mix_in, b_gate, conv_w, a_log, dt_bias, gdn_norm_w, w_branch_sb, w_branch_gdn, w_mix_out, ln2_g, ln2_b, ffn2_w_in, ffn2_w_out, ln3_g, ln3_b, w_ple_gate, b_ple_gate, w_ple, ln4_g, ln4_b, loss_target, m_ffn1_w_in, m_ffn1_w_out, m_ln1_g, m_ln1_b, m_w_mix_in, m_b_gate, m_conv_w, m_a_log, m_dt_bias, m_gdn_norm_w, m_w_branch_sb, m_w_branch_gdn, m_w_mix_out, m_ln2_g, m_ln2_b, m_ffn2_w_in, m_ffn2_w_out, m_ln3_g, m_ln3_b, m_w_ple_gate, m_b_ple_gate, m_w_ple, m_ln4_g, m_ln4_b, v_ffn1_w_in, v_ffn1_w_out, v_ln1_g, v_ln1_b, v_w_mix_in, v_b_gate, v_conv_w, v_a_log, v_dt_bias, v_gdn_norm_w, v_w_branch_sb, v_w_branch_gdn, v_w_mix_out, v_ln2_g, v_ln2_b, v_ffn2_w_in, v_ffn2_w_out, v_ln3_g, v_ln3_b, v_w_ple_gate, v_b_ple_gate, v_w_ple, v_ln4_g, v_ln4_b):
    given = dict(locals())
    w_loc = {n: given[n][0] for n in WEIGHTS}
    m_loc = {n: given["m_" + n][0] for n in WEIGHTS}
    v_loc = {n: given["v_" + n][0] for n in WEIGHTS}
    sizes = {n: w_loc[n].size for n in WEIGHTS}

    conv_hi = conv_w[0].astype(BF16)
    conv_lo = (conv_w[0] - conv_hi.astype(F32)).astype(BF16)
    gathered = _all_gather("gather_weights", [w_loc[n].astype(BF16) for n in BIG] + [conv_lo])
    W = dict(zip(BIG, gathered[:-1]))
    for n in ("ffn1_w_out", "ffn2_w_out"):
        W[n] = W[n].reshape(N_DEV // 2, SHARD_FF, D_MODEL)
    for n in ("w_mix_out", "w_ple_gate"):
        W[n] = W[n].reshape(D_MODEL, D_MODEL)
    for n in ("w_branch_sb", "w_branch_gdn", "w_ple"):
        W[n] = _from_col_shards(W[n])
    wm = _from_col_shards(W.pop("w_mix_in"))
    W["w_mix_pad"] = jnp.concatenate([wm[:, :7 * HEAD_W], wm[:, 7 * HEAD_W + 2 * N_HEADS:],
                                      wm[:, 7 * HEAD_W:7 * HEAD_W + 2 * N_HEADS],
                                      jnp.zeros((D_MODEL, 128 - 2 * N_HEADS), BF16)], axis=1)
    P = {n: w_loc[n].reshape(1, -1) for n in SMALL}
    P["conv_w"] = _from_col_shards(W.pop("conv_w").astype(F32) + gathered[-1].astype(F32))

    loss, grad_x, grads, small = _local_step(x[0], p[0, 0], loss_target[0], W, P)
    loss = lax.psum(loss, ("x", "y", "c"))

    for n in ("w_mix_in", "w_branch_sb", "w_branch_gdn", "w_ple"):
        grads[n] = _col_shards(grads[n])
    grads["conv_w"] = _col_shards(grads["conv_w"]).astype(BF16)
    send = [grads[n].reshape(N_DEV, *w_loc[n].shape) for n in BIG]
    core = lax.axis_index("c").astype(jnp.int32).reshape(1)
    got = _sibling_exchange("grads_to_sibling", send)
    sums = [_pair_add(f"pair_add_{n}", core, g, h) for n, g, h in zip(BIG, send, got, strict=True)]
    parts = dict(zip(BIG, _chip_exchange("scatter_grads", sums)))
    small_rows = 16
    (small_parts,) = _all_gather("gather_small_grads", [_pack_rows([small[n].reshape(-1) for n in SMALL], small_rows)])

    res = {n: _adamw(f"adamw_{n}", parts[n], w_loc[n], m_loc[n], v_loc[n]) for n in BIG}
    pack = lambda d: _pack_rows([d[n].reshape(-1) for n in SMALL], small_rows)
    small_out = _adamw("adamw_small", small_parts, pack(w_loc), pack(m_loc), pack(v_loc))
    off = 0
    for n in SMALL:
        res[n] = [o.reshape(-1)[off:off + sizes[n]] for o in small_out]
        off += sizes[n]
    outs = [[res[n][i].reshape(given[n].shape) for n in WEIGHTS] for i in range(4)]
    g_out, d_out, nm_out, nv_out = outs
    return (loss, grad_x[None], *g_out, *d_out, *nm_out, *nv_out)
```

```python
import functools
import math

import jax
import jax.numpy as jnp
from jax import lax
from jax.experimental import pallas as pl
from jax.experimental.pallas import tpu as pltpu

F32 = jnp.float32
BF16 = jnp.bfloat16
MESH = pl.DeviceIdType.MESH
HI = lax.Precision.HIGHEST

N_DEV = 8
D_MODEL = 1024
D_FF = 2816
PLE_DIM = 256
N_HEADS = 8
HEAD_DIM = 64
HEAD_W = N_HEADS * HEAD_DIM
GDN_CHUNK = 128
CONV_K = 4
CONV_CH = 3 * HEAD_W
ALPHA = 2.0 ** 0.25
LN_EPS = 1e-5
RMS_EPS = 1e-6
MIX_SB = 0
MIX_GDN = 3 * HEAD_W
MIX_GATES = MIX_GDN + 4 * HEAD_W
MIX_BA = MIX_GATES + 2 * D_MODEL
MIX_PAD = MIX_BA + 128
N_IN = 7 * HEAD_W + 2 * N_HEADS + 2 * D_MODEL

ADAM_LR = 0.001
ADAM_B1 = 0.9
ADAM_B2 = 0.999
ADAM_EPS = 1e-08
ADAM_WD = 0.01
ADAM_STEP = 10

SB_TQ = 512
SB_TK = 128
SB_CUTOFF = -110.0
HALO = 8

BIG = ["ffn1_w_in", "ffn1_w_out", "w_mix_in", "conv_w", "w_branch_sb", "w_branch_gdn", "w_mix_out",
       "ffn2_w_in", "ffn2_w_out", "w_ple_gate", "w_ple"]
BIG_SHAPES = {
    "ffn1_w_in": ((D_MODEL, 2 * D_FF), 1), "ffn1_w_out": ((D_FF, D_MODEL), 0),
    "w_mix_in": ((D_MODEL, N_IN), 1), "conv_w": ((CONV_K, CONV_CH), 1),
    "w_branch_sb": ((HEAD_W, D_MODEL), 1), "w_branch_gdn": ((HEAD_W, D_MODEL), 1),
    "w_mix_out": ((D_MODEL, D_MODEL), 0),
    "ffn2_w_in": ((D_MODEL, 2 * D_FF), 1), "ffn2_w_out": ((D_FF, D_MODEL), 0),
    "w_ple_gate": ((D_MODEL, D_MODEL), 0), "w_ple": ((PLE_DIM, D_MODEL), 1),
}
SMALL = ["ln1_g", "ln1_b", "b_gate", "a_log", "dt_bias", "gdn_norm_w", "ln2_g", "ln2_b", "ln3_g", "ln3_b",
         "b_ple_gate", "ln4_g", "ln4_b"]
WEIGHTS = ["ffn1_w_in", "ffn1_w_out", "ln1_g", "ln1_b", "w_mix_in", "b_gate", "conv_w", "a_log", "dt_bias",
           "gdn_norm_w", "w_branch_sb", "w_branch_gdn", "w_mix_out", "ln2_g", "ln2_b", "ffn2_w_in", "ffn2_w_out",
           "ln3_g", "ln3_b", "w_ple_gate", "b_ple_gate", "w_ple", "ln4_g", "ln4_b"]
PACK_COLS = 1024
ADAM_ROWS = 208


def _pick(n, cands):
    for c in cands:
        if n % c == 0:
            return c
    raise ValueError(f"no tile for {n} in {cands}")


_NN = (((1,), (0,)), ((), ()))
_NT = (((1,), (1,)), ((), ()))
_TN = (((0,), (0,)), ((), ()))
SHARD_FF = 2 * D_FF // N_DEV
ROW_TILES = (1024, 512, 256, 128)


def _matmul(name, dims, a, b, a_spec, b_spec, out_sds, out_spec, out_block, grid, nk):
    def body(a_ref, b_ref, o_ref, acc_ref):
        k = pl.program_id(2)

        @pl.when(k == 0)
        def _():
            acc_ref[...] = jnp.zeros_like(acc_ref)

        acc_ref[...] += lax.dot_general(a_ref[...], b_ref[...], dims, preferred_element_type=F32)

        @pl.when(k == nk - 1)
        def _():
            o_ref[...] = acc_ref[...].astype(o_ref.dtype)

    return pl.pallas_call(
        body, name=name, grid=grid, in_specs=[a_spec, b_spec], out_specs=out_spec, out_shape=out_sds,
        scratch_shapes=[pltpu.VMEM(out_block, F32)],
        compiler_params=pltpu.CompilerParams(dimension_semantics=("parallel", "parallel", "arbitrary")),
    )(a, b)


def _mm(name, a, w, out_dtype, n=None, col_off=0):
    M, K = a.shape
    n = w.shape[1] if n is None else n
    tm = _pick(M, ROW_TILES)
    tn = _pick(n, (512, 384, 256, 128))
    tk = _pick(K, (1024, 512, 256))
    assert col_off % tn == 0
    cb, nk = col_off // tn, K // tk
    return _matmul(
        name, _NN, a, w, pl.BlockSpec((tm, tk), lambda i, j, k: (i, k)), pl.BlockSpec((tk, tn), lambda i, j, k: (k, j + cb)),
        jax.ShapeDtypeStruct((M, n), out_dtype), pl.BlockSpec((tm, tn), lambda i, j, k: (i, j)), (tm, tn),
        (M // tm, n // tn, nk), nk)


def _mm_nt(name, a, w, out_dtype):
    M, K = a.shape
    N = w.shape[0]
    tm = _pick(M, ROW_TILES)
    tn = _pick(N, (1024, 512, 256, 128))
    tk = _pick(K, (1024, 1152, 512, 256))
    nk = K // tk
    return _matmul(
        name, _NT, a, w, pl.BlockSpec((tm, tk), lambda i, j, k: (i, k)), pl.BlockSpec((tn, tk), lambda i, j, k: (j, k)),
        jax.ShapeDtypeStruct((M, N), out_dtype), pl.BlockSpec((tm, tn), lambda i, j, k: (i, j)), (tm, tn),
        (M // tm, N // tn, nk), nk)


def _mm_tn(name, a, b, out_dtype):
    S, M = a.shape
    N = b.shape[1]
    tm = _pick(M, (1024, 512, 256, 128))
    tn = _pick(N, (1024, 640, 512, 256, 128))
    tk = _pick(S, ROW_TILES)
    nk = S // tk
    return _matmul(
        name, _TN, a, b, pl.BlockSpec((tk, tm), lambda i, j, k: (k, i)), pl.BlockSpec((tk, tn), lambda i, j, k: (k, j)),
        jax.ShapeDtypeStruct((M, N), out_dtype), pl.BlockSpec((tm, tn), lambda i, j, k: (i, j)), (tm, tn),
        (M // tm, N // tn, nk), nk)


def _sigmoid(z):
    return 1.0 / (1.0 + jnp.exp(-z))


def _ffn_up(name, xb, w_in):
    S, D = xb.shape
    c = w_in.shape[2]
    ts = _pick(S, ROW_TILES)
    half = N_DEV // 2

    def body(x_ref, wg_ref, wu_ref, a_ref, gu_ref):
        x = x_ref[...]
        g = jnp.dot(x, wg_ref[...], preferred_element_type=F32)
        u = jnp.dot(x, wu_ref[...], preferred_element_type=F32)
        a_ref[...] = (g * _sigmoid(g) * u).astype(a_ref.dtype)
        gu_ref[0] = g.astype(gu_ref.dtype)
        gu_ref[1] = u.astype(gu_ref.dtype)

    return pl.pallas_call(
        body, name=name, grid=(S // ts, half),
        in_specs=[pl.BlockSpec((ts, D), lambda i, j: (i, 0)), pl.BlockSpec((None, D, c), lambda i, j: (j, 0, 0)),
                  pl.BlockSpec((None, D, c), lambda i, j: (j + half, 0, 0))],
        out_specs=[pl.BlockSpec((None, ts, c), lambda i, j: (j, i, 0)),
                   pl.BlockSpec((2, None, ts, c), lambda i, j: (0, j, i, 0))],
        out_shape=[jax.ShapeDtypeStruct((half, S, c), BF16), jax.ShapeDtypeStruct((2, half, S, c), BF16)],
        compiler_params=pltpu.CompilerParams(dimension_semantics=("parallel", "parallel")),
    )(xb, w_in, w_in)


def _ffn_down(name, a4, w_out4):
    n, S, c = a4.shape
    D = w_out4.shape[2]
    tm = _pick(S, ROW_TILES)
    tn = _pick(D, (1024, 512, 256, 128))
    return _matmul(
        name, _NN, a4, w_out4, pl.BlockSpec((None, tm, c), lambda i, j, k: (k, i, 0)),
        pl.BlockSpec((None, c, tn), lambda i, j, k: (k, 0, j)),
        jax.ShapeDtypeStruct((S, D), F32), pl.BlockSpec((tm, tn), lambda i, j, k: (i, j)), (tm, tn),
        (S // tm, D // tn, n), n)


def _ffn_bwd_act(name, dyb, w_out4, gu):
    S, D = dyb.shape
    n, c, _ = w_out4.shape
    ts = _pick(S, ROW_TILES)

    def body(dy_ref, w_ref, gu_ref, o_ref):
        da = lax.dot_general(dy_ref[...], w_ref[...], _NT, preferred_element_type=F32)
        g = gu_ref[0].astype(F32)
        u = gu_ref[1].astype(F32)
        sg = _sigmoid(g)
        o_ref[0] = (da * u * (sg * (1.0 + g * (1.0 - sg)))).astype(o_ref.dtype)
        o_ref[1] = (da * (g * sg)).astype(o_ref.dtype)

    blk = pl.BlockSpec((2, None, ts, c), lambda i, j: (0, j, i, 0))
    return pl.pallas_call(
        body, name=name, grid=(S // ts, n),
        in_specs=[pl.BlockSpec((ts, D), lambda i, j: (i, 0)), pl.BlockSpec((None, c, D), lambda i, j: (j, 0, 0)), blk],
        out_specs=blk, out_shape=jax.ShapeDtypeStruct((2, n, S, c), BF16),
        compiler_params=pltpu.CompilerParams(dimension_semantics=("parallel", "parallel")),
    )(dyb, w_out4, gu)


def _ffn_d_w_out(name, a4, dyb):
    n, S, c = a4.shape
    D = dyb.shape[1]
    tn = _pick(D, (1024, 512, 256, 128))
    tk = _pick(S, ROW_TILES)
    nk = S // tk
    return _matmul(
        name, _TN, a4, dyb, pl.BlockSpec((None, tk, c), lambda i, j, k: (i, k, 0)), pl.BlockSpec((tk, tn), lambda i, j, k: (k, j)),
        jax.ShapeDtypeStruct((n, c, D), BF16), pl.BlockSpec((None, c, tn), lambda i, j, k: (i, 0, j)), (c, tn),
        (n, D // tn, nk), nk)


def _ffn_d_w_in(name, hb, dgu8):
    S, D = hb.shape
    n, _, c = dgu8.shape
    tm = _pick(D, (1024, 512, 256, 128))
    tk = _pick(S, ROW_TILES)
    nk = S // tk
    return _matmul(
        name, _TN, hb, dgu8, pl.BlockSpec((tk, tm), lambda i, j, k: (k, j)), pl.BlockSpec((None, tk, c), lambda i, j, k: (i, k, 0)),
        jax.ShapeDtypeStruct((n, D, c), BF16), pl.BlockSpec((None, tm, c), lambda i, j, k: (i, j, 0)), (tm, c),
        (n, D // tm, nk), nk)


def _ffn_in_bwd(name, dgu8, w_in):
    n, S, c = dgu8.shape
    D = w_in.shape[1]
    tm = _pick(S, ROW_TILES)
    tn = _pick(D, (1024, 512, 256, 128))
    return _matmul(
        name, _NT, dgu8, w_in, pl.BlockSpec((None, tm, c), lambda i, j, k: (k, i, 0)),
        pl.BlockSpec((None, tn, c), lambda i, j, k: (k, j, 0)),
        jax.ShapeDtypeStruct((S, D), F32), pl.BlockSpec((tm, tn), lambda i, j, k: (i, j)), (tm, tn),
        (S // tm, D // tn, n), n)


def _rowwise(name, fn, tiled, params, outs, reds=(), ts=256):
    S = tiled[0][0].shape[0]
    ts = _pick(S, (ts, 128))
    n_t, n_p, n_o = len(tiled), len(params), len(outs)

    def body(*refs):
        vals = [r[...] for r in refs[:n_t + n_p]]
        res_o, res_r = fn(*vals)
        o_refs = refs[n_t + n_p:n_t + n_p + n_o]
        r_refs = refs[n_t + n_p + n_o:]
        for r, val in zip(o_refs, res_o, strict=True):
            r[...] = val.astype(r.dtype)
        if r_refs:
            i = pl.program_id(0)

            @pl.when(i == 0)
            def _():
                for r, val in zip(r_refs, res_r, strict=True):
                    r[...] = val.astype(F32)

            @pl.when(i > 0)
            def _():
                for r, val in zip(r_refs, res_r, strict=True):
                    r[...] += val.astype(F32)

    in_specs = [pl.BlockSpec((ts, w), functools.partial(lambda i, cb: (i, cb), cb=cb)) for _, cb, w in tiled]
    in_specs += [pl.BlockSpec(p.shape, lambda i: (0, 0)) for p in params]
    out_specs = [pl.BlockSpec((ts, w), lambda i: (i, 0)) for w, _ in outs]
    out_specs += [pl.BlockSpec(tuple(r), lambda i: (0, 0)) for r in reds]
    out_shape = [jax.ShapeDtypeStruct((S, w), dt) for w, dt in outs]
    out_shape += [jax.ShapeDtypeStruct(tuple(r), F32) for r in reds]
    res = pl.pallas_call(
        body, name=name, grid=(S // ts,), in_specs=in_specs, out_specs=out_specs, out_shape=out_shape,
        compiler_params=pltpu.CompilerParams(dimension_semantics=("arbitrary",)),
    )(*[t[0] for t in tiled], *params)
    return res[:n_o], res[n_o:]


def _ln(r, g, b):
    mu = jnp.mean(r, axis=-1, keepdims=True)
    xc = r - mu
    var = jnp.mean(xc * xc, axis=-1, keepdims=True)
    return xc * lax.rsqrt(var + LN_EPS) * g + b


def _ln_bwd(r, g, b, dy):
    _, vjp = jax.vjp(_ln, r, g, b)
    return vjp(dy)


def _shift_down(x, d):
    return x if d == 0 else pltpu.roll(x, d, 0)


def _conv_fwd(name, x, w):
    S = x.shape[0]
    ts = _pick(S, (256, 128))
    hb = ts // HALO

    def body(x_ref, prev_ref, w_ref, o_ref):
        i = pl.program_id(0)
        prev = jnp.where(i > 0, prev_ref[...], 0.0)
        xe = jnp.concatenate([prev, x_ref[...]], axis=0)
        y = jnp.zeros((ts + HALO, CONV_CH), F32)
        for j in range(CONV_K):
            y = y + w_ref[pl.ds(j, 1), :] * _shift_down(xe, CONV_K - 1 - j)
        y = y[HALO:, :]
        o_ref[...] = y * _sigmoid(y)

    return pl.pallas_call(
        body, name=name, grid=(S // ts,),
        in_specs=[pl.BlockSpec((ts, CONV_CH), lambda i: (i, 0)),
                  pl.BlockSpec((HALO, CONV_CH), lambda i: (jnp.maximum(i * hb - 1, 0), 0)),
                  pl.BlockSpec((CONV_K, CONV_CH), lambda i: (0, 0))],
        out_specs=pl.BlockSpec((ts, CONV_CH), lambda i: (i, 0)),
        out_shape=jax.ShapeDtypeStruct((S, CONV_CH), F32),
        compiler_params=pltpu.CompilerParams(dimension_semantics=("arbitrary",)),
    )(x, x, w)


def _conv_bwd(name, x, w, dout):
    S = x.shape[0]
    ts = _pick(S, (256, 128))
    hb = ts // HALO
    nt = S // ts
    n_ext = ts + 2 * HALO

    def body(x_ref, prev_ref, next_ref, w_ref, d_ref, dnext_ref, dx_ref, dw_ref):
        i = pl.program_id(0)
        prev = jnp.where(i > 0, prev_ref[...], 0.0)
        last = i == nt - 1
        nxt = jnp.where(last, 0.0, next_ref[...])
        dnxt = jnp.where(last, 0.0, dnext_ref[...])
        xe = jnp.concatenate([prev, x_ref[...], nxt], axis=0)
        de = jnp.concatenate([jnp.zeros((HALO, CONV_CH), F32), d_ref[...], dnxt], axis=0)
        y = jnp.zeros((n_ext, CONV_CH), F32)
        for j in range(CONV_K):
            y = y + w_ref[pl.ds(j, 1), :] * _shift_down(xe, CONV_K - 1 - j)
        sg = _sigmoid(y)
        dy = de * (sg * (1.0 + y * (1.0 - sg)))
        dx = jnp.zeros((n_ext, CONV_CH), F32)
        for j in range(CONV_K):
            m = CONV_K - 1 - j
            dx = dx + w_ref[pl.ds(j, 1), :] * (dy if m == 0 else pltpu.roll(dy, n_ext - m, 0))
        dx_ref[...] = dx[HALO:HALO + ts, :].astype(dx_ref.dtype)
        row = lax.broadcasted_iota(jnp.int32, (n_ext, 1), 0)
        dy_own = jnp.where((row >= HALO) & (row < HALO + ts), dy, 0.0)
        parts = [jnp.sum(dy_own * _shift_down(xe, CONV_K - 1 - j), axis=0, keepdims=True) for j in range(CONV_K)]
        dw = jnp.concatenate(parts + [jnp.zeros((HALO - CONV_K, CONV_CH), F32)], axis=0)

        @pl.when(i == 0)
        def _():
            dw_ref[...] = dw

        @pl.when(i > 0)
        def _():
            dw_ref[...] += dw

    tile = pl.BlockSpec((ts, CONV_CH), lambda i: (i, 0))
    prev = pl.BlockSpec((HALO, CONV_CH), lambda i: (jnp.maximum(i * hb - 1, 0), 0))
    nxt = pl.BlockSpec((HALO, CONV_CH), lambda i: (jnp.minimum((i + 1) * hb, nt * hb - 1), 0))
    return pl.pallas_call(
        body, name=name, grid=(nt,),
        in_specs=[tile, prev, nxt, pl.BlockSpec((CONV_K, CONV_CH), lambda i: (0, 0)), tile, nxt],
        out_specs=[tile, pl.BlockSpec((HALO, CONV_CH), lambda i: (0, 0))],
        out_shape=[jax.ShapeDtypeStruct((S, CONV_CH), BF16), jax.ShapeDtypeStruct((HALO, CONV_CH), F32)],
        compiler_params=pltpu.CompilerParams(dimension_semantics=("arbitrary",)),
    )(x, x, x, w, dout, dout)


def _softplus(z):
    return jnp.maximum(z, 0.0) + jnp.log(1.0 + jnp.exp(-jnp.abs(z)))


def _dot2(x, tri2):
    hi = x.astype(BF16)
    lo = (x - hi.astype(F32)).astype(BF16)
    return jnp.dot(jnp.concatenate([hi, lo], axis=1), tri2, preferred_element_type=F32)


def _pair_tri(tk, keep):
    r = lax.broadcasted_iota(jnp.int32, (4 * tk, 2 * tk), 0)
    c = lax.broadcasted_iota(jnp.int32, (4 * tk, 2 * tk), 1)
    same_head = ((r // tk) % 2) == (c // tk)
    return (same_head & keep(r % tk, c % tk)).astype(BF16)


def _pair_rows(x2, first):
    return jnp.concatenate([jnp.where(first, x2, 0), jnp.where(first, 0, x2)], axis=0)


def _pair_sum(x):
    h = x.shape[1] // 2
    return jnp.sum(x[:, :h], axis=1, keepdims=True), jnp.sum(x[:, h:], axis=1, keepdims=True)


def _sb_tiles(S):
    tq = _pick(S, (SB_TQ, 256, 128))
    return tq, SB_TK


N_PAIRS = N_HEADS // 2


def _sb_specs(S, tq):
    q = pl.BlockSpec((tq, 128), lambda p, i: (i, p))
    k = pl.BlockSpec((S, 128), lambda p, i: (0, N_PAIRS + p))
    v = pl.BlockSpec((S, 128), lambda p, i: (0, 2 * N_PAIRS + p))
    return q, k, v


def _scaled(q):
    return (q.astype(F32) * (HEAD_DIM ** -0.5)).astype(BF16)


def _sb_fwd(qkv):
    S = qkv.shape[0]
    tq, tk = _sb_tiles(S)
    nd = tq // tk

    def body(q_ref, k_ref, v_ref, o_ref, ta_ref, tb_ref, cnt_ref):
        qi = pl.program_id(1)
        qs = _scaled(q_ref[...])
        row = qi * tq + lax.broadcasted_iota(jnp.int32, (tq, 2 * tk), 0)
        col = lax.broadcasted_iota(jnp.int32, (tq, 2 * tk), 1)
        left = col < tk
        col = col % tk
        later = _pair_tri(tk, lambda r, c: r > c)
        first = lax.broadcasted_iota(jnp.int32, (tk, 128), 1) < HEAD_DIM

        def tile(j, carry, masked):
            acc, suf_a, suf_b = carry
            off = pl.multiple_of(j * tk, tk)
            kc = _pair_rows(k_ref[pl.ds(off, tk), :], first)
            vc = _pair_rows(v_ref[pl.ds(off, tk), :], first)
            mask = (col + j * tk) < row
            z = lax.dot_general(qs, kc, _NT, preferred_element_type=F32)
            lf = -_softplus(z)
            if masked:
                lf = jnp.where(mask, lf, 0.0)
            w = jnp.exp(z + lf + _dot2(lf, later) + jnp.where(left, suf_a, suf_b))
            if masked:
                w = jnp.where(mask, w, 0.0)
            acc = acc + jnp.dot(w.astype(BF16), vc, preferred_element_type=F32)
            sum_a, sum_b = _pair_sum(lf)
            return acc, suf_a + sum_a, suf_b + sum_b

        zero1 = jnp.zeros((tq, 1), F32)
        carry = (jnp.zeros((tq, 128), F32), zero1, zero1)
        n_full = qi * nd
        acc, suf_a, suf_b = lax.fori_loop(0, nd, lambda d, c: tile(n_full + nd - 1 - d, c, True), carry)

        def alive(sa, sb):
            return jnp.maximum(jnp.max(sa), jnp.max(sb)) > SB_CUTOFF

        def step(c):
            jj, acc, sa, sb, _ = c
            acc, sa, sb = tile(n_full - 1 - jj, (acc, sa, sb), False)
            return jj + 1, acc, sa, sb, alive(sa, sb)

        n_used, acc, suf_a, suf_b, _ = lax.while_loop(
            lambda c: (c[0] < n_full) & c[4], step, (0, acc, suf_a, suf_b, alive(suf_a, suf_b)))
        o_ref[...] = acc.astype(o_ref.dtype)
        ta_ref[...] = suf_a
        tb_ref[...] = suf_b
        cnt_ref[pl.program_id(0), qi] = n_used

    tot = pl.BlockSpec((None, tq, 1), lambda p, i: (p, i, 0))
    tshape = jax.ShapeDtypeStruct((N_PAIRS, S, 1), F32)
    return pl.pallas_call(
        body, name="sb_fwd", grid=(N_PAIRS, S // tq), in_specs=list(_sb_specs(S, tq)),
        out_specs=[pl.BlockSpec((tq, 128), lambda p, i: (i, p)), tot, tot, pl.BlockSpec(memory_space=pltpu.SMEM)],
        out_shape=[jax.ShapeDtypeStruct((S, HEAD_W), BF16), tshape, tshape,
                   jax.ShapeDtypeStruct((N_PAIRS, S // tq), jnp.int32)],
        compiler_params=pltpu.CompilerParams(dimension_semantics=("arbitrary", "arbitrary")),
    )(qkv, qkv, qkv)


def _sb_bwd(qkv, tot_a, tot_b, visited, do):
    S = qkv.shape[0]
    tq, tk = _sb_tiles(S)
    nd = tq // tk
    scale = HEAD_DIM ** -0.5

    def body(q_ref, k_ref, v_ref, ta_ref, tb_ref, cnt_ref, do_ref, dq_ref, dk_ref, dv_ref):
        qi = pl.program_id(1)

        @pl.when(qi == 0)
        def _():
            dk_ref[...] = jnp.zeros_like(dk_ref)
            dv_ref[...] = jnp.zeros_like(dv_ref)

        qs = _scaled(q_ref[...])
        do2 = do_ref[...]
        row = qi * tq + lax.broadcasted_iota(jnp.int32, (tq, 2 * tk), 0)
        col = lax.broadcasted_iota(jnp.int32, (tq, 2 * tk), 1)
        left = col < tk
        col = col % tk
        total = jnp.where(left, ta_ref[...], tb_ref[...])
        upto = _pair_tri(tk, lambda r, c: r <= c)
        first = lax.broadcasted_iota(jnp.int32, (tk, 128), 1) < HEAD_DIM

        def tile(j, carry, masked):
            dq, pl_a, pl_b, pg_a, pg_b = carry
            off = pl.multiple_of(j * tk, tk)
            kc = _pair_rows(k_ref[pl.ds(off, tk), :], first)
            vc = _pair_rows(v_ref[pl.ds(off, tk), :], first)
            mask = (col + j * tk) < row
            z = lax.dot_general(qs, kc, _NT, preferred_element_type=F32)
            lf = -_softplus(z)
            sig = jnp.exp(z + lf)
            if masked:
                lf = jnp.where(mask, lf, 0.0)
            w = jnp.exp(z + lf + (total - (jnp.where(left, pl_a, pl_b) + _dot2(lf, upto))))
            if masked:
                w = jnp.where(mask, w, 0.0)
            gl = lax.dot_general(do2, vc, _NT, preferred_element_type=F32) * w
            dz = gl - sig * (jnp.where(left, pg_a, pg_b) + _dot2(gl, upto))
            if masked:
                dz = jnp.where(mask, dz, 0.0)
            dzb = dz.astype(BF16)
            dq = dq + jnp.dot(dzb, kc, preferred_element_type=F32)
            dkc = lax.dot_general(dzb, qs, _TN, preferred_element_type=F32)
            dvc = lax.dot_general(w.astype(BF16), do2, _TN, preferred_element_type=F32)
            dk_ref[pl.ds(off, tk), :] += jnp.where(first, dkc[:tk], dkc[tk:])
            dv_ref[pl.ds(off, tk), :] += jnp.where(first, dvc[:tk], dvc[tk:])
            sl_a, sl_b = _pair_sum(lf)
            sg_a, sg_b = _pair_sum(gl)
            return dq, pl_a + sl_a, pl_b + sl_b, pg_a + sg_a, pg_b + sg_b

        zero1 = jnp.zeros((tq, 1), F32)
        carry = (jnp.zeros((tq, 128), F32), zero1, zero1, zero1, zero1)
        n_full = qi * nd
        n_used = cnt_ref[pl.program_id(0), qi]
        first_tile = n_full - n_used
        carry = lax.fori_loop(0, n_used, lambda t, c: tile(first_tile + t, c, False), carry)
        dq = lax.fori_loop(0, nd, lambda d, c: tile(n_full + d, c, True), carry)[0]
        dq_ref[...] = dq * scale

    q_spec, k_spec, v_spec = _sb_specs(S, tq)
    tot = pl.BlockSpec((None, tq, 1), lambda p, i: (p, i, 0))
    whole = pl.BlockSpec((S, 128), lambda p, i: (0, p))
    big = jax.ShapeDtypeStruct((S, HEAD_W), F32)
    return pl.pallas_call(
        body, name="sb_bwd", grid=(N_PAIRS, S // tq),
        in_specs=[q_spec, k_spec, v_spec, tot, tot, pl.BlockSpec(memory_space=pltpu.SMEM), q_spec],
        out_specs=[q_spec, whole, whole], out_shape=[big, big, big],
        compiler_params=pltpu.CompilerParams(dimension_semantics=("arbitrary", "arbitrary")),
    )(qkv, qkv, qkv, tot_a, tot_b, visited, do)


GDN_PRECISION = lax.Precision.HIGH


def _bmm(a, b):
    return lax.dot_general(a, b, (((2,), (1,)), ((0,), (0,))), precision=GDN_PRECISION, preferred_element_type=F32)


def _bmm_nt(a, b):
    return lax.dot_general(a, b, (((2,), (2,)), ((0,), (0,))), precision=GDN_PRECISION, preferred_element_type=F32)


def _bmm_tn(a, b):
    return lax.dot_general(a, b, (((1,), (1,)), ((0,), (0,))), precision=GDN_PRECISION, preferred_element_type=F32)


def _tri_inv(lower):
    C = lower.shape[-1]
    ii = lax.broadcasted_iota(jnp.int32, (C, C), 0)
    jj = lax.broadcasted_iota(jnp.int32, (C, C), 1)
    eye = (ii == jj).astype(F32)[None]
    xd = jnp.where((ii // 8 == jj // 8)[None], -lower, 0.0)
    x2 = _bmm(xd, xd)
    x4 = _bmm(x2, x2)
    inv = eye + xd
    inv = inv + _bmm(inv, x2)
    inv = inv + _bmm(inv, x4)
    b = 8
    while b < C:
        off = jnp.where(((ii // (2 * b) == jj // (2 * b)) & (ii // b != jj // b))[None], lower, 0.0)
        inv = inv - _bmm(inv, _bmm(off, inv))
        b *= 2
    return inv


@jax.custom_vjp
def _tri_solve(lower, inv, rhs):
    return _bmm(inv, rhs)


def _tri_solve_fwd(lower, inv, rhs):
    sol = _bmm(inv, rhs)
    return sol, (inv, sol)


def _tri_solve_bwd(res, dsol):
    inv, sol = res
    drhs = _bmm_tn(inv, dsol)
    C = inv.shape[-1]
    ii = lax.broadcasted_iota(jnp.int32, (C, C), 0)
    jj = lax.broadcasted_iota(jnp.int32, (C, C), 1)
    return jnp.where((jj < ii)[None], -_bmm_nt(drhs, sol), 0.0), jnp.zeros_like(inv), drhs


_tri_solve.defvjp(_tri_solve_fwd, _tri_solve_bwd)


def _pairs(x):
    return jnp.stack([x[:, 128 * p:128 * (p + 1)] for p in range(N_PAIRS)], axis=0)


def _unpairs(x):
    return jnp.concatenate([x[p] for p in range(N_PAIRS)], axis=1)


def _gdn_chunk(state, qkv, gz, ba, a_log_x, dt_x, norm_x, inv=None):
    C = qkv.shape[0]
    lane = lax.broadcasted_iota(jnp.int32, (1, 1, 128), 2)
    first = lane < HEAD_DIM

    def split_heads(x2):
        return jnp.stack([jnp.where(first, x2, 0.0), jnp.where(first, 0.0, x2)], axis=1).reshape(N_HEADS, *x2.shape[1:])

    def merge_heads(xh):
        x = xh.reshape(N_PAIRS, 2, *xh.shape[1:])
        return x[:, 0] + x[:, 1]

    def head_cols(x2):
        a = jnp.sum(jnp.where(lane == 0, x2, 0.0), axis=-1, keepdims=True)
        b = jnp.sum(jnp.where(lane == HEAD_DIM, x2, 0.0), axis=-1, keepdims=True)
        return jnp.stack([a, b], axis=1).reshape(N_HEADS, *a.shape[1:])

    def to_pair(xh):
        x = xh.reshape(N_PAIRS, 2, *xh.shape[1:])
        return jnp.where(first, x[:, 0], x[:, 1])

    def head_sums(x2):
        a = jnp.sum(jnp.where(first, x2, 0.0), axis=-1, keepdims=True)
        b = jnp.sum(jnp.where(first, 0.0, x2), axis=-1, keepdims=True)
        return jnp.where(first, a, b)

    er = lax.broadcasted_iota(jnp.int32, (128, 2 * HEAD_W), 0)
    ec = lax.broadcasted_iota(jnp.int32, (128, 2 * HEAD_W), 1)
    spread = (er == ec // HEAD_DIM).astype(F32)
    bx = lax.dot_general(ba, spread, (((1,), (0,)), ((), ())), precision=HI, preferred_element_type=F32)
    beta2 = _pairs(_sigmoid(bx[:, :HEAD_W]))
    g2 = _pairs(-jnp.exp(a_log_x) * _softplus(bx[:, HEAD_W:] + dt_x))
    beta = head_cols(beta2)
    g = head_cols(g2)

    q2, k2, v2 = (_pairs(qkv[:, i * HEAD_W:(i + 1) * HEAD_W]) for i in range(3))
    qn2 = q2 * lax.rsqrt(head_sums(q2 * q2) + RMS_EPS) * (HEAD_DIM ** -0.5)
    kn2 = k2 * lax.rsqrt(head_sums(k2 * k2) + RMS_EPS)
    knh = split_heads(kn2)

    ii = lax.broadcasted_iota(jnp.int32, (C, C), 0)[None]
    jj = lax.broadcasted_iota(jnp.int32, (C, C), 1)[None]
    incl = jj <= ii
    g_row = jnp.sum(jnp.where(ii == jj, g, 0.0), axis=1, keepdims=True)
    gc_col = jnp.sum(jnp.where(incl, g_row, 0.0), axis=2, keepdims=True)
    gc_row = jnp.sum(jnp.where(ii <= jj, g, 0.0), axis=1, keepdims=True)
    decay = jnp.where(incl, jnp.exp(jnp.where(incl, gc_col - gc_row, 0.0)), 0.0)
    lower = jnp.where(jj < ii, beta * _bmm_nt(knh, knh) * decay, 0.0)
    gc2 = to_pair(gc_col)
    egc2 = jnp.exp(gc2)
    if inv is None:
        inv = _tri_inv(lower)
    u2 = merge_heads(_tri_solve(lower, inv, split_heads(v2 * beta2)))
    w2 = merge_heads(_tri_solve(lower, inv, split_heads(kn2 * (beta2 * egc2))))
    qk = jnp.where(incl, _bmm_nt(jnp.repeat(qn2, 2, axis=0), knh) * decay, 0.0)
    g_last2 = to_pair(jnp.sum(g, axis=1, keepdims=True))
    v_new2 = u2 - _bmm(w2, state)
    o2 = _bmm(qn2 * egc2, state) + merge_heads(_bmm(qk, split_heads(v_new2)))
    sr = lax.broadcasted_iota(jnp.int32, (128, 128), 0)
    sc = lax.broadcasted_iota(jnp.int32, (128, 128), 1)
    same_head = ((sr < HEAD_DIM) == (sc < HEAD_DIM))[None]
    new_state = state * jnp.exp(g_last2) + jnp.where(same_head, _bmm_tn(kn2 * jnp.exp(g_last2 - gc2), v_new2), 0.0)
    o2 = o2 * lax.rsqrt(head_sums(o2 * o2) * (1.0 / HEAD_DIM) + RMS_EPS) * _pairs(norm_x)
    gz2 = _pairs(gz)
    return new_state, _unpairs(o2 * (gz2 * _sigmoid(gz2))), inv


def _gdn_specs(order):
    C = GDN_CHUNK
    par = pl.BlockSpec((1, HEAD_W), lambda c: (0, 0))
    return [pl.BlockSpec((C, CONV_CH), lambda c: (order(c), 0)), pl.BlockSpec((C, HEAD_W), lambda c: (order(c), 3)),
            pl.BlockSpec((C, 128), lambda c: (order(c), 0)), par, par, par]


def _gdn_fwd(qkv, gdnp, ba, a_log_x, dt_x, norm_x):
    S = qkv.shape[0]
    C = GDN_CHUNK
    n = S // C

    def body(qkv_ref, gz_ref, ba_ref, al_ref, dt_ref, nw_ref, o_ref, st_ref, inv_ref, state):
        @pl.when(pl.program_id(0) == 0)
        def _():
            state[...] = jnp.zeros_like(state)

        st = state[...]
        st_ref[...] = st
        new, o, inv = _gdn_chunk(st, qkv_ref[...], gz_ref[...], ba_ref[...], al_ref[...], dt_ref[...], nw_ref[...])
        state[...] = new
        o_ref[...] = o.astype(o_ref.dtype)
        inv_ref[...] = inv

    return pl.pallas_call(
        body, name="gdn_fwd", grid=(n,), in_specs=_gdn_specs(lambda c: c),
        out_specs=[pl.BlockSpec((C, HEAD_W), lambda c: (c, 0)),
                   pl.BlockSpec((None, N_PAIRS, 128, 128), lambda c: (c, 0, 0, 0)),
                   pl.BlockSpec((None, N_HEADS, C, C), lambda c: (c, 0, 0, 0))],
        out_shape=[jax.ShapeDtypeStruct((S, HEAD_W), BF16), jax.ShapeDtypeStruct((n, N_PAIRS, 128, 128), F32),
                   jax.ShapeDtypeStruct((n, N_HEADS, C, C), F32)],
        scratch_shapes=[pltpu.VMEM((N_PAIRS, 128, 128), F32)],
        compiler_params=pltpu.CompilerParams(dimension_semantics=("arbitrary",)),
    )(qkv, gdnp, ba, a_log_x, dt_x, norm_x)


def _gdn_bwd(qkv, gdnp, ba, a_log_x, dt_x, norm_x, states, invs, do):
    S = qkv.shape[0]
    C = GDN_CHUNK
    n = S // C

    def body(qkv_ref, gz_ref, ba_ref, al_ref, dt_ref, nw_ref, st_ref, inv_ref, do_ref,
             dqkv_ref, dgz_ref, dba_ref, dal_ref, ddt_ref, dnw_ref, dstate):
        @pl.when(pl.program_id(0) == 0)
        def _():
            dstate[...] = jnp.zeros_like(dstate)
            dal_ref[...] = jnp.zeros_like(dal_ref)
            ddt_ref[...] = jnp.zeros_like(ddt_ref)
            dnw_ref[...] = jnp.zeros_like(dnw_ref)

        args = (st_ref[...], qkv_ref[...], gz_ref[...], ba_ref[...], al_ref[...], dt_ref[...], nw_ref[...])
        inv = inv_ref[...]
        _, vjp = jax.vjp(lambda *a: _gdn_chunk(*a, inv=inv)[:2], *args)
        dst, dqkv, dgz, dba, dal, ddt, dnw = vjp((dstate[...], do_ref[...]))
        dstate[...] = dst
        dqkv_ref[...] = dqkv
        dgz_ref[...] = dgz.astype(dgz_ref.dtype)
        dba_ref[...] = dba
        dal_ref[...] += dal
        ddt_ref[...] += ddt
        dnw_ref[...] += dnw

    rev = lambda c: n - 1 - c
    par = pl.BlockSpec((1, HEAD_W), lambda c: (0, 0))
    f = jax.ShapeDtypeStruct
    return pl.pallas_call(
        body, name="gdn_bwd", grid=(n,),
        in_specs=_gdn_specs(rev) + [pl.BlockSpec((None, N_PAIRS, 128, 128), lambda c: (rev(c), 0, 0, 0)),
                                    pl.BlockSpec((None, N_HEADS, C, C), lambda c: (rev(c), 0, 0, 0)),
                                    pl.BlockSpec((C, HEAD_W), lambda c: (rev(c), 0))],
        out_specs=[pl.BlockSpec((C, CONV_CH), lambda c: (rev(c), 0)), pl.BlockSpec((C, HEAD_W), lambda c: (rev(c), 0)),
                   pl.BlockSpec((C, 128), lambda c: (rev(c), 0)), par, par, par],
        out_shape=[f((S, CONV_CH), F32), f((S, HEAD_W), BF16), f((S, 128), F32)] + [f((1, HEAD_W), F32)] * 3,
        scratch_shapes=[pltpu.VMEM((N_PAIRS, 128, 128), F32)],
        compiler_params=pltpu.CompilerParams(dimension_semantics=("arbitrary",)),
    )(qkv, gdnp, ba, a_log_x, dt_x, norm_x, states, invs, do)


def _local_step(x, p, target, W, P):
    S = x.shape[0]
    D = D_MODEL
    xb = x.astype(BF16)
    pb = p.astype(BF16)

    def ffn_fwd(tag, h, hb, w_in, w_out, g, b):
        a, gu = _ffn_up(f"{tag}_up", hb, w_in)
        f = _ffn_down(f"{tag}_down", a, w_out)

        def fn(h, f, g, b):
            r = ALPHA * h + 0.5 * f
            y = _ln(r, g, b)
            return (r, y, y), ()

        (r, y, yb), _ = _rowwise(f"{tag}_ln", fn, [(h, 0, D), (f, 0, D)], [g, b], [(D, F32), (D, F32), (D, BF16)])
        return a, gu, r, y, yb

    a1, gu1, r1, h1, h1b = ffn_fwd("ffn1", x, xb, W["ffn1_w_in"], W["ffn1_w_out"], P["ln1_g"], P["ln1_b"])

    wmix = W["w_mix_pad"]
    sbp = _mm("mix_sb", h1b, wmix, BF16, n=3 * HEAD_W, col_off=MIX_SB)
    gdnp = _mm("mix_gdn", h1b, wmix, F32, n=4 * HEAD_W, col_off=MIX_GDN)
    gates = _mm("mix_gates", h1b, wmix, F32, n=2 * D, col_off=MIX_GATES)
    ba = _mm("mix_ba", h1b, wmix, F32, n=128, col_off=MIX_BA)

    attb, sb_tot_a, sb_tot_b, sb_visited = _sb_fwd(sbp)
    y_sb = _mm("sb_out", attb, W["w_branch_sb"], F32)

    conv_w = P["conv_w"]
    qkv = _conv_fwd("conv_fwd", gdnp, conv_w)
    a_log_x = jnp.repeat(P["a_log"], HEAD_DIM, axis=1)
    dt_x = jnp.repeat(P["dt_bias"], HEAD_DIM, axis=1)
    norm_x = jnp.tile(P["gdn_norm_w"], (1, N_HEADS))
    gob, states, invs = _gdn_fwd(qkv, gdnp, ba, a_log_x, dt_x, norm_x)
    y_gdn = _mm("gdn_out", gob, W["w_branch_gdn"], F32)

    def merge_fn(gs, gg, ys, yg, bs, bg):
        return ((_sigmoid(gs + bs) * ys + _sigmoid(gg + bg) * yg),), ()

    b_gate = P["b_gate"]
    bs, bg = b_gate[:, :D], b_gate[:, D:]
    (merged,), _ = _rowwise("mix_merge", merge_fn, [(gates, 0, D), (gates, 1, D), (y_sb, 0, D), (y_gdn, 0, D)], [bs, bg],
                            [(D, BF16)])
    mix = _mm("mix_out", merged, W["w_mix_out"], F32)

    def ln2_fn(h, f, g, b):
        r = ALPHA * h + f
        y = _ln(r, g, b)
        return (r, y, y), ()

    (r2, h2, h2b), _ = _rowwise("mix_ln", ln2_fn, [(h1, 0, D), (mix, 0, D)], [P["ln2_g"], P["ln2_b"]],
                                [(D, F32), (D, F32), (D, BF16)])

    a2, gu2, r3, h3, h3b = ffn_fwd("ffn2", h2, h2b, W["ffn2_w_in"], W["ffn2_w_out"], P["ln3_g"], P["ln3_b"])

    zg = _mm("ple_gate", h3b, W["w_ple_gate"], F32)
    pp = _mm("ple_proj", pb, W["w_ple"], F32)

    def ple(h, zg, pp, bp, g, b):
        return _ln(ALPHA * h + _sigmoid(zg + bp) * pp, g, b)

    def head_fn(h, zg, pp, tgt, bp, g, b):
        y, vjp = jax.vjp(ple, h, zg, pp, bp, g, b)
        err = y - tgt
        dh, dzg, dpp, dbp, dg, db = vjp(err * (1.0 / D))
        loss = 0.5 * jnp.sum(jnp.sum(err * err, axis=1, keepdims=True), axis=0, keepdims=True) * (1.0 / D)
        return (dh, dzg, dpp), (loss, dbp, dg, db)

    (dh3_a, dzg, dpp), (loss, d_bple, d_ln4g, d_ln4b) = _rowwise(
        "ple_head", head_fn, [(h3, 0, D), (zg, 0, D), (pp, 0, D), (target, 0, D)],
        [P["b_ple_gate"], P["ln4_g"], P["ln4_b"]], [(D, F32), (D, BF16), (D, BF16)],
        [(1, 1), (1, D), (1, D), (1, D)])

    grads, small = {}, {"b_ple_gate": d_bple, "ln4_g": d_ln4g, "ln4_b": d_ln4b}
    grads["w_ple_gate"] = _mm_tn("d_w_ple_gate", h3b, dzg, BF16)
    grads["w_ple"] = _mm_tn("d_w_ple", pb, dpp, BF16)
    dh3_b = _mm_nt("d_ple_gate_in", dzg, W["w_ple_gate"], F32)

    def ffn_bwd(tag, dy_parts, r, g, b, a, gu, hb_in, w_in, w_out):
        n_parts = len(dy_parts)

        def fn(*vals):
            dy = vals[0]
            for extra in vals[1:n_parts]:
                dy = dy + extra
            r, g, b = vals[n_parts:]
            dr, dg, db = _ln_bwd(r, g, b, dy)
            return (ALPHA * dr, 0.5 * dr), (dg, db)

        (dh_res, dfb), (dg, db) = _rowwise(f"{tag}_ln_bwd", fn, [(t, 0, D) for t in dy_parts] + [(r, 0, D)], [g, b],
                                           [(D, F32), (D, BF16)], [(1, D), (1, D)])
        dgu = _ffn_bwd_act(f"{tag}_act_bwd", dfb, w_out, gu)
        dgu8 = dgu.reshape(N_DEV, S, SHARD_FF)
        d_w_out = _ffn_d_w_out(f"d_{tag}_w_out", a, dfb)
        d_w_in = _ffn_d_w_in(f"d_{tag}_w_in", hb_in, dgu8)
        dh_ffn = _ffn_in_bwd(f"{tag}_in_bwd", dgu8, w_in)
        return dh_res, dh_ffn, d_w_in, d_w_out, dg, db

    dh2_a, dh2_b, grads["ffn2_w_in"], grads["ffn2_w_out"], small["ln3_g"], small["ln3_b"] = ffn_bwd(
        "ffn2", [dh3_a, dh3_b], r3, P["ln3_g"], P["ln3_b"], a2, gu2, h2b, W["ffn2_w_in"], W["ffn2_w_out"])

    def ln2_bwd_fn(d1, d2, r, g, b):
        dr, dg, db = _ln_bwd(r, g, b, d1 + d2)
        return (ALPHA * dr, dr), (dg, db)

    (dh1_a, dmixb), (small["ln2_g"], small["ln2_b"]) = _rowwise(
        "mix_ln_bwd", ln2_bwd_fn, [(dh2_a, 0, D), (dh2_b, 0, D), (r2, 0, D)], [P["ln2_g"], P["ln2_b"]],
        [(D, F32), (D, BF16)], [(1, D), (1, D)])
    grads["w_mix_out"] = _mm_tn("d_w_mix_out", merged, dmixb, BF16)
    dmerged = _mm_nt("mix_out_bwd", dmixb, W["w_mix_out"], F32)

    def merge_bwd_fn(dm, gs, gg, ys, yg, bs, bg):
        ss, sg = _sigmoid(gs + bs), _sigmoid(gg + bg)
        dgs = dm * ys * ss * (1.0 - ss)
        dgg = dm * yg * sg * (1.0 - sg)
        return (dgs, dgg, dm * ss, dm * sg), (jnp.sum(dgs, axis=0, keepdims=True), jnp.sum(dgg, axis=0, keepdims=True))

    (dgs, dgg, dy_sb, dy_gdn), (d_bs, d_bg) = _rowwise(
        "mix_merge_bwd", merge_bwd_fn, [(dmerged, 0, D), (gates, 0, D), (gates, 1, D), (y_sb, 0, D), (y_gdn, 0, D)],
        [bs, bg], [(D, BF16)] * 4, [(1, D), (1, D)])
    small["b_gate"] = jnp.concatenate([d_bs, d_bg], axis=1)

    grads["w_branch_sb"] = _mm_tn("d_w_branch_sb", attb, dy_sb, BF16)
    datt = _mm_nt("sb_out_bwd", dy_sb, W["w_branch_sb"], BF16)
    dsq, dsk, dsv = _sb_bwd(sbp, sb_tot_a, sb_tot_b, sb_visited, datt)

    grads["w_branch_gdn"] = _mm_tn("d_w_branch_gdn", gob, dy_gdn, BF16)
    dgo = _mm_nt("gdn_out_bwd", dy_gdn, W["w_branch_gdn"], F32)
    dqkv, dgz, dba, d_alog_x, d_dt_x, d_norm_x = _gdn_bwd(qkv, gdnp, ba, a_log_x, dt_x, norm_x, states, invs, dgo)
    small["a_log"] = jnp.sum(d_alog_x.reshape(N_HEADS, HEAD_DIM), axis=1).reshape(1, N_HEADS)
    small["dt_bias"] = jnp.sum(d_dt_x.reshape(N_HEADS, HEAD_DIM), axis=1).reshape(1, N_HEADS)
    small["gdn_norm_w"] = jnp.sum(d_norm_x.reshape(N_HEADS, HEAD_DIM), axis=0).reshape(1, HEAD_DIM)
    dconv_in, d_conv_w = _conv_bwd("conv_bwd", gdnp, conv_w, dqkv)
    grads["conv_w"] = d_conv_w[:CONV_K]

    dproj = jnp.concatenate([dsq.astype(BF16), dsk.astype(BF16), dsv.astype(BF16), dconv_in, dgz, dgs, dgg,
                             dba.astype(BF16)], axis=1)
    d_wmix = _mm_tn("d_w_mix_in", h1b, dproj, BF16)
    grads["w_mix_in"] = jnp.concatenate([d_wmix[:, :MIX_GATES], d_wmix[:, MIX_BA:MIX_BA + 2 * N_HEADS],
                                         d_wmix[:, MIX_GATES:MIX_BA]], axis=1)
    dh1_b = _mm_nt("mix_in_bwd", dproj, wmix, F32)

    dx_a, dx_b, grads["ffn1_w_in"], grads["ffn1_w_out"], small["ln1_g"], small["ln1_b"] = ffn_bwd(
        "ffn1", [dh1_a, dh1_b], r1, P["ln1_g"], P["ln1_b"], a1, gu1, xb, W["ffn1_w_in"], W["ffn1_w_out"])

    (grad_x,), _ = _rowwise("grad_x", lambda a, b: ((a + b,), ()), [(dx_a, 0, D), (dx_b, 0, D)], [], [(D, F32)])
    return loss[0, 0], grad_x, grads, small


def _coords():
    return lax.axis_index("x"), lax.axis_index("y"), lax.axis_index("c")


def _all_gather(name, shards):
    n = len(shards)

    def body(*refs):
        x_refs, out_refs = refs[:n], refs[n:2 * n]
        send_sems, recv_sems, local_sems = refs[2 * n:]
        x, y, c = _coords()
        me, sibling = (x, y, c), (x, y, 1 - c)
        chips = [(1 - x, y), (x, 1 - y), (1 - x, 1 - y)]

        def copy(a, k, block, to, src=None):
            px, py, pc = block
            dst = out_refs[a].at[4 * px + 2 * py + pc]
            return pltpu.make_async_remote_copy(
                src_ref=dst if src is None else src, dst_ref=dst,
                send_sem=send_sems.at[a, k], recv_sem=recv_sems.at[a, k], device_id=to, device_id_type=MESH)

        mine = [pltpu.make_async_copy(x_refs[a], out_refs[a].at[4 * x + 2 * y + c], local_sems.at[a]) for a in range(n)]
        for cp in mine:
            cp.start()
        first = []
        for a in range(n):
            first.append(copy(a, 0, me, sibling, src=x_refs[a]))
            first += [copy(a, 1 + j, me, (*chip, c), src=x_refs[a]) for j, chip in enumerate(chips)]
        for cp in first:
            cp.start()
        passed = []
        for j, chip in enumerate(chips):
            for a in range(n):
                copy(a, 1 + j, (*chip, c), me).wait_recv()
                fwd = copy(a, 4 + j, (*chip, c), sibling)
                fwd.start()
                passed.append(fwd)
        for a in range(n):
            copy(a, 0, sibling, me).wait_recv()
            for j, chip in enumerate(chips):
                copy(a, 4 + j, (*chip, 1 - c), me).wait_recv()
        for cp in first + passed:
            cp.wait_send()
        for cp in mine:
            cp.wait()

    any_spec = pl.BlockSpec(memory_space=pl.ANY)
    return pl.pallas_call(
        body, name=name, out_shape=[jax.ShapeDtypeStruct((N_DEV, *s.shape), s.dtype) for s in shards],
        in_specs=[any_spec] * n, out_specs=[any_spec] * n,
        scratch_shapes=[pltpu.SemaphoreType.DMA((n, 7)), pltpu.SemaphoreType.DMA((n, 7)), pltpu.SemaphoreType.DMA((n,))],
    )(*shards)


N_CHIPS = N_DEV // 2


def _sibling_exchange(name, srcs):
    n = len(srcs)

    def body(*refs):
        src_refs, dst_refs = refs[:n], refs[n:2 * n]
        send_sems, recv_sems = refs[2 * n:]
        x, y, c = _coords()
        copies = []
        for a in range(n):
            for k in range(N_CHIPS):
                copies.append(pltpu.make_async_remote_copy(
                    src_ref=src_refs[a].at[2 * k + (1 - c)], dst_ref=dst_refs[a].at[k], send_sem=send_sems.at[a, k],
                    recv_sem=recv_sems.at[a, k], device_id=(x, y, 1 - c), device_id_type=MESH))
        for cp in copies:
            cp.start()
        for cp in copies:
            cp.wait_recv()
        for cp in copies:
            cp.wait_send()

    any_spec = pl.BlockSpec(memory_space=pl.ANY)
    return pl.pallas_call(
        body, name=name, out_shape=[jax.ShapeDtypeStruct((N_CHIPS, *s.shape[1:]), s.dtype) for s in srcs],
        in_specs=[any_spec] * n, out_specs=[any_spec] * n,
        scratch_shapes=[pltpu.SemaphoreType.DMA((n, N_CHIPS)), pltpu.SemaphoreType.DMA((n, N_CHIPS))],
    )(*srcs)


def _pair_add(name, core, g, got):
    _, R, C = g.shape
    tr = _pick(R, (256, 176, 128, R))

    def body(c_ref, a_ref, b_ref, o_ref):
        o_ref[...] = (a_ref[...].astype(F32) + b_ref[...].astype(F32)).astype(o_ref.dtype)

    grid_spec = pltpu.PrefetchScalarGridSpec(
        num_scalar_prefetch=1, grid=(N_CHIPS, R // tr),
        in_specs=[pl.BlockSpec((None, None, tr, C), lambda k, i, c: (k, c[0], i, 0)),
                  pl.BlockSpec((None, tr, C), lambda k, i, c: (k, i, 0))],
        out_specs=pl.BlockSpec((None, tr, C), lambda k, i, c: (k, i, 0)))
    return pl.pallas_call(body, name=name, grid_spec=grid_spec, out_shape=jax.ShapeDtypeStruct((N_CHIPS, R, C), g.dtype),
                          )(core, g.reshape(N_CHIPS, 2, R, C), got)


def _chip_exchange(name, srcs):
    n = len(srcs)

    def body(*refs):
        src_refs, dst_refs = refs[:n], refs[n:2 * n]
        send_sems, recv_sems, local_sems = refs[2 * n:]
        x, y, c = _coords()
        me = 2 * x + y
        mine = [pltpu.make_async_copy(src_refs[a].at[me], dst_refs[a].at[me], local_sems.at[a]) for a in range(n)]
        for cp in mine:
            cp.start()
        sends, recvs = [], []
        for r in range(1, N_CHIPS):
            px = 1 - x if r & 2 else x
            py = 1 - y if r & 1 else y
            peer = 2 * px + py
            for a in range(n):
                sends.append(pltpu.make_async_remote_copy(
                    src_ref=src_refs[a].at[peer], dst_ref=dst_refs[a].at[me], send_sem=send_sems.at[a, r - 1],
                    recv_sem=recv_sems.at[a, r - 1], device_id=(px, py, c), device_id_type=MESH))
                recvs.append(pltpu.make_async_remote_copy(
                    src_ref=src_refs[a].at[me], dst_ref=dst_refs[a].at[peer], send_sem=send_sems.at[a, r - 1],
                    recv_sem=recv_sems.at[a, r - 1], device_id=(px, py, c), device_id_type=MESH))
        for cp in sends:
            cp.start()
        for cp in recvs:
            cp.wait_recv()
        for cp in sends:
            cp.wait_send()
        for cp in mine:
            cp.wait()

    any_spec = pl.BlockSpec(memory_space=pl.ANY)
    return pl.pallas_call(
        body, name=name, out_shape=[jax.ShapeDtypeStruct(s.shape, s.dtype) for s in srcs],
        in_specs=[any_spec] * n, out_specs=[any_spec] * n,
        scratch_shapes=[pltpu.SemaphoreType.DMA((n, N_CHIPS - 1)), pltpu.SemaphoreType.DMA((n, N_CHIPS - 1)),
                        pltpu.SemaphoreType.DMA((n,))],
    )(*srcs)


def _adamw(name, parts, w, m, v):
    n_parts = parts.shape[0]
    R, C = w.shape
    tr = _pick(R, (256, 176, 128, R))
    c1 = 1.0 - ADAM_B1 ** ADAM_STEP
    c2 = 1.0 - ADAM_B2 ** ADAM_STEP

    def body(p_ref, w_ref, m_ref, v_ref, g_ref, d_ref, nm_ref, nv_ref):
        g = p_ref[0].astype(F32)
        for d in range(1, n_parts):
            g = g + p_ref[d].astype(F32)
        nm = ADAM_B1 * m_ref[...] + (1.0 - ADAM_B1) * g
        nv = ADAM_B2 * v_ref[...] + (1.0 - ADAM_B2) * (g * g)
        g_ref[...] = g
        nm_ref[...] = nm
        nv_ref[...] = nv
        d_ref[...] = -ADAM_LR * ((nm / c1) / (jnp.sqrt(nv / c2) + ADAM_EPS) + ADAM_WD * w_ref[...])

    t = pl.BlockSpec((tr, C), lambda i: (i, 0))
    o = jax.ShapeDtypeStruct((R, C), F32)
    return pl.pallas_call(
        body, name=name, grid=(R // tr,),
        in_specs=[pl.BlockSpec((n_parts, tr, C), lambda i: (0, i, 0)), t, t, t],
        out_specs=[t, t, t, t], out_shape=[o, o, o, o],
        compiler_params=pltpu.CompilerParams(dimension_semantics=("parallel",)),
    )(parts, w, m, v)


def _pack_rows(flats, rows):
    cat = jnp.concatenate(flats, axis=-1)
    return jnp.pad(cat, [(0, rows * PACK_COLS - cat.shape[-1])]).reshape(rows, PACK_COLS)


def _col_shards(full):
    r, cdim = full.shape
    return full.reshape(r, N_DEV, cdim // N_DEV).transpose(1, 0, 2)


def _from_col_shards(sh):
    _, r, c = sh.shape
    return sh.transpose(1, 0, 2).reshape(r, N_DEV * c)


def kernel(x, p, ffn1_w_in, ffn1_w_out, ln1_g, ln1_b, w_mix_in, b_gate, conv_w, a_log, dt_bias, gdn_norm_w, w_branch_sb, w_branch_gdn, w_mix_out, ln2_g, ln2_b, ffn2_w_in, ffn2_w_out, ln3_g, ln3_b, w_ple_gate, b_ple_gate, w_ple, ln4_g, ln4_b, loss_target, m_ffn1_w_in, m_ffn1_w_out, m_ln1_g, m_ln1_b, m_w_mix_in, m_b_gate, m_conv_w, m_a_log, m_dt_bias, m_gdn_norm_w, m_w_branch_sb, m_w_branch_gdn, m_w_mix_out, m_ln2_g, m_ln2_b, m_ffn2_w_in, m_ffn2_w_out, m_ln3_g, m_ln3_b, m_w_ple_gate, m_b_ple_gate, m_w_ple, m_ln4_g, m_ln4_b, v_ffn1_w_in, v_ffn1_w_out, v_ln1_g, v_ln1_b, v_w_mix_in, v_b_gate, v_conv_w, v_a_log, v_dt_bias, v_gdn_norm_w, v_w_branch_sb, v_w_branch_gdn, v_w_mix_out, v_ln2_g, v_ln2_b, v_ffn2_w_in, v_ffn2_w_out, v_ln3_g, v_ln3_b, v_w_ple_gate, v_b_ple_gate, v_w_ple, v_ln4_g, v_ln4_b):
    given = dict(locals())
    w_loc = {n: given[n][0] for n in WEIGHTS}
    m_loc = {n: given["m_" + n][0] for n in WEIGHTS}
    v_loc = {n: given["v_" + n][0] for n in WEIGHTS}
    sizes = {n: w_loc[n].size for n in WEIGHTS}

    conv_hi = conv_w[0].astype(BF16)
    conv_lo = (conv_w[0] - conv_hi.astype(F32)).astype(BF16)
    gathered = _all_gather("gather_weights", [w_loc[n].astype(BF16) for n in BIG] + [conv_lo])
    W = dict(zip(BIG, gathered[:-1]))
    for n in ("ffn1_w_out", "ffn2_w_out"):
        W[n] = W[n].reshape(N_DEV // 2, SHARD_FF, D_MODEL)
    for n in ("w_mix_out", "w_ple_gate"):
        W[n] = W[n].reshape(D_MODEL, D_MODEL)
    for n in ("w_branch_sb", "w_branch_gdn", "w_ple"):
        W[n] = _from_col_shards(W[n])
    wm = _from_col_shards(W.pop("w_mix_in"))
    W["w_mix_pad"] = jnp.concatenate([wm[:, :7 * HEAD_W], wm[:, 7 * HEAD_W + 2 * N_HEADS:],
                                      wm[:, 7 * HEAD_W:7 * HEAD_W + 2 * N_HEADS],
                                      jnp.zeros((D_MODEL, 128 - 2 * N_HEADS), BF16)], axis=1)
    P = {n: w_loc[n].reshape(1, -1) for n in SMALL}
    P["conv_w"] = _from_col_shards(W.pop("conv_w").astype(F32) + gathered[-1].astype(F32))

    loss, grad_x, grads, small = _local_step(x[0], p[0, 0], loss_target[0], W, P)
    loss = lax.psum(loss, ("x", "y", "c"))

    for n in ("w_mix_in", "w_branch_sb", "w_branch_gdn", "w_ple"):
        grads[n] = _col_shards(grads[n])
    grads["conv_w"] = _col_shards(grads["conv_w"]).astype(BF16)
    send = [grads[n].reshape(N_DEV, *w_loc[n].shape) for n in BIG]
    core = lax.axis_index("c").astype(jnp.int32).reshape(1)
    got = _sibling_exchange("grads_to_sibling", send)
    sums = [_pair_add(f"pair_add_{n}", core, g, h) for n, g, h in zip(BIG, send, got, strict=True)]
    parts = dict(zip(BIG, _chip_exchange("scatter_grads", sums)))
    small_rows = 16
    (small_parts,) = _all_gather("gather_small_grads", [_pack_rows([small[n].reshape(-1) for n in SMALL], small_rows)])

    res = {n: _adamw(f"adamw_{n}", parts[n], w_loc[n], m_loc[n], v_loc[n]) for n in BIG}
    pack = lambda d: _pack_rows([d[n].reshape(-1) for n in SMALL], small_rows)
    small_out = _adamw("adamw_small", small_parts, pack(w_loc), pack(m_loc), pack(v_loc))
    off = 0
    for n in SMALL:
        res[n] = [o.reshape(-1)[off:off + sizes[n]] for o in small_out]
        off += sizes[n]
    outs = [[res[n][i].reshape(given[n].shape) for n in WEIGHTS] for i in range(4)]
    g_out, d_out, nm_out, nv_out = outs
    return (loss, grad_x[None], *g_out, *d_out, *nm_out, *nv_out)
```

```python
import functools

import jax
import jax.numpy as jnp
from jax import lax
from jax.experimental import pallas as pl
from jax.experimental.pallas import tpu as pltpu

F32 = jnp.float32
BF16 = jnp.bfloat16
MESH = pl.DeviceIdType.MESH
HI = lax.Precision.HIGHEST

N_DEV = 8
D_MODEL = 1024
D_FF = 2816
PLE_DIM = 256
N_HEADS = 8
HEAD_DIM = 64
HEAD_W = N_HEADS * HEAD_DIM
GDN_CHUNK = 128
CONV_K = 4
CONV_CH = 3 * HEAD_W
ALPHA = 2.0 ** 0.25
LN_EPS = 1e-5
RMS_EPS = 1e-6
MIX_SB = 0
MIX_GDN = 3 * HEAD_W
MIX_GATES = MIX_GDN + 4 * HEAD_W
MIX_BA = MIX_GATES + 2 * D_MODEL
MIX_PAD = MIX_BA + 128
N_IN = 7 * HEAD_W + 2 * N_HEADS + 2 * D_MODEL

ADAM_LR = 0.001
ADAM_B1 = 0.9
ADAM_B2 = 0.999
ADAM_EPS = 1e-08
ADAM_WD = 0.01
ADAM_STEP = 10

SB_TQ = 512
SB_TK = 128
SB_CUTOFF = -110.0
HALO = 8

BIG = ["ffn1_w_in", "ffn1_w_out", "w_mix_in", "conv_w", "w_branch_sb", "w_branch_gdn", "w_mix_out",
       "ffn2_w_in", "ffn2_w_out", "w_ple_gate", "w_ple"]
SMALL = ["ln1_g", "ln1_b", "b_gate", "a_log", "dt_bias", "gdn_norm_w", "ln2_g", "ln2_b", "ln3_g", "ln3_b",
         "b_ple_gate", "ln4_g", "ln4_b"]
WEIGHTS = ["ffn1_w_in", "ffn1_w_out", "ln1_g", "ln1_b", "w_mix_in", "b_gate", "conv_w", "a_log", "dt_bias",
           "gdn_norm_w", "w_branch_sb", "w_branch_gdn", "w_mix_out", "ln2_g", "ln2_b", "ffn2_w_in", "ffn2_w_out",
           "ln3_g", "ln3_b", "w_ple_gate", "b_ple_gate", "w_ple", "ln4_g", "ln4_b"]
PACK_COLS = 1024


def _pick(n, cands):
    for c in cands:
        if n % c == 0:
            return c
    raise ValueError(f"no tile for {n} in {cands}")


_NN = (((1,), (0,)), ((), ()))
_NT = (((1,), (1,)), ((), ()))
_TN = (((0,), (0,)), ((), ()))
SHARD_FF = 2 * D_FF // N_DEV
ROW_TILES = (1024, 512, 256, 128)


def _matmul(name, dims, a, b, a_spec, b_spec, out_sds, out_spec, out_block, grid, nk):
    def body(a_ref, b_ref, o_ref, acc_ref):
        k = pl.program_id(2)

        @pl.when(k == 0)
        def _():
            acc_ref[...] = jnp.zeros_like(acc_ref)

        acc_ref[...] += lax.dot_general(a_ref[...], b_ref[...], dims, preferred_element_type=F32)

        @pl.when(k == nk - 1)
        def _():
            o_ref[...] = acc_ref[...].astype(o_ref.dtype)

    return pl.pallas_call(
        body, name=name, grid=grid, in_specs=[a_spec, b_spec], out_specs=out_spec, out_shape=out_sds,
        scratch_shapes=[pltpu.VMEM(out_block, F32)],
        compiler_params=pltpu.CompilerParams(dimension_semantics=("parallel", "parallel", "arbitrary")),
    )(a, b)


def _mm(name, a, w, out_dtype, n=None, col_off=0):
    M, K = a.shape
    n = w.shape[1] if n is None else n
    tm = _pick(M, ROW_TILES)
    tn = _pick(n, (512, 384, 256, 128))
    tk = _pick(K, (1024, 512, 256))
    assert col_off % tn == 0
    cb, nk = col_off // tn, K // tk
    return _matmul(
        name, _NN, a, w, pl.BlockSpec((tm, tk), lambda i, j, k: (i, k)), pl.BlockSpec((tk, tn), lambda i, j, k: (k, j + cb)),
        jax.ShapeDtypeStruct((M, n), out_dtype), pl.BlockSpec((tm, tn), lambda i, j, k: (i, j)), (tm, tn),
        (M // tm, n // tn, nk), nk)


def _mm_nt(name, a, w, out_dtype):
    M, K = a.shape
    N = w.shape[0]
    tm = _pick(M, ROW_TILES)
    tn = _pick(N, (1024, 512, 256, 128))
    tk = _pick(K, (1024, 1152, 512, 256))
    nk = K // tk
    return _matmul(
        name, _NT, a, w, pl.BlockSpec((tm, tk), lambda i, j, k: (i, k)), pl.BlockSpec((tn, tk), lambda i, j, k: (j, k)),
        jax.ShapeDtypeStruct((M, N), out_dtype), pl.BlockSpec((tm, tn), lambda i, j, k: (i, j)), (tm, tn),
        (M // tm, N // tn, nk), nk)


def _mm_tn(name, a, b, out_dtype):
    S, M = a.shape
    N = b.shape[1]
    tm = _pick(M, (1024, 512, 256, 128))
    tn = _pick(N, (1024, 640, 512, 256, 128))
    tk = _pick(S, ROW_TILES)
    nk = S // tk
    return _matmul(
        name, _TN, a, b, pl.BlockSpec((tk, tm), lambda i, j, k: (k, i)), pl.BlockSpec((tk, tn), lambda i, j, k: (k, j)),
        jax.ShapeDtypeStruct((M, N), out_dtype), pl.BlockSpec((tm, tn), lambda i, j, k: (i, j)), (tm, tn),
        (M // tm, N // tn, nk), nk)


def _sigmoid(z):
    return 1.0 / (1.0 + jnp.exp(-z))


def _ffn_up(name, xb, w_in):
    S, D = xb.shape
    c = w_in.shape[2]
    ts = _pick(S, ROW_TILES)
    half = N_DEV // 2

    def body(x_ref, wg_ref, wu_ref, a_ref, gu_ref):
        x = x_ref[...]
        g = jnp.dot(x, wg_ref[...], preferred_element_type=F32)
        u = jnp.dot(x, wu_ref[...], preferred_element_type=F32)
        a_ref[...] = (g * _sigmoid(g) * u).astype(a_ref.dtype)
        gu_ref[0] = g.astype(gu_ref.dtype)
        gu_ref[1] = u.astype(gu_ref.dtype)

    return pl.pallas_call(
        body, name=name, grid=(S // ts, half),
        in_specs=[pl.BlockSpec((ts, D), lambda i, j: (i, 0)), pl.BlockSpec((None, D, c), lambda i, j: (j, 0, 0)),
                  pl.BlockSpec((None, D, c), lambda i, j: (j + half, 0, 0))],
        out_specs=[pl.BlockSpec((None, ts, c), lambda i, j: (j, i, 0)),
                   pl.BlockSpec((2, None, ts, c), lambda i, j: (0, j, i, 0))],
        out_shape=[jax.ShapeDtypeStruct((half, S, c), BF16), jax.ShapeDtypeStruct((2, half, S, c), BF16)],
        compiler_params=pltpu.CompilerParams(dimension_semantics=("parallel", "parallel")),
    )(xb, w_in, w_in)


def _ffn_down(name, a4, w_out4):
    n, S, c = a4.shape
    D = w_out4.shape[2]
    tm = _pick(S, ROW_TILES)
    tn = _pick(D, (1024, 512, 256, 128))
    return _matmul(
        name, _NN, a4, w_out4, pl.BlockSpec((None, tm, c), lambda i, j, k: (k, i, 0)),
        pl.BlockSpec((None, c, tn), lambda i, j, k: (k, 0, j)),
        jax.ShapeDtypeStruct((S, D), F32), pl.BlockSpec((tm, tn), lambda i, j, k: (i, j)), (tm, tn),
        (S // tm, D // tn, n), n)


def _ffn_bwd_act(name, dyb, w_out4, gu):
    S, D = dyb.shape
    n, c, _ = w_out4.shape
    ts = _pick(S, ROW_TILES)

    def body(dy_ref, w_ref, gu_ref, o_ref):
        da = lax.dot_general(dy_ref[...], w_ref[...], _NT, preferred_element_type=F32)
        g = gu_ref[0].astype(F32)
        u = gu_ref[1].astype(F32)
        sg = _sigmoid(g)
        o_ref[0] = (da * u * (sg * (1.0 + g * (1.0 - sg)))).astype(o_ref.dtype)
        o_ref[1] = (da * (g * sg)).astype(o_ref.dtype)

    blk = pl.BlockSpec((2, None, ts, c), lambda i, j: (0, j, i, 0))
    return pl.pallas_call(
        body, name=name, grid=(S // ts, n),
        in_specs=[pl.BlockSpec((ts, D), lambda i, j: (i, 0)), pl.BlockSpec((None, c, D), lambda i, j: (j, 0, 0)), blk],
        out_specs=blk, out_shape=jax.ShapeDtypeStruct((2, n, S, c), BF16),
        compiler_params=pltpu.CompilerParams(dimension_semantics=("parallel", "parallel")),
    )(dyb, w_out4, gu)


def _ffn_d_w_out(name, a4, dyb):
    n, S, c = a4.shape
    D = dyb.shape[1]
    tn = _pick(D, (1024, 512, 256, 128))
    tk = _pick(S, ROW_TILES)
    nk = S // tk
    return _matmul(
        name, _TN, a4, dyb, pl.BlockSpec((None, tk, c), lambda i, j, k: (i, k, 0)), pl.BlockSpec((tk, tn), lambda i, j, k: (k, j)),
        jax.ShapeDtypeStruct((n, c, D), BF16), pl.BlockSpec((None, c, tn), lambda i, j, k: (i, 0, j)), (c, tn),
        (n, D // tn, nk), nk)


def _ffn_d_w_in(name, hb, dgu8):
    S, D = hb.shape
    n, _, c = dgu8.shape
    tm = _pick(D, (1024, 512, 256, 128))
    tk = _pick(S, ROW_TILES)
    nk = S // tk
    return _matmul(
        name, _TN, hb, dgu8, pl.BlockSpec((tk, tm), lambda i, j, k: (k, j)), pl.BlockSpec((None, tk, c), lambda i, j, k: (i, k, 0)),
        jax.ShapeDtypeStruct((n, D, c), BF16), pl.BlockSpec((None, tm, c), lambda i, j, k: (i, j, 0)), (tm, c),
        (n, D // tm, nk), nk)


def _ffn_in_bwd(name, dgu8, w_in):
    n, S, c = dgu8.shape
    D = w_in.shape[1]
    tm = _pick(S, ROW_TILES)
    tn = _pick(D, (1024, 512, 256, 128))
    return _matmul(
        name, _NT, dgu8, w_in, pl.BlockSpec((None, tm, c), lambda i, j, k: (k, i, 0)),
        pl.BlockSpec((None, tn, c), lambda i, j, k: (k, j, 0)),
        jax.ShapeDtypeStruct((S, D), F32), pl.BlockSpec((tm, tn), lambda i, j, k: (i, j)), (tm, tn),
        (S // tm, D // tn, n), n)


def _rowwise(name, fn, tiled, params, outs, reds=(), ts=256):
    S = tiled[0][0].shape[0]
    ts = _pick(S, (ts, 128))
    n_t, n_p, n_o = len(tiled), len(params), len(outs)

    def body(*refs):
        vals = [r[...] for r in refs[:n_t + n_p]]
        res_o, res_r = fn(*vals)
        o_refs = refs[n_t + n_p:n_t + n_p + n_o]
        r_refs = refs[n_t + n_p + n_o:]
        for r, val in zip(o_refs, res_o, strict=True):
            r[...] = val.astype(r.dtype)
        if r_refs:
            i = pl.program_id(0)

            @pl.when(i == 0)
            def _():
                for r, val in zip(r_refs, res_r, strict=True):
                    r[...] = val.astype(F32)

            @pl.when(i > 0)
            def _():
                for r, val in zip(r_refs, res_r, strict=True):
                    r[...] += val.astype(F32)

    in_specs = [pl.BlockSpec((ts, w), functools.partial(lambda i, cb: (i, cb), cb=cb)) for _, cb, w in tiled]
    in_specs += [pl.BlockSpec(p.shape, lambda i: (0, 0)) for p in params]
    out_specs = [pl.BlockSpec((ts, w), lambda i: (i, 0)) for w, _ in outs]
    out_specs += [pl.BlockSpec(tuple(r), lambda i: (0, 0)) for r in reds]
    out_shape = [jax.ShapeDtypeStruct((S, w), dt) for w, dt in outs]
    out_shape += [jax.ShapeDtypeStruct(tuple(r), F32) for r in reds]
    res = pl.pallas_call(
        body, name=name, grid=(S // ts,), in_specs=in_specs, out_specs=out_specs, out_shape=out_shape,
        compiler_params=pltpu.CompilerParams(dimension_semantics=("arbitrary",)),
    )(*[t[0] for t in tiled], *params)
    return res[:n_o], res[n_o:]


def _ln(r, g, b):
    mu = jnp.mean(r, axis=-1, keepdims=True)
    xc = r - mu
    var = jnp.mean(xc * xc, axis=-1, keepdims=True)
    return xc * lax.rsqrt(var + LN_EPS) * g + b


def _ln_bwd(r, g, b, dy):
    _, vjp = jax.vjp(_ln, r, g, b)
    return vjp(dy)


def _shift_down(x, d):
    return x if d == 0 else pltpu.roll(x, d, 0)


def _conv_fwd(name, x, w):
    S = x.shape[0]
    ts = _pick(S, (256, 128))
    hb = ts // HALO

    def body(x_ref, prev_ref, w_ref, o_ref):
        i = pl.program_id(0)
        prev = jnp.where(i > 0, prev_ref[...], 0.0)
        xe = jnp.concatenate([prev, x_ref[...]], axis=0)
        y = jnp.zeros((ts + HALO, CONV_CH), F32)
        for j in range(CONV_K):
            y = y + w_ref[pl.ds(j, 1), :] * _shift_down(xe, CONV_K - 1 - j)
        y = y[HALO:, :]
        o_ref[...] = y * _sigmoid(y)

    return pl.pallas_call(
        body, name=name, grid=(S // ts,),
        in_specs=[pl.BlockSpec((ts, CONV_CH), lambda i: (i, 0)),
                  pl.BlockSpec((HALO, CONV_CH), lambda i: (jnp.maximum(i * hb - 1, 0), 0)),
                  pl.BlockSpec((CONV_K, CONV_CH), lambda i: (0, 0))],
        out_specs=pl.BlockSpec((ts, CONV_CH), lambda i: (i, 0)),
        out_shape=jax.ShapeDtypeStruct((S, CONV_CH), F32),
        compiler_params=pltpu.CompilerParams(dimension_semantics=("arbitrary",)),
    )(x, x, w)


def _conv_bwd(name, x, w, dout):
    S = x.shape[0]
    ts = _pick(S, (256, 128))
    hb = ts // HALO
    nt = S // ts
    n_ext = ts + 2 * HALO

    def body(x_ref, prev_ref, next_ref, w_ref, d_ref, dnext_ref, dx_ref, dw_ref):
        i = pl.program_id(0)
        prev = jnp.where(i > 0, prev_ref[...], 0.0)
        last = i == nt - 1
        nxt = jnp.where(last, 0.0, next_ref[...])
        dnxt = jnp.where(last, 0.0, dnext_ref[...])
        xe = jnp.concatenate([prev, x_ref[...], nxt], axis=0)
        de = jnp.concatenate([jnp.zeros((HALO, CONV_CH), F32), d_ref[...], dnxt], axis=0)
        y = jnp.zeros((n_ext, CONV_CH), F32)
        for j in range(CONV_K):
            y = y + w_ref[pl.ds(j, 1), :] * _shift_down(xe, CONV_K - 1 - j)
        sg = _sigmoid(y)
        dy = de * (sg * (1.0 + y * (1.0 - sg)))
        dx = jnp.zeros((n_ext, CONV_CH), F32)
        for j in range(CONV_K):
            m = CONV_K - 1 - j
            dx = dx + w_ref[pl.ds(j, 1), :] * (dy if m == 0 else pltpu.roll(dy, n_ext - m, 0))
        dx_ref[...] = dx[HALO:HALO + ts, :].astype(dx_ref.dtype)
        row = lax.broadcasted_iota(jnp.int32, (n_ext, 1), 0)
        dy_own = jnp.where((row >= HALO) & (row < HALO + ts), dy, 0.0)
        parts = [jnp.sum(dy_own * _shift_down(xe, CONV_K - 1 - j), axis=0, keepdims=True) for j in range(CONV_K)]
        dw = jnp.concatenate(parts + [jnp.zeros((HALO - CONV_K, CONV_CH), F32)], axis=0)

        @pl.when(i == 0)
        def _():
            dw_ref[...] = dw

        @pl.when(i > 0)
        def _():
            dw_ref[...] += dw

    tile = pl.BlockSpec((ts, CONV_CH), lambda i: (i, 0))
    prev = pl.BlockSpec((HALO, CONV_CH), lambda i: (jnp.maximum(i * hb - 1, 0), 0))
    nxt = pl.BlockSpec((HALO, CONV_CH), lambda i: (jnp.minimum((i + 1) * hb, nt * hb - 1), 0))
    return pl.pallas_call(
        body, name=name, grid=(nt,),
        in_specs=[tile, prev, nxt, pl.BlockSpec((CONV_K, CONV_CH), lambda i: (0, 0)), tile, nxt],
        out_specs=[tile, pl.BlockSpec((HALO, CONV_CH), lambda i: (0, 0))],
        out_shape=[jax.ShapeDtypeStruct((S, CONV_CH), BF16), jax.ShapeDtypeStruct((HALO, CONV_CH), F32)],
        compiler_params=pltpu.CompilerParams(dimension_semantics=("arbitrary",)),
    )(x, x, x, w, dout, dout)


def _softplus(z):
    return jnp.maximum(z, 0.0) + jnp.log(1.0 + jnp.exp(-jnp.abs(z)))


def _dot2(x, tri2):
    hi = x.astype(BF16)
    lo = (x - hi.astype(F32)).astype(BF16)
    return jnp.dot(jnp.concatenate([hi, lo], axis=1), tri2, preferred_element_type=F32)


def _pair_tri(tk, keep):
    r = lax.broadcasted_iota(jnp.int32, (4 * tk, 2 * tk), 0)
    c = lax.broadcasted_iota(jnp.int32, (4 * tk, 2 * tk), 1)
    same_head = ((r // tk) % 2) == (c // tk)
    return (same_head & keep(r % tk, c % tk)).astype(BF16)


def _pair_rows(x2, first):
    return jnp.concatenate([jnp.where(first, x2, 0), jnp.where(first, 0, x2)], axis=0)


def _pair_sum(x):
    h = x.shape[1] // 2
    return jnp.sum(x[:, :h], axis=1, keepdims=True), jnp.sum(x[:, h:], axis=1, keepdims=True)


def _sb_tiles(S):
    tq = _pick(S, (SB_TQ, 256, 128))
    return tq, SB_TK


N_PAIRS = N_HEADS // 2


def _sb_specs(S, tq):
    q = pl.BlockSpec((tq, 128), lambda p, i: (i, p))
    k = pl.BlockSpec((S, 128), lambda p, i: (0, N_PAIRS + p))
    v = pl.BlockSpec((S, 128), lambda p, i: (0, 2 * N_PAIRS + p))
    return q, k, v


def _scaled(q):
    return (q.astype(F32) * (HEAD_DIM ** -0.5)).astype(BF16)


def _sb_fwd(qkv):
    S = qkv.shape[0]
    tq, tk = _sb_tiles(S)
    nd = tq // tk

    def body(q_ref, k_ref, v_ref, o_ref, ta_ref, tb_ref, cnt_ref):
        qi = pl.program_id(1)
        qs = _scaled(q_ref[...])
        row = qi * tq + lax.broadcasted_iota(jnp.int32, (tq, 2 * tk), 0)
        col = lax.broadcasted_iota(jnp.int32, (tq, 2 * tk), 1)
        left = col < tk
        col = col % tk
        later = _pair_tri(tk, lambda r, c: r > c)
        first = lax.broadcasted_iota(jnp.int32, (tk, 128), 1) < HEAD_DIM

        def tile(j, carry, masked):
            acc, suf_a, suf_b = carry
            off = pl.multiple_of(j * tk, tk)
            kc = _pair_rows(k_ref[pl.ds(off, tk), :], first)
            vc = _pair_rows(v_ref[pl.ds(off, tk), :], first)
            mask = (col + j * tk) < row
            z = lax.dot_general(qs, kc, _NT, preferred_element_type=F32)
            lf = -_softplus(z)
            if masked:
                lf = jnp.where(mask, lf, 0.0)
            w = jnp.exp(z + lf + _dot2(lf, later) + jnp.where(left, suf_a, suf_b))
            if masked:
                w = jnp.where(mask, w, 0.0)
            acc = acc + jnp.dot(w.astype(BF16), vc, preferred_element_type=F32)
            sum_a, sum_b = _pair_sum(lf)
            return acc, suf_a + sum_a, suf_b + sum_b

        zero1 = jnp.zeros((tq, 1), F32)
        carry = (jnp.zeros((tq, 128), F32), zero1, zero1)
        n_full = qi * nd
        acc, suf_a, suf_b = lax.fori_loop(0, nd, lambda d, c: tile(n_full + nd - 1 - d, c, True), carry)

        def alive(sa, sb):
            return jnp.maximum(jnp.max(sa), jnp.max(sb)) > SB_CUTOFF

        def step(c):
            jj, acc, sa, sb, _ = c
            acc, sa, sb = tile(n_full - 1 - jj, (acc, sa, sb), False)
            return jj + 1, acc, sa, sb, alive(sa, sb)

        n_used, acc, suf_a, suf_b, _ = lax.while_loop(
            lambda c: (c[0] < n_full) & c[4], step, (0, acc, suf_a, suf_b, alive(suf_a, suf_b)))
        o_ref[...] = acc.astype(o_ref.dtype)
        ta_ref[...] = suf_a
        tb_ref[...] = suf_b
        cnt_ref[pl.program_id(0), qi] = n_used

    tot = pl.BlockSpec((None, tq, 1), lambda p, i: (p, i, 0))
    tshape = jax.ShapeDtypeStruct((N_PAIRS, S, 1), F32)
    return pl.pallas_call(
        body, name="sb_fwd", grid=(N_PAIRS, S // tq), in_specs=list(_sb_specs(S, tq)),
        out_specs=[pl.BlockSpec((tq, 128), lambda p, i: (i, p)), tot, tot, pl.BlockSpec(memory_space=pltpu.SMEM)],
        out_shape=[jax.ShapeDtypeStruct((S, HEAD_W), BF16), tshape, tshape,
                   jax.ShapeDtypeStruct((N_PAIRS, S // tq), jnp.int32)],
        compiler_params=pltpu.CompilerParams(dimension_semantics=("arbitrary", "arbitrary")),
    )(qkv, qkv, qkv)


def _sb_bwd(qkv, tot_a, tot_b, visited, do):
    S = qkv.shape[0]
    tq, tk = _sb_tiles(S)
    nd = tq // tk
    scale = HEAD_DIM ** -0.5

    def body(q_ref, k_ref, v_ref, ta_ref, tb_ref, cnt_ref, do_ref, dq_ref, dk_ref, dv_ref):
        qi = pl.program_id(1)

        @pl.when(qi == 0)
        def _():
            dk_ref[...] = jnp.zeros_like(dk_ref)
            dv_ref[...] = jnp.zeros_like(dv_ref)

        qs = _scaled(q_ref[...])
        do2 = do_ref[...]
        row = qi * tq + lax.broadcasted_iota(jnp.int32, (tq, 2 * tk), 0)
        col = lax.broadcasted_iota(jnp.int32, (tq, 2 * tk), 1)
        left = col < tk
        col = col % tk
        total = jnp.where(left, ta_ref[...], tb_ref[...])
        upto = _pair_tri(tk, lambda r, c: r <= c)
        first = lax.broadcasted_iota(jnp.int32, (tk, 128), 1) < HEAD_DIM

        def tile(j, carry, masked):
            dq, pl_a, pl_b, pg_a, pg_b = carry
            off = pl.multiple_of(j * tk, tk)
            kc = _pair_rows(k_ref[pl.ds(off, tk), :], first)
            vc = _pair_rows(v_ref[pl.ds(off, tk), :], first)
            mask = (col + j * tk) < row
            z = lax.dot_general(qs, kc, _NT, preferred_element_type=F32)
            lf = -_softplus(z)
            sig = jnp.exp(z + lf)
            if masked:
                lf = jnp.where(mask, lf, 0.0)
            w = jnp.exp(z + lf + (total - (jnp.where(left, pl_a, pl_b) + _dot2(lf, upto))))
            if masked:
                w = jnp.where(mask, w, 0.0)
            gl = lax.dot_general(do2, vc, _NT, preferred_element_type=F32) * w
            dz = gl - sig * (jnp.where(left, pg_a, pg_b) + _dot2(gl, upto))
            if masked:
                dz = jnp.where(mask, dz, 0.0)
            dzb = dz.astype(BF16)
            dq = dq + jnp.dot(dzb, kc, preferred_element_type=F32)
            dkc = lax.dot_general(dzb, qs, _TN, preferred_element_type=F32)
            dvc = lax.dot_general(w.astype(BF16), do2, _TN, preferred_element_type=F32)
            dk_ref[pl.ds(off, tk), :] += jnp.where(first, dkc[:tk], dkc[tk:])
            dv_ref[pl.ds(off, tk), :] += jnp.where(first, dvc[:tk], dvc[tk:])
            sl_a, sl_b = _pair_sum(lf)
            sg_a, sg_b = _pair_sum(gl)
            return dq, pl_a + sl_a, pl_b + sl_b, pg_a + sg_a, pg_b + sg_b

        zero1 = jnp.zeros((tq, 1), F32)
        carry = (jnp.zeros((tq, 128), F32), zero1, zero1, zero1, zero1)
        n_full = qi * nd
        n_used = cnt_ref[pl.program_id(0), qi]
        first_tile = n_full - n_used
        carry = lax.fori_loop(0, n_used, lambda t, c: tile(first_tile + t, c, False), carry)
        dq = lax.fori_loop(0, nd, lambda d, c: tile(n_full + d, c, True), carry)[0]
        dq_ref[...] = dq * scale

    q_spec, k_spec, v_spec = _sb_specs(S, tq)
    tot = pl.BlockSpec((None, tq, 1), lambda p, i: (p, i, 0))
    whole = pl.BlockSpec((S, 128), lambda p, i: (0, p))
    big = jax.ShapeDtypeStruct((S, HEAD_W), F32)
    return pl.pallas_call(
        body, name="sb_bwd", grid=(N_PAIRS, S // tq),
        in_specs=[q_spec, k_spec, v_spec, tot, tot, pl.BlockSpec(memory_space=pltpu.SMEM), q_spec],
        out_specs=[q_spec, whole, whole], out_shape=[big, big, big],
        compiler_params=pltpu.CompilerParams(dimension_semantics=("arbitrary", "arbitrary")),
    )(qkv, qkv, qkv, tot_a, tot_b, visited, do)


GDN_PRECISION = lax.Precision.HIGH


GDN_STATE_PRECISION = lax.Precision.DEFAULT


def _bmm(a, b, precision=GDN_PRECISION):
    return lax.dot_general(a, b, (((2,), (1,)), ((0,), (0,))), precision=precision, preferred_element_type=F32)


def _bmm_nt(a, b, precision=GDN_PRECISION):
    return lax.dot_general(a, b, (((2,), (2,)), ((0,), (0,))), precision=precision, preferred_element_type=F32)


def _bmm_tn(a, b, precision=GDN_PRECISION):
    return lax.dot_general(a, b, (((1,), (1,)), ((0,), (0,))), precision=precision, preferred_element_type=F32)


def _tri_inv(lower):
    C = lower.shape[-1]
    ii = lax.broadcasted_iota(jnp.int32, (C, C), 0)
    jj = lax.broadcasted_iota(jnp.int32, (C, C), 1)
    eye = (ii == jj).astype(F32)[None]
    xd = jnp.where((ii // 8 == jj // 8)[None], -lower, 0.0)
    x2 = _bmm(xd, xd)
    x4 = _bmm(x2, x2)
    inv = eye + xd
    inv = inv + _bmm(inv, x2)
    inv = inv + _bmm(inv, x4)
    b = 8
    while b < C:
        off = jnp.where(((ii // (2 * b) == jj // (2 * b)) & (ii // b != jj // b))[None], lower, 0.0)
        inv = inv - _bmm(inv, _bmm(off, inv))
        b *= 2
    return inv


@jax.custom_vjp
def _tri_solve(lower, inv, rhs):
    return _bmm(inv, rhs)


def _tri_solve_fwd(lower, inv, rhs):
    sol = _bmm(inv, rhs)
    return sol, (inv, sol)


def _tri_solve_bwd(res, dsol):
    inv, sol = res
    drhs = _bmm_tn(inv, dsol)
    C = inv.shape[-1]
    ii = lax.broadcasted_iota(jnp.int32, (C, C), 0)
    jj = lax.broadcasted_iota(jnp.int32, (C, C), 1)
    return jnp.where((jj < ii)[None], -_bmm_nt(drhs, sol), 0.0), jnp.zeros_like(inv), drhs


_tri_solve.defvjp(_tri_solve_fwd, _tri_solve_bwd)


def _pairs(x):
    return jnp.stack([x[:, 128 * p:128 * (p + 1)] for p in range(N_PAIRS)], axis=0)


def _unpairs(x):
    return jnp.concatenate([x[p] for p in range(N_PAIRS)], axis=1)


def _gdn_chunk(state, qkv, gz, ba, a_log_x, dt_x, norm_x, inv=None):
    C = qkv.shape[0]
    lane = lax.broadcasted_iota(jnp.int32, (1, 1, 128), 2)
    first = lane < HEAD_DIM

    def split_heads(x2):
        return jnp.stack([jnp.where(first, x2, 0.0), jnp.where(first, 0.0, x2)], axis=1).reshape(N_HEADS, *x2.shape[1:])

    def merge_heads(xh):
        x = xh.reshape(N_PAIRS, 2, *xh.shape[1:])
        return x[:, 0] + x[:, 1]

    def head_cols(x2):
        a = jnp.sum(jnp.where(lane == 0, x2, 0.0), axis=-1, keepdims=True)
        b = jnp.sum(jnp.where(lane == HEAD_DIM, x2, 0.0), axis=-1, keepdims=True)
        return jnp.stack([a, b], axis=1).reshape(N_HEADS, *a.shape[1:])

    def to_pair(xh):
        x = xh.reshape(N_PAIRS, 2, *xh.shape[1:])
        return jnp.where(first, x[:, 0], x[:, 1])

    def head_sums(x2):
        a = jnp.sum(jnp.where(first, x2, 0.0), axis=-1, keepdims=True)
        b = jnp.sum(jnp.where(first, 0.0, x2), axis=-1, keepdims=True)
        return jnp.where(first, a, b)

    er = lax.broadcasted_iota(jnp.int32, (128, 2 * HEAD_W), 0)
    ec = lax.broadcasted_iota(jnp.int32, (128, 2 * HEAD_W), 1)
    spread = (er == ec // HEAD_DIM).astype(F32)
    bx = lax.dot_general(ba, spread, (((1,), (0,)), ((), ())), precision=HI, preferred_element_type=F32)
    beta2 = _pairs(_sigmoid(bx[:, :HEAD_W]))
    g2 = _pairs(-jnp.exp(a_log_x) * _softplus(bx[:, HEAD_W:] + dt_x))
    beta = head_cols(beta2)
    g = head_cols(g2)

    q2, k2, v2 = (_pairs(qkv[:, i * HEAD_W:(i + 1) * HEAD_W]) for i in range(3))
    qn2 = q2 * lax.rsqrt(head_sums(q2 * q2) + RMS_EPS) * (HEAD_DIM ** -0.5)
    kn2 = k2 * lax.rsqrt(head_sums(k2 * k2) + RMS_EPS)
    knh = split_heads(kn2)

    ii = lax.broadcasted_iota(jnp.int32, (C, C), 0)[None]
    jj = lax.broadcasted_iota(jnp.int32, (C, C), 1)[None]
    incl = jj <= ii
    g_row = jnp.sum(jnp.where(ii == jj, g, 0.0), axis=1, keepdims=True)
    gc_col = jnp.sum(jnp.where(incl, g_row, 0.0), axis=2, keepdims=True)
    gc_row = jnp.sum(jnp.where(ii <= jj, g, 0.0), axis=1, keepdims=True)
    decay = jnp.where(incl, jnp.exp(jnp.where(incl, gc_col - gc_row, 0.0)), 0.0)
    lower = jnp.where(jj < ii, beta * _bmm_nt(knh, knh) * decay, 0.0)
    gc2 = to_pair(gc_col)
    egc2 = jnp.exp(gc2)
    if inv is None:
        inv = _tri_inv(lower)
    u2 = merge_heads(_tri_solve(lower, inv, split_heads(v2 * beta2)))
    w2 = merge_heads(_tri_solve(lower, inv, split_heads(kn2 * (beta2 * egc2))))
    lo = GDN_STATE_PRECISION
    qk = jnp.where(incl, _bmm_nt(jnp.repeat(qn2, 2, axis=0), knh, lo) * decay, 0.0)
    g_last2 = to_pair(jnp.sum(g, axis=1, keepdims=True))
    v_new2 = u2 - _bmm(w2, state, lo)
    o2 = _bmm(qn2 * egc2, state, lo) + merge_heads(_bmm(qk, split_heads(v_new2), lo))
    sr = lax.broadcasted_iota(jnp.int32, (128, 128), 0)
    sc = lax.broadcasted_iota(jnp.int32, (128, 128), 1)
    same_head = ((sr < HEAD_DIM) == (sc < HEAD_DIM))[None]
    new_state = state * jnp.exp(g_last2) + jnp.where(same_head, _bmm_tn(kn2 * jnp.exp(g_last2 - gc2), v_new2, lo), 0.0)
    o2 = o2 * lax.rsqrt(head_sums(o2 * o2) * (1.0 / HEAD_DIM) + RMS_EPS) * _pairs(norm_x)
    gz2 = _pairs(gz)
    return new_state, _unpairs(o2 * (gz2 * _sigmoid(gz2))), inv


def _gdn_specs(order):
    C = GDN_CHUNK
    par = pl.BlockSpec((1, HEAD_W), lambda c: (0, 0))
    return [pl.BlockSpec((C, CONV_CH), lambda c: (order(c), 0)), pl.BlockSpec((C, HEAD_W), lambda c: (order(c), 3)),
            pl.BlockSpec((C, 128), lambda c: (order(c), 0)), par, par, par]


def _gdn_fwd(qkv, gdnp, ba, a_log_x, dt_x, norm_x):
    S = qkv.shape[0]
    C = GDN_CHUNK
    n = S // C

    def body(qkv_ref, gz_ref, ba_ref, al_ref, dt_ref, nw_ref, o_ref, st_ref, inv_ref, state):
        @pl.when(pl.program_id(0) == 0)
        def _():
            state[...] = jnp.zeros_like(state)

        st = state[...]
        st_ref[...] = st
        new, o, inv = _gdn_chunk(st, qkv_ref[...], gz_ref[...], ba_ref[...], al_ref[...], dt_ref[...], nw_ref[...])
        state[...] = new
        o_ref[...] = o.astype(o_ref.dtype)
        inv_ref[...] = inv

    return pl.pallas_call(
        body, name="gdn_fwd", grid=(n,), in_specs=_gdn_specs(lambda c: c),
        out_specs=[pl.BlockSpec((C, HEAD_W), lambda c: (c, 0)),
                   pl.BlockSpec((None, N_PAIRS, 128, 128), lambda c: (c, 0, 0, 0)),
                   pl.BlockSpec((None, N_HEADS, C, C), lambda c: (c, 0, 0, 0))],
        out_shape=[jax.ShapeDtypeStruct((S, HEAD_W), BF16), jax.ShapeDtypeStruct((n, N_PAIRS, 128, 128), F32),
                   jax.ShapeDtypeStruct((n, N_HEADS, C, C), F32)],
        scratch_shapes=[pltpu.VMEM((N_PAIRS, 128, 128), F32)],
        compiler_params=pltpu.CompilerParams(dimension_semantics=("arbitrary",)),
    )(qkv, gdnp, ba, a_log_x, dt_x, norm_x)


def _gdn_bwd(qkv, gdnp, ba, a_log_x, dt_x, norm_x, states, invs, do):
    S = qkv.shape[0]
    C = GDN_CHUNK
    n = S // C

    def body(qkv_ref, gz_ref, ba_ref, al_ref, dt_ref, nw_ref, st_ref, inv_ref, do_ref,
             dqkv_ref, dgz_ref, dba_ref, dal_ref, ddt_ref, dnw_ref, dstate):
        @pl.when(pl.program_id(0) == 0)
        def _():
            dstate[...] = jnp.zeros_like(dstate)
            dal_ref[...] = jnp.zeros_like(dal_ref)
            ddt_ref[...] = jnp.zeros_like(ddt_ref)
            dnw_ref[...] = jnp.zeros_like(dnw_ref)

        args = (st_ref[...], qkv_ref[...], gz_ref[...], ba_ref[...], al_ref[...], dt_ref[...], nw_ref[...])
        inv = inv_ref[...]
        _, vjp = jax.vjp(lambda *a: _gdn_chunk(*a, inv=inv)[:2], *args)
        dst, dqkv, dgz, dba, dal, ddt, dnw = vjp((dstate[...], do_ref[...]))
        dstate[...] = dst
        dqkv_ref[...] = dqkv
        dgz_ref[...] = dgz.astype(dgz_ref.dtype)
        dba_ref[...] = dba
        dal_ref[...] += dal
        ddt_ref[...] += ddt
        dnw_ref[...] += dnw

    rev = lambda c: n - 1 - c
    par = pl.BlockSpec((1, HEAD_W), lambda c: (0, 0))
    f = jax.ShapeDtypeStruct
    return pl.pallas_call(
        body, name="gdn_bwd", grid=(n,),
        in_specs=_gdn_specs(rev) + [pl.BlockSpec((None, N_PAIRS, 128, 128), lambda c: (rev(c), 0, 0, 0)),
                                    pl.BlockSpec((None, N_HEADS, C, C), lambda c: (rev(c), 0, 0, 0)),
                                    pl.BlockSpec((C, HEAD_W), lambda c: (rev(c), 0))],
        out_specs=[pl.BlockSpec((C, CONV_CH), lambda c: (rev(c), 0)), pl.BlockSpec((C, HEAD_W), lambda c: (rev(c), 0)),
                   pl.BlockSpec((C, 128), lambda c: (rev(c), 0)), par, par, par],
        out_shape=[f((S, CONV_CH), F32), f((S, HEAD_W), BF16), f((S, 128), F32)] + [f((1, HEAD_W), F32)] * 3,
        scratch_shapes=[pltpu.VMEM((N_PAIRS, 128, 128), F32)],
        compiler_params=pltpu.CompilerParams(dimension_semantics=("arbitrary",)),
    )(qkv, gdnp, ba, a_log_x, dt_x, norm_x, states, invs, do)


def _local_step(x, p, target, W, P):
    S = x.shape[0]
    D = D_MODEL
    xb = x.astype(BF16)
    pb = p.astype(BF16)

    def ffn_fwd(tag, h, hb, w_in, w_out, g, b):
        a, gu = _ffn_up(f"{tag}_up", hb, w_in)
        f = _ffn_down(f"{tag}_down", a, w_out)

        def fn(h, f, g, b):
            r = ALPHA * h + 0.5 * f
            y = _ln(r, g, b)
            return (r, y, y), ()

        (r, y, yb), _ = _rowwise(f"{tag}_ln", fn, [(h, 0, D), (f, 0, D)], [g, b], [(D, F32), (D, F32), (D, BF16)])
        return a, gu, r, y, yb

    a1, gu1, r1, h1, h1b = ffn_fwd("ffn1", x, xb, W["ffn1_w_in"], W["ffn1_w_out"], P["ln1_g"], P["ln1_b"])

    wmix = W["w_mix_pad"]
    sbp = _mm("mix_sb", h1b, wmix, BF16, n=3 * HEAD_W, col_off=MIX_SB)
    gdnp = _mm("mix_gdn", h1b, wmix, F32, n=4 * HEAD_W, col_off=MIX_GDN)
    gates = _mm("mix_gates", h1b, wmix, F32, n=2 * D, col_off=MIX_GATES)
    ba = _mm("mix_ba", h1b, wmix, F32, n=128, col_off=MIX_BA)

    attb, sb_tot_a, sb_tot_b, sb_visited = _sb_fwd(sbp)
    y_sb = _mm("sb_out", attb, W["w_branch_sb"], F32)

    conv_w = P["conv_w"]
    qkv = _conv_fwd("conv_fwd", gdnp, conv_w)
    a_log_x = jnp.repeat(P["a_log"], HEAD_DIM, axis=1)
    dt_x = jnp.repeat(P["dt_bias"], HEAD_DIM, axis=1)
    norm_x = jnp.tile(P["gdn_norm_w"], (1, N_HEADS))
    gob, states, invs = _gdn_fwd(qkv, gdnp, ba, a_log_x, dt_x, norm_x)
    y_gdn = _mm("gdn_out", gob, W["w_branch_gdn"], F32)

    def merge_fn(gs, gg, ys, yg, bs, bg):
        return ((_sigmoid(gs + bs) * ys + _sigmoid(gg + bg) * yg),), ()

    b_gate = P["b_gate"]
    bs, bg = b_gate[:, :D], b_gate[:, D:]
    (merged,), _ = _rowwise("mix_merge", merge_fn, [(gates, 0, D), (gates, 1, D), (y_sb, 0, D), (y_gdn, 0, D)], [bs, bg],
                            [(D, BF16)])
    mix = _mm("mix_out", merged, W["w_mix_out"], F32)

    def ln2_fn(h, f, g, b):
        r = ALPHA * h + f
        y = _ln(r, g, b)
        return (r, y, y), ()

    (r2, h2, h2b), _ = _rowwise("mix_ln", ln2_fn, [(h1, 0, D), (mix, 0, D)], [P["ln2_g"], P["ln2_b"]],
                                [(D, F32), (D, F32), (D, BF16)])

    a2, gu2, r3, h3, h3b = ffn_fwd("ffn2", h2, h2b, W["ffn2_w_in"], W["ffn2_w_out"], P["ln3_g"], P["ln3_b"])

    zg = _mm("ple_gate", h3b, W["w_ple_gate"], F32)
    pp = _mm("ple_proj", pb, W["w_ple"], F32)

    def ple(h, zg, pp, bp, g, b):
        return _ln(ALPHA * h + _sigmoid(zg + bp) * pp, g, b)

    def head_fn(h, zg, pp, tgt, bp, g, b):
        y, vjp = jax.vjp(ple, h, zg, pp, bp, g, b)
        err = y - tgt
        dh, dzg, dpp, dbp, dg, db = vjp(err * (1.0 / D))
        loss = 0.5 * jnp.sum(jnp.sum(err * err, axis=1, keepdims=True), axis=0, keepdims=True) * (1.0 / D)
        return (dh, dzg, dpp), (loss, dbp, dg, db)

    (dh3_a, dzg, dpp), (loss, d_bple, d_ln4g, d_ln4b) = _rowwise(
        "ple_head", head_fn, [(h3, 0, D), (zg, 0, D), (pp, 0, D), (target, 0, D)],
        [P["b_ple_gate"], P["ln4_g"], P["ln4_b"]], [(D, F32), (D, BF16), (D, BF16)],
        [(1, 1), (1, D), (1, D), (1, D)])

    grads, small = {}, {"b_ple_gate": d_bple, "ln4_g": d_ln4g, "ln4_b": d_ln4b}
    grads["w_ple_gate"] = _mm_tn("d_w_ple_gate", h3b, dzg, BF16)
    grads["w_ple"] = _mm_tn("d_w_ple", pb, dpp, BF16)
    dh3_b = _mm_nt("d_ple_gate_in", dzg, W["w_ple_gate"], F32)

    def ffn_bwd(tag, dy_parts, r, g, b, a, gu, hb_in, w_in, w_out):
        n_parts = len(dy_parts)

        def fn(*vals):
            dy = vals[0]
            for extra in vals[1:n_parts]:
                dy = dy + extra
            r, g, b = vals[n_parts:]
            dr, dg, db = _ln_bwd(r, g, b, dy)
            return (ALPHA * dr, 0.5 * dr), (dg, db)

        (dh_res, dfb), (dg, db) = _rowwise(f"{tag}_ln_bwd", fn, [(t, 0, D) for t in dy_parts] + [(r, 0, D)], [g, b],
                                           [(D, F32), (D, BF16)], [(1, D), (1, D)])
        dgu = _ffn_bwd_act(f"{tag}_act_bwd", dfb, w_out, gu)
        dgu8 = dgu.reshape(N_DEV, S, SHARD_FF)
        d_w_out = _ffn_d_w_out(f"d_{tag}_w_out", a, dfb)
        d_w_in = _ffn_d_w_in(f"d_{tag}_w_in", hb_in, dgu8)
        dh_ffn = _ffn_in_bwd(f"{tag}_in_bwd", dgu8, w_in)
        return dh_res, dh_ffn, d_w_in, d_w_out, dg, db

    dh2_a, dh2_b, grads["ffn2_w_in"], grads["ffn2_w_out"], small["ln3_g"], small["ln3_b"] = ffn_bwd(
        "ffn2", [dh3_a, dh3_b], r3, P["ln3_g"], P["ln3_b"], a2, gu2, h2b, W["ffn2_w_in"], W["ffn2_w_out"])

    def ln2_bwd_fn(d1, d2, r, g, b):
        dr, dg, db = _ln_bwd(r, g, b, d1 + d2)
        return (ALPHA * dr, dr), (dg, db)

    (dh1_a, dmixb), (small["ln2_g"], small["ln2_b"]) = _rowwise(
        "mix_ln_bwd", ln2_bwd_fn, [(dh2_a, 0, D), (dh2_b, 0, D), (r2, 0, D)], [P["ln2_g"], P["ln2_b"]],
        [(D, F32), (D, BF16)], [(1, D), (1, D)])
    grads["w_mix_out"] = _mm_tn("d_w_mix_out", merged, dmixb, BF16)
    dmerged = _mm_nt("mix_out_bwd", dmixb, W["w_mix_out"], F32)

    def merge_bwd_fn(dm, gs, gg, ys, yg, bs, bg):
        ss, sg = _sigmoid(gs + bs), _sigmoid(gg + bg)
        dgs = dm * ys * ss * (1.0 - ss)
        dgg = dm * yg * sg * (1.0 - sg)
        return (dgs, dgg, dm * ss, dm * sg), (jnp.sum(dgs, axis=0, keepdims=True), jnp.sum(dgg, axis=0, keepdims=True))

    (dgs, dgg, dy_sb, dy_gdn), (d_bs, d_bg) = _rowwise(
        "mix_merge_bwd", merge_bwd_fn, [(dmerged, 0, D), (gates, 0, D), (gates, 1, D), (y_sb, 0, D), (y_gdn, 0, D)],
        [bs, bg], [(D, BF16)] * 4, [(1, D), (1, D)])
    small["b_gate"] = jnp.concatenate([d_bs, d_bg], axis=1)

    grads["w_branch_sb"] = _mm_tn("d_w_branch_sb", attb, dy_sb, BF16)
    datt = _mm_nt("sb_out_bwd", dy_sb, W["w_branch_sb"], BF16)
    dsq, dsk, dsv = _sb_bwd(sbp, sb_tot_a, sb_tot_b, sb_visited, datt)

    grads["w_branch_gdn"] = _mm_tn("d_w_branch_gdn", gob, dy_gdn, BF16)
    dgo = _mm_nt("gdn_out_bwd", dy_gdn, W["w_branch_gdn"], F32)
    dqkv, dgz, dba, d_alog_x, d_dt_x, d_norm_x = _gdn_bwd(qkv, gdnp, ba, a_log_x, dt_x, norm_x, states, invs, dgo)
    small["a_log"] = jnp.sum(d_alog_x.reshape(N_HEADS, HEAD_DIM), axis=1).reshape(1, N_HEADS)
    small["dt_bias"] = jnp.sum(d_dt_x.reshape(N_HEADS, HEAD_DIM), axis=1).reshape(1, N_HEADS)
    small["gdn_norm_w"] = jnp.sum(d_norm_x.reshape(N_HEADS, HEAD_DIM), axis=0).reshape(1, HEAD_DIM)
    dconv_in, d_conv_w = _conv_bwd("conv_bwd", gdnp, conv_w, dqkv)
    grads["conv_w"] = d_conv_w[:CONV_K]

    dproj = jnp.concatenate([dsq.astype(BF16), dsk.astype(BF16), dsv.astype(BF16), dconv_in, dgz, dgs, dgg,
                             dba.astype(BF16)], axis=1)
    d_wmix = _mm_tn("d_w_mix_in", h1b, dproj, BF16)
    grads["w_mix_in"] = jnp.concatenate([d_wmix[:, :MIX_GATES], d_wmix[:, MIX_BA:MIX_BA + 2 * N_HEADS],
                                         d_wmix[:, MIX_GATES:MIX_BA]], axis=1)
    dh1_b = _mm_nt("mix_in_bwd", dproj, wmix, F32)

    dx_a, dx_b, grads["ffn1_w_in"], grads["ffn1_w_out"], small["ln1_g"], small["ln1_b"] = ffn_bwd(
        "ffn1", [dh1_a, dh1_b], r1, P["ln1_g"], P["ln1_b"], a1, gu1, xb, W["ffn1_w_in"], W["ffn1_w_out"])

    (grad_x,), _ = _rowwise("grad_x", lambda a, b: ((a + b,), ()), [(dx_a, 0, D), (dx_b, 0, D)], [], [(D, F32)])
    return loss[0, 0], grad_x, grads, small


def _coords():
    return lax.axis_index("x"), lax.axis_index("y"), lax.axis_index("c")


def _all_gather(name, shards):
    n = len(shards)

    def body(*refs):
        x_refs, out_refs = refs[:n], refs[n:2 * n]
        send_sems, recv_sems, local_sems = refs[2 * n:]
        x, y, c = _coords()
        me, sibling = (x, y, c), (x, y, 1 - c)
        chips = [(1 - x, y), (x, 1 - y), (1 - x, 1 - y)]

        def copy(a, k, block, to, src=None):
            px, py, pc = block
            dst = out_refs[a].at[4 * px + 2 * py + pc]
            return pltpu.make_async_remote_copy(
                src_ref=dst if src is None else src, dst_ref=dst,
                send_sem=send_sems.at[a, k], recv_sem=recv_sems.at[a, k], device_id=to, device_id_type=MESH)

        mine = [pltpu.make_async_copy(x_refs[a], out_refs[a].at[4 * x + 2 * y + c], local_sems.at[a]) for a in range(n)]
        for cp in mine:
            cp.start()
        first = []
        for a in range(n):
            first.append(copy(a, 0, me, sibling, src=x_refs[a]))
            first += [copy(a, 1 + j, me, (*chip, c), src=x_refs[a]) for j, chip in enumerate(chips)]
        for cp in first:
            cp.start()
        passed = []
        for j, chip in enumerate(chips):
            for a in range(n):
                copy(a, 1 + j, (*chip, c), me).wait_recv()
                fwd = copy(a, 4 + j, (*chip, c), sibling)
                fwd.start()
                passed.append(fwd)
        for a in range(n):
            copy(a, 0, sibling, me).wait_recv()
            for j, chip in enumerate(chips):
                copy(a, 4 + j, (*chip, 1 - c), me).wait_recv()
        for cp in first + passed:
            cp.wait_send()
        for cp in mine:
            cp.wait()

    any_spec = pl.BlockSpec(memory_space=pl.ANY)
    return pl.pallas_call(
        body, name=name, out_shape=[jax.ShapeDtypeStruct((N_DEV, *s.shape), s.dtype) for s in shards],
        in_specs=[any_spec] * n, out_specs=[any_spec] * n,
        scratch_shapes=[pltpu.SemaphoreType.DMA((n, 7)), pltpu.SemaphoreType.DMA((n, 7)), pltpu.SemaphoreType.DMA((n,))],
    )(*shards)


N_CHIPS = N_DEV // 2


def _sibling_exchange(name, srcs):
    n = len(srcs)

    def body(*refs):
        src_refs, dst_refs = refs[:n], refs[n:2 * n]
        send_sems, recv_sems = refs[2 * n:]
        x, y, c = _coords()
        copies = []
        for a in range(n):
            for k in range(N_CHIPS):
                copies.append(pltpu.make_async_remote_copy(
                    src_ref=src_refs[a].at[2 * k + (1 - c)], dst_ref=dst_refs[a].at[k], send_sem=send_sems.at[a, k],
                    recv_sem=recv_sems.at[a, k], device_id=(x, y, 1 - c), device_id_type=MESH))
        for cp in copies:
            cp.start()
        for cp in copies:
            cp.wait_recv()
        for cp in copies:
            cp.wait_send()

    any_spec = pl.BlockSpec(memory_space=pl.ANY)
    return pl.pallas_call(
        body, name=name, out_shape=[jax.ShapeDtypeStruct((N_CHIPS, *s.shape[1:]), s.dtype) for s in srcs],
        in_specs=[any_spec] * n, out_specs=[any_spec] * n,
        scratch_shapes=[pltpu.SemaphoreType.DMA((n, N_CHIPS)), pltpu.SemaphoreType.DMA((n, N_CHIPS))],
    )(*srcs)


def _pair_add(name, core, g, got):
    _, R, C = g.shape
    tr = _pick(R, (256, 176, 128, R))

    def body(c_ref, a_ref, b_ref, o_ref):
        o_ref[...] = (a_ref[...].astype(F32) + b_ref[...].astype(F32)).astype(o_ref.dtype)

    grid_spec = pltpu.PrefetchScalarGridSpec(
        num_scalar_prefetch=1, grid=(N_CHIPS, R // tr),
        in_specs=[pl.BlockSpec((None, None, tr, C), lambda k, i, c: (k, c[0], i, 0)),
                  pl.BlockSpec((None, tr, C), lambda k, i, c: (k, i, 0))],
        out_specs=pl.BlockSpec((None, tr, C), lambda k, i, c: (k, i, 0)))
    return pl.pallas_call(body, name=name, grid_spec=grid_spec, out_shape=jax.ShapeDtypeStruct((N_CHIPS, R, C), g.dtype),
                          )(core, g.reshape(N_CHIPS, 2, R, C), got)


def _chip_exchange(name, srcs):
    n = len(srcs)

    def body(*refs):
        src_refs, dst_refs = refs[:n], refs[n:2 * n]
        send_sems, recv_sems, local_sems = refs[2 * n:]
        x, y, c = _coords()
        me = 2 * x + y
        mine = [pltpu.make_async_copy(src_refs[a].at[me], dst_refs[a].at[me], local_sems.at[a]) for a in range(n)]
        for cp in mine:
            cp.start()
        sends, recvs = [], []
        for r in range(1, N_CHIPS):
            px = 1 - x if r & 2 else x
            py = 1 - y if r & 1 else y
            peer = 2 * px + py
            for a in range(n):
                sends.append(pltpu.make_async_remote_copy(
                    src_ref=src_refs[a].at[peer], dst_ref=dst_refs[a].at[me], send_sem=send_sems.at[a, r - 1],
                    recv_sem=recv_sems.at[a, r - 1], device_id=(px, py, c), device_id_type=MESH))
                recvs.append(pltpu.make_async_remote_copy(
                    src_ref=src_refs[a].at[me], dst_ref=dst_refs[a].at[peer], send_sem=send_sems.at[a, r - 1],
                    recv_sem=recv_sems.at[a, r - 1], device_id=(px, py, c), device_id_type=MESH))
        for cp in sends:
            cp.start()
        for cp in recvs:
            cp.wait_recv()
        for cp in sends:
            cp.wait_send()
        for cp in mine:
            cp.wait()

    any_spec = pl.BlockSpec(memory_space=pl.ANY)
    return pl.pallas_call(
        body, name=name, out_shape=[jax.ShapeDtypeStruct(s.shape, s.dtype) for s in srcs],
        in_specs=[any_spec] * n, out_specs=[any_spec] * n,
        scratch_shapes=[pltpu.SemaphoreType.DMA((n, N_CHIPS - 1)), pltpu.SemaphoreType.DMA((n, N_CHIPS - 1)),
                        pltpu.SemaphoreType.DMA((n,))],
    )(*srcs)


def _adamw(name, parts, w, m, v):
    n_parts = parts.shape[0]
    R, C = w.shape
    tr = _pick(R, (256, 176, 128, R))
    c1 = 1.0 - ADAM_B1 ** ADAM_STEP
    c2 = 1.0 - ADAM_B2 ** ADAM_STEP

    def body(p_ref, w_ref, m_ref, v_ref, g_ref, d_ref, nm_ref, nv_ref):
        g = p_ref[0].astype(F32)
        for d in range(1, n_parts):
            g = g + p_ref[d].astype(F32)
        nm = ADAM_B1 * m_ref[...] + (1.0 - ADAM_B1) * g
        nv = ADAM_B2 * v_ref[...] + (1.0 - ADAM_B2) * (g * g)
        g_ref[...] = g
        nm_ref[...] = nm
        nv_ref[...] = nv
        d_ref[...] = -ADAM_LR * ((nm / c1) / (jnp.sqrt(nv / c2) + ADAM_EPS) + ADAM_WD * w_ref[...])

    t = pl.BlockSpec((tr, C), lambda i: (i, 0))
    o = jax.ShapeDtypeStruct((R, C), F32)
    return pl.pallas_call(
        body, name=name, grid=(R // tr,),
        in_specs=[pl.BlockSpec((n_parts, tr, C), lambda i: (0, i, 0)), t, t, t],
        out_specs=[t, t, t, t], out_shape=[o, o, o, o],
        compiler_params=pltpu.CompilerParams(dimension_semantics=("parallel",)),
    )(parts, w, m, v)


def _pack_rows(flats, rows):
    cat = jnp.concatenate(flats, axis=-1)
    return jnp.pad(cat, [(0, rows * PACK_COLS - cat.shape[-1])]).reshape(rows, PACK_COLS)


def _col_shards(full):
    r, cdim = full.shape
    return full.reshape(r, N_DEV, cdim // N_DEV).transpose(1, 0, 2)


def _from_col_shards(sh):
    _, r, c = sh.shape
    return sh.transpose(1, 0, 2).reshape(r, N_DEV * c)


def kernel(x, p, ffn1_w_in, ffn1_w_out, ln1_g, ln1_b, w_mix_in, b_gate, conv_w, a_log, dt_bias, gdn_norm_w, w_branch_sb, w_branch_gdn, w_mix_out, ln2_g, ln2_b, ffn2_w_in, ffn2_w_out, ln3_g, ln3_b, w_ple_gate, b_ple_gate, w_ple, ln4_g, ln4_b, loss_target, m_ffn1_w_in, m_ffn1_w_out, m_ln1_g, m_ln1_b, m_w_mix_in, m_b_gate, m_conv_w, m_a_log, m_dt_bias, m_gdn_norm_w, m_w_branch_sb, m_w_branch_gdn, m_w_mix_out, m_ln2_g, m_ln2_b, m_ffn2_w_in, m_ffn2_w_out, m_ln3_g, m_ln3_b, m_w_ple_gate, m_b_ple_gate, m_w_ple, m_ln4_g, m_ln4_b, v_ffn1_w_in, v_ffn1_w_out, v_ln1_g, v_ln1_b, v_w_mix_in, v_b_gate, v_conv_w, v_a_log, v_dt_bias, v_gdn_norm_w, v_w_branch_sb, v_w_branch_gdn, v_w_mix_out, v_ln2_g, v_ln2_b, v_ffn2_w_in, v_ffn2_w_out, v_ln3_g, v_ln3_b, v_w_ple_gate, v_b_ple_gate, v_w_ple, v_ln4_g, v_ln4_b):
    given = dict(locals())
    w_loc = {n: given[n][0] for n in WEIGHTS}
    m_loc = {n: given["m_" + n][0] for n in WEIGHTS}
    v_loc = {n: given["v_" + n][0] for n in WEIGHTS}
    sizes = {n: w_loc[n].size for n in WEIGHTS}

    conv_hi = conv_w[0].astype(BF16)
    conv_lo = (conv_w[0] - conv_hi.astype(F32)).astype(BF16)
    gathered = _all_gather("gather_weights", [w_loc[n].astype(BF16) for n in BIG] + [conv_lo])
    W = dict(zip(BIG, gathered[:-1]))
    for n in ("ffn1_w_out", "ffn2_w_out"):
        W[n] = W[n].reshape(N_DEV // 2, SHARD_FF, D_MODEL)
    for n in ("w_mix_out", "w_ple_gate"):
        W[n] = W[n].reshape(D_MODEL, D_MODEL)
    for n in ("w_branch_sb", "w_branch_gdn", "w_ple"):
        W[n] = _from_col_shards(W[n])
    wm = _from_col_shards(W.pop("w_mix_in"))
    W["w_mix_pad"] = jnp.concatenate([wm[:, :7 * HEAD_W], wm[:, 7 * HEAD_W + 2 * N_HEADS:],
                                      wm[:, 7 * HEAD_W:7 * HEAD_W + 2 * N_HEADS],
                                      jnp.zeros((D_MODEL, 128 - 2 * N_HEADS), BF16)], axis=1)
    P = {n: w_loc[n].reshape(1, -1) for n in SMALL}
    P["conv_w"] = _from_col_shards(W.pop("conv_w").astype(F32) + gathered[-1].astype(F32))

    loss, grad_x, grads, small = _local_step(x[0], p[0, 0], loss_target[0], W, P)
    loss = lax.psum(loss, ("x", "y", "c"))

    for n in ("w_mix_in", "w_branch_sb", "w_branch_gdn", "w_ple"):
        grads[n] = _col_shards(grads[n])
    grads["conv_w"] = _col_shards(grads["conv_w"]).astype(BF16)
    send = [grads[n].reshape(N_DEV, *w_loc[n].shape) for n in BIG]
    core = lax.axis_index("c").astype(jnp.int32).reshape(1)
    got = _sibling_exchange("grads_to_sibling", send)
    sums = [_pair_add(f"pair_add_{n}", core, g, h) for n, g, h in zip(BIG, send, got, strict=True)]
    parts = dict(zip(BIG, _chip_exchange("scatter_grads", sums)))
    small_rows = 16
    (small_parts,) = _all_gather("gather_small_grads", [_pack_rows([small[n].reshape(-1) for n in SMALL], small_rows)])

    res = {n: _adamw(f"adamw_{n}", parts[n], w_loc[n], m_loc[n], v_loc[n]) for n in BIG}
    pack = lambda d: _pack_rows([d[n].reshape(-1) for n in SMALL], small_rows)
    small_out = _adamw("adamw_small", small_parts, pack(w_loc), pack(m_loc), pack(v_loc))
    off = 0
    for n in SMALL:
        res[n] = [o.reshape(-1)[off:off + sizes[n]] for o in small_out]
        off += sizes[n]
    outs = [[res[n][i].reshape(given[n].shape) for n in WEIGHTS] for i in range(4)]
    g_out, d_out, nm_out, nv_out = outs
    return (loss, grad_x[None], *g_out, *d_out, *nm_out, *nv_out)
```

```python
import functools

import jax
import jax.numpy as jnp
from jax import lax
from jax.experimental import pallas as pl
from jax.experimental.pallas import tpu as pltpu

F32 = jnp.float32
BF16 = jnp.bfloat16
MESH = pl.DeviceIdType.MESH
HI = lax.Precision.HIGHEST

N_DEV = 8
D_MODEL = 1024
D_FF = 2816
PLE_DIM = 256
N_HEADS = 8
HEAD_DIM = 64
HEAD_W = N_HEADS * HEAD_DIM
GDN_CHUNK = 128
CONV_K = 4
CONV_CH = 3 * HEAD_W
ALPHA = 2.0 ** 0.25
LN_EPS = 1e-5
RMS_EPS = 1e-6
MIX_SB = 0
MIX_GDN = 3 * HEAD_W
MIX_GATES = MIX_GDN + 4 * HEAD_W
MIX_BA = MIX_GATES + 2 * D_MODEL
MIX_PAD = MIX_BA + 128
N_IN = 7 * HEAD_W + 2 * N_HEADS + 2 * D_MODEL

ADAM_LR = 0.001
ADAM_B1 = 0.9
ADAM_B2 = 0.999
ADAM_EPS = 1e-08
ADAM_WD = 0.01
ADAM_STEP = 10

SB_TQ = 512
SB_TK = 256
SB_CUTOFF = -110.0
HALO = 8

BIG = ["ffn1_w_in", "ffn1_w_out", "w_mix_in", "conv_w", "w_branch_sb", "w_branch_gdn", "w_mix_out",
       "ffn2_w_in", "ffn2_w_out", "w_ple_gate", "w_ple"]
SMALL = ["ln1_g", "ln1_b", "b_gate", "a_log", "dt_bias", "gdn_norm_w", "ln2_g", "ln2_b", "ln3_g", "ln3_b",
         "b_ple_gate", "ln4_g", "ln4_b"]
WEIGHTS = ["ffn1_w_in", "ffn1_w_out", "ln1_g", "ln1_b", "w_mix_in", "b_gate", "conv_w", "a_log", "dt_bias",
           "gdn_norm_w", "w_branch_sb", "w_branch_gdn", "w_mix_out", "ln2_g", "ln2_b", "ffn2_w_in", "ffn2_w_out",
           "ln3_g", "ln3_b", "w_ple_gate", "b_ple_gate", "w_ple", "ln4_g", "ln4_b"]
PACK_COLS = 1024


def _pick(n, cands):
    for c in cands:
        if n % c == 0:
            return c
    raise ValueError(f"no tile for {n} in {cands}")


_NN = (((1,), (0,)), ((), ()))
_NT = (((1,), (1,)), ((), ()))
_TN = (((0,), (0,)), ((), ()))
SHARD_FF = 2 * D_FF // N_DEV
ROW_TILES = (1024, 512, 256, 128)


def _matmul(name, dims, a, b, a_spec, b_spec, out_sds, out_spec, out_block, grid, nk):
    def body(a_ref, b_ref, o_ref, acc_ref):
        k = pl.program_id(2)

        @pl.when(k == 0)
        def _():
            acc_ref[...] = jnp.zeros_like(acc_ref)

        acc_ref[...] += lax.dot_general(a_ref[...], b_ref[...], dims, preferred_element_type=F32)

        @pl.when(k == nk - 1)
        def _():
            o_ref[...] = acc_ref[...].astype(o_ref.dtype)

    return pl.pallas_call(
        body, name=name, grid=grid, in_specs=[a_spec, b_spec], out_specs=out_spec, out_shape=out_sds,
        scratch_shapes=[pltpu.VMEM(out_block, F32)],
        compiler_params=pltpu.CompilerParams(dimension_semantics=("parallel", "parallel", "arbitrary")),
    )(a, b)


def _mm(name, a, w, out_dtype, n=None, col_off=0):
    M, K = a.shape
    n = w.shape[1] if n is None else n
    tm = _pick(M, ROW_TILES)
    tn = _pick(n, (512, 384, 256, 128))
    tk = _pick(K, (1024, 512, 256))
    assert col_off % tn == 0
    cb, nk = col_off // tn, K // tk
    return _matmul(
        name, _NN, a, w, pl.BlockSpec((tm, tk), lambda i, j, k: (i, k)), pl.BlockSpec((tk, tn), lambda i, j, k: (k, j + cb)),
        jax.ShapeDtypeStruct((M, n), out_dtype), pl.BlockSpec((tm, tn), lambda i, j, k: (i, j)), (tm, tn),
        (M // tm, n // tn, nk), nk)


def _mm_nt(name, a, w, out_dtype):
    M, K = a.shape
    N = w.shape[0]
    tm = _pick(M, ROW_TILES)
    tn = _pick(N, (1024, 512, 256, 128))
    tk = _pick(K, (1024, 1152, 512, 256))
    nk = K // tk
    return _matmul(
        name, _NT, a, w, pl.BlockSpec((tm, tk), lambda i, j, k: (i, k)), pl.BlockSpec((tn, tk), lambda i, j, k: (j, k)),
        jax.ShapeDtypeStruct((M, N), out_dtype), pl.BlockSpec((tm, tn), lambda i, j, k: (i, j)), (tm, tn),
        (M // tm, N // tn, nk), nk)


def _mm_tn(name, a, b, out_dtype):
    S, M = a.shape
    N = b.shape[1]
    tm = _pick(M, (1024, 512, 256, 128))
    tn = _pick(N, (1024, 640, 512, 256, 128))
    tk = _pick(S, ROW_TILES)
    nk = S // tk
    return _matmul(
        name, _TN, a, b, pl.BlockSpec((tk, tm), lambda i, j, k: (k, i)), pl.BlockSpec((tk, tn), lambda i, j, k: (k, j)),
        jax.ShapeDtypeStruct((M, N), out_dtype), pl.BlockSpec((tm, tn), lambda i, j, k: (i, j)), (tm, tn),
        (M // tm, N // tn, nk), nk)


def _sigmoid(z):
    return 1.0 / (1.0 + jnp.exp(-z))


def _ffn_up(name, xb, w_in):
    S, D = xb.shape
    c = w_in.shape[2]
    ts = _pick(S, ROW_TILES)
    half = N_DEV // 2

    def body(x_ref, wg_ref, wu_ref, a_ref, gu_ref):
        x = x_ref[...]
        g = jnp.dot(x, wg_ref[...], preferred_element_type=F32)
        u = jnp.dot(x, wu_ref[...], preferred_element_type=F32)
        a_ref[...] = (g * _sigmoid(g) * u).astype(a_ref.dtype)
        gu_ref[0] = g.astype(gu_ref.dtype)
        gu_ref[1] = u.astype(gu_ref.dtype)

    return pl.pallas_call(
        body, name=name, grid=(S // ts, half),
        in_specs=[pl.BlockSpec((ts, D), lambda i, j: (i, 0)), pl.BlockSpec((None, D, c), lambda i, j: (j, 0, 0)),
                  pl.BlockSpec((None, D, c), lambda i, j: (j + half, 0, 0))],
        out_specs=[pl.BlockSpec((None, ts, c), lambda i, j: (j, i, 0)),
                   pl.BlockSpec((2, None, ts, c), lambda i, j: (0, j, i, 0))],
        out_shape=[jax.ShapeDtypeStruct((half, S, c), BF16), jax.ShapeDtypeStruct((2, half, S, c), BF16)],
        compiler_params=pltpu.CompilerParams(dimension_semantics=("parallel", "parallel")),
    )(xb, w_in, w_in)


def _ffn_down(name, a4, w_out4):
    n, S, c = a4.shape
    D = w_out4.shape[2]
    tm = _pick(S, ROW_TILES)
    tn = _pick(D, (1024, 512, 256, 128))
    return _matmul(
        name, _NN, a4, w_out4, pl.BlockSpec((None, tm, c), lambda i, j, k: (k, i, 0)),
        pl.BlockSpec((None, c, tn), lambda i, j, k: (k, 0, j)),
        jax.ShapeDtypeStruct((S, D), F32), pl.BlockSpec((tm, tn), lambda i, j, k: (i, j)), (tm, tn),
        (S // tm, D // tn, n), n)


def _ffn_bwd_act(name, dyb, w_out4, gu):
    S, D = dyb.shape
    n, c, _ = w_out4.shape
    ts = _pick(S, ROW_TILES)

    def body(dy_ref, w_ref, gu_ref, o_ref):
        da = lax.dot_general(dy_ref[...], w_ref[...], _NT, preferred_element_type=F32)
        g = gu_ref[0].astype(F32)
        u = gu_ref[1].astype(F32)
        sg = _sigmoid(g)
        o_ref[0] = (da * u * (sg * (1.0 + g * (1.0 - sg)))).astype(o_ref.dtype)
        o_ref[1] = (da * (g * sg)).astype(o_ref.dtype)

    blk = pl.BlockSpec((2, None, ts, c), lambda i, j: (0, j, i, 0))
    return pl.pallas_call(
        body, name=name, grid=(S // ts, n),
        in_specs=[pl.BlockSpec((ts, D), lambda i, j: (i, 0)), pl.BlockSpec((None, c, D), lambda i, j: (j, 0, 0)), blk],
        out_specs=blk, out_shape=jax.ShapeDtypeStruct((2, n, S, c), BF16),
        compiler_params=pltpu.CompilerParams(dimension_semantics=("parallel", "parallel")),
    )(dyb, w_out4, gu)


def _ffn_d_w_out(name, a4, dyb):
    n, S, c = a4.shape
    D = dyb.shape[1]
    tn = _pick(D, (1024, 512, 256, 128))
    tk = _pick(S, ROW_TILES)
    nk = S // tk
    return _matmul(
        name, _TN, a4, dyb, pl.BlockSpec((None, tk, c), lambda i, j, k: (i, k, 0)), pl.BlockSpec((tk, tn), lambda i, j, k: (k, j)),
        jax.ShapeDtypeStruct((n, c, D), BF16), pl.BlockSpec((None, c, tn), lambda i, j, k: (i, 0, j)), (c, tn),
        (n, D // tn, nk), nk)


def _ffn_d_w_in(name, hb, dgu8):
    S, D = hb.shape
    n, _, c = dgu8.shape
    tm = _pick(D, (1024, 512, 256, 128))
    tk = _pick(S, ROW_TILES)
    nk = S // tk
    return _matmul(
        name, _TN, hb, dgu8, pl.BlockSpec((tk, tm), lambda i, j, k: (k, j)), pl.BlockSpec((None, tk, c), lambda i, j, k: (i, k, 0)),
        jax.ShapeDtypeStruct((n, D, c), BF16), pl.BlockSpec((None, tm, c), lambda i, j, k: (i, j, 0)), (tm, c),
        (n, D // tm, nk), nk)


def _ffn_in_bwd(name, dgu8, w_in):
    n, S, c = dgu8.shape
    D = w_in.shape[1]
    tm = _pick(S, ROW_TILES)
    tn = _pick(D, (1024, 512, 256, 128))
    return _matmul(
        name, _NT, dgu8, w_in, pl.BlockSpec((None, tm, c), lambda i, j, k: (k, i, 0)),
        pl.BlockSpec((None, tn, c), lambda i, j, k: (k, j, 0)),
        jax.ShapeDtypeStruct((S, D), F32), pl.BlockSpec((tm, tn), lambda i, j, k: (i, j)), (tm, tn),
        (S // tm, D // tn, n), n)


def _rowwise(name, fn, tiled, params, outs, reds=(), ts=256):
    S = tiled[0][0].shape[0]
    ts = _pick(S, (ts, 128))
    n_t, n_p, n_o = len(tiled), len(params), len(outs)

    def body(*refs):
        vals = [r[...] for r in refs[:n_t + n_p]]
        res_o, res_r = fn(*vals)
        o_refs = refs[n_t + n_p:n_t + n_p + n_o]
        r_refs = refs[n_t + n_p + n_o:]
        for r, val in zip(o_refs, res_o, strict=True):
            r[...] = val.astype(r.dtype)
        if r_refs:
            i = pl.program_id(0)

            @pl.when(i == 0)
            def _():
                for r, val in zip(r_refs, res_r, strict=True):
                    r[...] = val.astype(F32)

            @pl.when(i > 0)
            def _():
                for r, val in zip(r_refs, res_r, strict=True):
                    r[...] += val.astype(F32)

    in_specs = [pl.BlockSpec((ts, w), functools.partial(lambda i, cb: (i, cb), cb=cb)) for _, cb, w in tiled]
    in_specs += [pl.BlockSpec(p.shape, lambda i: (0, 0)) for p in params]
    out_specs = [pl.BlockSpec((ts, w), lambda i: (i, 0)) for w, _ in outs]
    out_specs += [pl.BlockSpec(tuple(r), lambda i: (0, 0)) for r in reds]
    out_shape = [jax.ShapeDtypeStruct((S, w), dt) for w, dt in outs]
    out_shape += [jax.ShapeDtypeStruct(tuple(r), F32) for r in reds]
    res = pl.pallas_call(
        body, name=name, grid=(S // ts,), in_specs=in_specs, out_specs=out_specs, out_shape=out_shape,
        compiler_params=pltpu.CompilerParams(dimension_semantics=("arbitrary",)),
    )(*[t[0] for t in tiled], *params)
    return res[:n_o], res[n_o:]


def _ln(r, g, b):
    mu = jnp.mean(r, axis=-1, keepdims=True)
    xc = r - mu
    var = jnp.mean(xc * xc, axis=-1, keepdims=True)
    return xc * lax.rsqrt(var + LN_EPS) * g + b


def _ln_bwd(r, g, b, dy):
    _, vjp = jax.vjp(_ln, r, g, b)
    return vjp(dy)


def _shift_down(x, d):
    return x if d == 0 else pltpu.roll(x, d, 0)


def _conv_fwd(name, x, w):
    S = x.shape[0]
    ts = _pick(S, (256, 128))
    hb = ts // HALO

    def body(x_ref, prev_ref, w_ref, o_ref):
        i = pl.program_id(0)
        prev = jnp.where(i > 0, prev_ref[...], 0.0)
        xe = jnp.concatenate([prev, x_ref[...]], axis=0)
        y = jnp.zeros((ts + HALO, CONV_CH), F32)
        for j in range(CONV_K):
            y = y + w_ref[pl.ds(j, 1), :] * _shift_down(xe, CONV_K - 1 - j)
        y = y[HALO:, :]
        o_ref[...] = y * _sigmoid(y)

    return pl.pallas_call(
        body, name=name, grid=(S // ts,),
        in_specs=[pl.BlockSpec((ts, CONV_CH), lambda i: (i, 0)),
                  pl.BlockSpec((HALO, CONV_CH), lambda i: (jnp.maximum(i * hb - 1, 0), 0)),
                  pl.BlockSpec((CONV_K, CONV_CH), lambda i: (0, 0))],
        out_specs=pl.BlockSpec((ts, CONV_CH), lambda i: (i, 0)),
        out_shape=jax.ShapeDtypeStruct((S, CONV_CH), F32),
        compiler_params=pltpu.CompilerParams(dimension_semantics=("arbitrary",)),
    )(x, x, w)


def _conv_bwd(name, x, w, dout):
    S = x.shape[0]
    ts = _pick(S, (256, 128))
    hb = ts // HALO
    nt = S // ts
    n_ext = ts + 2 * HALO

    def body(x_ref, prev_ref, next_ref, w_ref, d_ref, dnext_ref, dx_ref, dw_ref):
        i = pl.program_id(0)
        prev = jnp.where(i > 0, prev_ref[...], 0.0)
        last = i == nt - 1
        nxt = jnp.where(last, 0.0, next_ref[...])
        dnxt = jnp.where(last, 0.0, dnext_ref[...])
        xe = jnp.concatenate([prev, x_ref[...], nxt], axis=0)
        de = jnp.concatenate([jnp.zeros((HALO, CONV_CH), F32), d_ref[...], dnxt], axis=0)
        y = jnp.zeros((n_ext, CONV_CH), F32)
        for j in range(CONV_K):
            y = y + w_ref[pl.ds(j, 1), :] * _shift_down(xe, CONV_K - 1 - j)
        sg = _sigmoid(y)
        dy = de * (sg * (1.0 + y * (1.0 - sg)))
        dx = jnp.zeros((n_ext, CONV_CH), F32)
        for j in range(CONV_K):
            m = CONV_K - 1 - j
            dx = dx + w_ref[pl.ds(j, 1), :] * (dy if m == 0 else pltpu.roll(dy, n_ext - m, 0))
        dx_ref[...] = dx[HALO:HALO + ts, :].astype(dx_ref.dtype)
        row = lax.broadcasted_iota(jnp.int32, (n_ext, 1), 0)
        dy_own = jnp.where((row >= HALO) & (row < HALO + ts), dy, 0.0)
        parts = [jnp.sum(dy_own * _shift_down(xe, CONV_K - 1 - j), axis=0, keepdims=True) for j in range(CONV_K)]
        dw = jnp.concatenate(parts + [jnp.zeros((HALO - CONV_K, CONV_CH), F32)], axis=0)

        @pl.when(i == 0)
        def _():
            dw_ref[...] = dw

        @pl.when(i > 0)
        def _():
            dw_ref[...] += dw

    tile = pl.BlockSpec((ts, CONV_CH), lambda i: (i, 0))
    prev = pl.BlockSpec((HALO, CONV_CH), lambda i: (jnp.maximum(i * hb - 1, 0), 0))
    nxt = pl.BlockSpec((HALO, CONV_CH), lambda i: (jnp.minimum((i + 1) * hb, nt * hb - 1), 0))
    return pl.pallas_call(
        body, name=name, grid=(nt,),
        in_specs=[tile, prev, nxt, pl.BlockSpec((CONV_K, CONV_CH), lambda i: (0, 0)), tile, nxt],
        out_specs=[tile, pl.BlockSpec((HALO, CONV_CH), lambda i: (0, 0))],
        out_shape=[jax.ShapeDtypeStruct((S, CONV_CH), BF16), jax.ShapeDtypeStruct((HALO, CONV_CH), F32)],
        compiler_params=pltpu.CompilerParams(dimension_semantics=("arbitrary",)),
    )(x, x, x, w, dout, dout)


def _softplus(z):
    return jnp.maximum(z, 0.0) + jnp.log(1.0 + jnp.exp(-jnp.abs(z)))


def _dot2(x, tri2):
    hi = x.astype(BF16)
    lo = (x - hi.astype(F32)).astype(BF16)
    return jnp.dot(jnp.concatenate([hi, lo], axis=1), tri2, preferred_element_type=F32)


def _pair_tri(tk, keep):
    r = lax.broadcasted_iota(jnp.int32, (4 * tk, 2 * tk), 0)
    c = lax.broadcasted_iota(jnp.int32, (4 * tk, 2 * tk), 1)
    same_head = ((r // tk) % 2) == (c // tk)
    return (same_head & keep(r % tk, c % tk)).astype(BF16)


def _pair_rows(x2, first):
    return jnp.concatenate([jnp.where(first, x2, 0), jnp.where(first, 0, x2)], axis=0)


def _pair_sum(x):
    h = x.shape[1] // 2
    return jnp.sum(x[:, :h], axis=1, keepdims=True), jnp.sum(x[:, h:], axis=1, keepdims=True)


def _sb_tiles(S):
    tq = _pick(S, (SB_TQ, 256, 128))
    return tq, SB_TK


N_PAIRS = N_HEADS // 2


def _sb_specs(S, tq):
    q = pl.BlockSpec((tq, 128), lambda p, i: (i, p))
    k = pl.BlockSpec((S, 128), lambda p, i: (0, N_PAIRS + p))
    v = pl.BlockSpec((S, 128), lambda p, i: (0, 2 * N_PAIRS + p))
    return q, k, v


def _scaled(q):
    return (q.astype(F32) * (HEAD_DIM ** -0.5)).astype(BF16)


def _sb_fwd(qkv):
    S = qkv.shape[0]
    tq, tk = _sb_tiles(S)
    nd = tq // tk

    def body(q_ref, k_ref, v_ref, o_ref, ta_ref, tb_ref, cnt_ref):
        qi = pl.program_id(1)
        qs = _scaled(q_ref[...])
        row = qi * tq + lax.broadcasted_iota(jnp.int32, (tq, 2 * tk), 0)
        col = lax.broadcasted_iota(jnp.int32, (tq, 2 * tk), 1)
        left = col < tk
        col = col % tk
        later = _pair_tri(tk, lambda r, c: r > c)
        first = lax.broadcasted_iota(jnp.int32, (tk, 128), 1) < HEAD_DIM

        def tile(j, carry, masked):
            acc, suf_a, suf_b = carry
            off = pl.multiple_of(j * tk, tk)
            kc = _pair_rows(k_ref[pl.ds(off, tk), :], first)
            vc = _pair_rows(v_ref[pl.ds(off, tk), :], first)
            mask = (col + j * tk) < row
            z = lax.dot_general(qs, kc, _NT, preferred_element_type=F32)
            lf = -_softplus(z)
            if masked:
                lf = jnp.where(mask, lf, 0.0)
            w = jnp.exp(z + lf + _dot2(lf, later) + jnp.where(left, suf_a, suf_b))
            if masked:
                w = jnp.where(mask, w, 0.0)
            acc = acc + jnp.dot(w.astype(BF16), vc, preferred_element_type=F32)
            sum_a, sum_b = _pair_sum(lf)
            return acc, suf_a + sum_a, suf_b + sum_b

        zero1 = jnp.zeros((tq, 1), F32)
        carry = (jnp.zeros((tq, 128), F32), zero1, zero1)
        n_full = qi * nd
        acc, suf_a, suf_b = lax.fori_loop(0, nd, lambda d, c: tile(n_full + nd - 1 - d, c, True), carry)

        def alive(sa, sb):
            return jnp.maximum(jnp.max(sa), jnp.max(sb)) > SB_CUTOFF

        def step(c):
            jj, acc, sa, sb, _ = c
            acc, sa, sb = tile(n_full - 1 - jj, (acc, sa, sb), False)
            return jj + 1, acc, sa, sb, alive(sa, sb)

        n_used, acc, suf_a, suf_b, _ = lax.while_loop(
            lambda c: (c[0] < n_full) & c[4], step, (0, acc, suf_a, suf_b, alive(suf_a, suf_b)))
        o_ref[...] = acc.astype(o_ref.dtype)
        ta_ref[...] = suf_a
        tb_ref[...] = suf_b
        cnt_ref[pl.program_id(0), qi] = n_used

    tot = pl.BlockSpec((None, tq, 1), lambda p, i: (p, i, 0))
    tshape = jax.ShapeDtypeStruct((N_PAIRS, S, 1), F32)
    return pl.pallas_call(
        body, name="sb_fwd", grid=(N_PAIRS, S // tq), in_specs=list(_sb_specs(S, tq)),
        out_specs=[pl.BlockSpec((tq, 128), lambda p, i: (i, p)), tot, tot, pl.BlockSpec(memory_space=pltpu.SMEM)],
        out_shape=[jax.ShapeDtypeStruct((S, HEAD_W), BF16), tshape, tshape,
                   jax.ShapeDtypeStruct((N_PAIRS, S // tq), jnp.int32)],
        compiler_params=pltpu.CompilerParams(dimension_semantics=("arbitrary", "arbitrary")),
    )(qkv, qkv, qkv)


def _sb_bwd(qkv, tot_a, tot_b, visited, do):
    S = qkv.shape[0]
    tq, tk = _sb_tiles(S)
    nd = tq // tk
    scale = HEAD_DIM ** -0.5

    def body(q_ref, k_ref, v_ref, ta_ref, tb_ref, cnt_ref, do_ref, dq_ref, dk_ref, dv_ref):
        qi = pl.program_id(1)

        @pl.when(qi == 0)
        def _():
            dk_ref[...] = jnp.zeros_like(dk_ref)
            dv_ref[...] = jnp.zeros_like(dv_ref)

        qs = _scaled(q_ref[...])
        do2 = do_ref[...]
        row = qi * tq + lax.broadcasted_iota(jnp.int32, (tq, 2 * tk), 0)
        col = lax.broadcasted_iota(jnp.int32, (tq, 2 * tk), 1)
        left = col < tk
        col = col % tk
        total = jnp.where(left, ta_ref[...], tb_ref[...])
        upto = _pair_tri(tk, lambda r, c: r <= c)
        first = lax.broadcasted_iota(jnp.int32, (tk, 128), 1) < HEAD_DIM

        def tile(j, carry, masked):
            dq, pl_a, pl_b, pg_a, pg_b = carry
            off = pl.multiple_of(j * tk, tk)
            kc = _pair_rows(k_ref[pl.ds(off, tk), :], first)
            vc = _pair_rows(v_ref[pl.ds(off, tk), :], first)
            mask = (col + j * tk) < row
            z = lax.dot_general(qs, kc, _NT, preferred_element_type=F32)
            lf = -_softplus(z)
            sig = jnp.exp(z + lf)
            if masked:
                lf = jnp.where(mask, lf, 0.0)
            w = jnp.exp(z + lf + (total - (jnp.where(left, pl_a, pl_b) + _dot2(lf, upto))))
            if masked:
                w = jnp.where(mask, w, 0.0)
            gl = lax.dot_general(do2, vc, _NT, preferred_element_type=F32) * w
            dz = gl - sig * (jnp.where(left, pg_a, pg_b) + _dot2(gl, upto))
            if masked:
                dz = jnp.where(mask, dz, 0.0)
            dzb = dz.astype(BF16)
            dq = dq + jnp.dot(dzb, kc, preferred_element_type=F32)
            dkc = lax.dot_general(dzb, qs, _TN, preferred_element_type=F32)
            dvc = lax.dot_general(w.astype(BF16), do2, _TN, preferred_element_type=F32)
            dk_ref[pl.ds(off, tk), :] += jnp.where(first, dkc[:tk], dkc[tk:])
            dv_ref[pl.ds(off, tk), :] += jnp.where(first, dvc[:tk], dvc[tk:])
            sl_a, sl_b = _pair_sum(lf)
            sg_a, sg_b = _pair_sum(gl)
            return dq, pl_a + sl_a, pl_b + sl_b, pg_a + sg_a, pg_b + sg_b

        zero1 = jnp.zeros((tq, 1), F32)
        carry = (jnp.zeros((tq, 128), F32), zero1, zero1, zero1, zero1)
        n_full = qi * nd
        n_used = cnt_ref[pl.program_id(0), qi]
        first_tile = n_full - n_used
        carry = lax.fori_loop(0, n_used, lambda t, c: tile(first_tile + t, c, False), carry)
        dq = lax.fori_loop(0, nd, lambda d, c: tile(n_full + d, c, True), carry)[0]
        dq_ref[...] = dq * scale

    q_spec, k_spec, v_spec = _sb_specs(S, tq)
    tot = pl.BlockSpec((None, tq, 1), lambda p, i: (p, i, 0))
    whole = pl.BlockSpec((S, 128), lambda p, i: (0, p))
    big = jax.ShapeDtypeStruct((S, HEAD_W), F32)
    return pl.pallas_call(
        body, name="sb_bwd", grid=(N_PAIRS, S // tq),
        in_specs=[q_spec, k_spec, v_spec, tot, tot, pl.BlockSpec(memory_space=pltpu.SMEM), q_spec],
        out_specs=[q_spec, whole, whole], out_shape=[big, big, big],
        compiler_params=pltpu.CompilerParams(dimension_semantics=("arbitrary", "arbitrary")),
    )(qkv, qkv, qkv, tot_a, tot_b, visited, do)


GDN_PRECISION = lax.Precision.HIGH


GDN_STATE_PRECISION = lax.Precision.DEFAULT


def _bmm(a, b, precision=GDN_PRECISION):
    return lax.dot_general(a, b, (((2,), (1,)), ((0,), (0,))), precision=precision, preferred_element_type=F32)


def _bmm_nt(a, b, precision=GDN_PRECISION):
    return lax.dot_general(a, b, (((2,), (2,)), ((0,), (0,))), precision=precision, preferred_element_type=F32)


def _bmm_tn(a, b, precision=GDN_PRECISION):
    return lax.dot_general(a, b, (((1,), (1,)), ((0,), (0,))), precision=precision, preferred_element_type=F32)


def _tri_inv(lower):
    C = lower.shape[-1]
    ii = lax.broadcasted_iota(jnp.int32, (C, C), 0)
    jj = lax.broadcasted_iota(jnp.int32, (C, C), 1)
    eye = (ii == jj).astype(F32)[None]
    xd = jnp.where((ii // 8 == jj // 8)[None], -lower, 0.0)
    x2 = _bmm(xd, xd)
    x4 = _bmm(x2, x2)
    inv = eye + xd
    inv = inv + _bmm(inv, x2)
    inv = inv + _bmm(inv, x4)
    b = 8
    while b < C:
        off = jnp.where(((ii // (2 * b) == jj // (2 * b)) & (ii // b != jj // b))[None], lower, 0.0)
        inv = inv - _bmm(inv, _bmm(off, inv))
        b *= 2
    return inv


@jax.custom_vjp
def _tri_solve(lower, inv, rhs):
    return _bmm(inv, rhs)


def _tri_solve_fwd(lower, inv, rhs):
    sol = _bmm(inv, rhs)
    return sol, (inv, sol)


def _tri_solve_bwd(res, dsol):
    inv, sol = res
    drhs = _bmm_tn(inv, dsol)
    C = inv.shape[-1]
    ii = lax.broadcasted_iota(jnp.int32, (C, C), 0)
    jj = lax.broadcasted_iota(jnp.int32, (C, C), 1)
    return jnp.where((jj < ii)[None], -_bmm_nt(drhs, sol), 0.0), jnp.zeros_like(inv), drhs


_tri_solve.defvjp(_tri_solve_fwd, _tri_solve_bwd)


def _pairs(x):
    return jnp.stack([x[:, 128 * p:128 * (p + 1)] for p in range(N_PAIRS)], axis=0)


def _unpairs(x):
    return jnp.concatenate([x[p] for p in range(N_PAIRS)], axis=1)


def _gdn_chunk(state, qkv, gz, ba, a_log_x, dt_x, norm_x, inv=None):
    C = qkv.shape[0]
    lane = lax.broadcasted_iota(jnp.int32, (1, 1, 128), 2)
    first = lane < HEAD_DIM

    def split_heads(x2):
        return jnp.stack([jnp.where(first, x2, 0.0), jnp.where(first, 0.0, x2)], axis=1).reshape(N_HEADS, *x2.shape[1:])

    def merge_heads(xh):
        x = xh.reshape(N_PAIRS, 2, *xh.shape[1:])
        return x[:, 0] + x[:, 1]

    def head_cols(x2):
        a = jnp.sum(jnp.where(lane == 0, x2, 0.0), axis=-1, keepdims=True)
        b = jnp.sum(jnp.where(lane == HEAD_DIM, x2, 0.0), axis=-1, keepdims=True)
        return jnp.stack([a, b], axis=1).reshape(N_HEADS, *a.shape[1:])

    def to_pair(xh):
        x = xh.reshape(N_PAIRS, 2, *xh.shape[1:])
        return jnp.where(first, x[:, 0], x[:, 1])

    def head_sums(x2):
        a = jnp.sum(jnp.where(first, x2, 0.0), axis=-1, keepdims=True)
        b = jnp.sum(jnp.where(first, 0.0, x2), axis=-1, keepdims=True)
        return jnp.where(first, a, b)

    er = lax.broadcasted_iota(jnp.int32, (128, 2 * HEAD_W), 0)
    ec = lax.broadcasted_iota(jnp.int32, (128, 2 * HEAD_W), 1)
    spread = (er == ec // HEAD_DIM).astype(F32)
    bx = lax.dot_general(ba, spread, (((1,), (0,)), ((), ())), precision=HI, preferred_element_type=F32)
    beta2 = _pairs(_sigmoid(bx[:, :HEAD_W]))
    g2 = _pairs(-jnp.exp(a_log_x) * _softplus(bx[:, HEAD_W:] + dt_x))
    beta = head_cols(beta2)
    g = head_cols(g2)

    q2, k2, v2 = (_pairs(qkv[:, i * HEAD_W:(i + 1) * HEAD_W]) for i in range(3))
    qn2 = q2 * lax.rsqrt(head_sums(q2 * q2) + RMS_EPS) * (HEAD_DIM ** -0.5)
    kn2 = k2 * lax.rsqrt(head_sums(k2 * k2) + RMS_EPS)
    knh = split_heads(kn2)

    ii = lax.broadcasted_iota(jnp.int32, (C, C), 0)[None]
    jj = lax.broadcasted_iota(jnp.int32, (C, C), 1)[None]
    incl = jj <= ii
    g_row = jnp.sum(jnp.where(ii == jj, g, 0.0), axis=1, keepdims=True)
    gc_col = jnp.sum(jnp.where(incl, g_row, 0.0), axis=2, keepdims=True)
    gc_row = jnp.sum(jnp.where(ii <= jj, g, 0.0), axis=1, keepdims=True)
    decay = jnp.where(incl, jnp.exp(jnp.where(incl, gc_col - gc_row, 0.0)), 0.0)
    lower = jnp.where(jj < ii, beta * _bmm_nt(knh, knh) * decay, 0.0)
    gc2 = to_pair(gc_col)
    egc2 = jnp.exp(gc2)
    if inv is None:
        inv = _tri_inv(lower)
    u2 = merge_heads(_tri_solve(lower, inv, split_heads(v2 * beta2)))
    w2 = merge_heads(_tri_solve(lower, inv, split_heads(kn2 * (beta2 * egc2))))
    lo = GDN_STATE_PRECISION
    qk = jnp.where(incl, _bmm_nt(jnp.repeat(qn2, 2, axis=0), knh, lo) * decay, 0.0)
    g_last2 = to_pair(jnp.sum(g, axis=1, keepdims=True))
    v_new2 = u2 - _bmm(w2, state, lo)
    o2 = _bmm(qn2 * egc2, state, lo) + merge_heads(_bmm(qk, split_heads(v_new2), lo))
    sr = lax.broadcasted_iota(jnp.int32, (128, 128), 0)
    sc = lax.broadcasted_iota(jnp.int32, (128, 128), 1)
    same_head = ((sr < HEAD_DIM) == (sc < HEAD_DIM))[None]
    new_state = state * jnp.exp(g_last2) + jnp.where(same_head, _bmm_tn(kn2 * jnp.exp(g_last2 - gc2), v_new2, lo), 0.0)
    o2 = o2 * lax.rsqrt(head_sums(o2 * o2) * (1.0 / HEAD_DIM) + RMS_EPS) * _pairs(norm_x)
    gz2 = _pairs(gz)
    return new_state, _unpairs(o2 * (gz2 * _sigmoid(gz2))), inv


def _gdn_specs(order):
    C = GDN_CHUNK
    par = pl.BlockSpec((1, HEAD_W), lambda c: (0, 0))
    return [pl.BlockSpec((C, CONV_CH), lambda c: (order(c), 0)), pl.BlockSpec((C, HEAD_W), lambda c: (order(c), 3)),
            pl.BlockSpec((C, 128), lambda c: (order(c), 0)), par, par, par]


def _gdn_fwd(qkv, gdnp, ba, a_log_x, dt_x, norm_x):
    S = qkv.shape[0]
    C = GDN_CHUNK
    n = S // C

    def body(qkv_ref, gz_ref, ba_ref, al_ref, dt_ref, nw_ref, o_ref, st_ref, inv_ref, state):
        @pl.when(pl.program_id(0) == 0)
        def _():
            state[...] = jnp.zeros_like(state)

        st = state[...]
        st_ref[...] = st
        new, o, inv = _gdn_chunk(st, qkv_ref[...], gz_ref[...], ba_ref[...], al_ref[...], dt_ref[...], nw_ref[...])
        state[...] = new
        o_ref[...] = o.astype(o_ref.dtype)
        inv_ref[...] = inv

    return pl.pallas_call(
        body, name="gdn_fwd", grid=(n,), in_specs=_gdn_specs(lambda c: c),
        out_specs=[pl.BlockSpec((C, HEAD_W), lambda c: (c, 0)),
                   pl.BlockSpec((None, N_PAIRS, 128, 128), lambda c: (c, 0, 0, 0)),
                   pl.BlockSpec((None, N_HEADS, C, C), lambda c: (c, 0, 0, 0))],
        out_shape=[jax.ShapeDtypeStruct((S, HEAD_W), BF16), jax.ShapeDtypeStruct((n, N_PAIRS, 128, 128), F32),
                   jax.ShapeDtypeStruct((n, N_HEADS, C, C), F32)],
        scratch_shapes=[pltpu.VMEM((N_PAIRS, 128, 128), F32)],
        compiler_params=pltpu.CompilerParams(dimension_semantics=("arbitrary",)),
    )(qkv, gdnp, ba, a_log_x, dt_x, norm_x)


def _gdn_bwd(qkv, gdnp, ba, a_log_x, dt_x, norm_x, states, invs, do):
    S = qkv.shape[0]
    C = GDN_CHUNK
    n = S // C

    def body(qkv_ref, gz_ref, ba_ref, al_ref, dt_ref, nw_ref, st_ref, inv_ref, do_ref,
             dqkv_ref, dgz_ref, dba_ref, dal_ref, ddt_ref, dnw_ref, dstate):
        @pl.when(pl.program_id(0) == 0)
        def _():
            dstate[...] = jnp.zeros_like(dstate)
            dal_ref[...] = jnp.zeros_like(dal_ref)
            ddt_ref[...] = jnp.zeros_like(ddt_ref)
            dnw_ref[...] = jnp.zeros_like(dnw_ref)

        args = (st_ref[...], qkv_ref[...], gz_ref[...], ba_ref[...], al_ref[...], dt_ref[...], nw_ref[...])
        inv = inv_ref[...]
        _, vjp = jax.vjp(lambda *a: _gdn_chunk(*a, inv=inv)[:2], *args)
        dst, dqkv, dgz, dba, dal, ddt, dnw = vjp((dstate[...], do_ref[...]))
        dstate[...] = dst
        dqkv_ref[...] = dqkv
        dgz_ref[...] = dgz.astype(dgz_ref.dtype)
        dba_ref[...] = dba
        dal_ref[...] += dal
        ddt_ref[...] += ddt
        dnw_ref[...] += dnw

    rev = lambda c: n - 1 - c
    par = pl.BlockSpec((1, HEAD_W), lambda c: (0, 0))
    f = jax.ShapeDtypeStruct
    return pl.pallas_call(
        body, name="gdn_bwd", grid=(n,),
        in_specs=_gdn_specs(rev) + [pl.BlockSpec((None, N_PAIRS, 128, 128), lambda c: (rev(c), 0, 0, 0)),
                                    pl.BlockSpec((None, N_HEADS, C, C), lambda c: (rev(c), 0, 0, 0)),
                                    pl.BlockSpec((C, HEAD_W), lambda c: (rev(c), 0))],
        out_specs=[pl.BlockSpec((C, CONV_CH), lambda c: (rev(c), 0)), pl.BlockSpec((C, HEAD_W), lambda c: (rev(c), 0)),
                   pl.BlockSpec((C, 128), lambda c: (rev(c), 0)), par, par, par],
        out_shape=[f((S, CONV_CH), F32), f((S, HEAD_W), BF16), f((S, 128), F32)] + [f((1, HEAD_W), F32)] * 3,
        scratch_shapes=[pltpu.VMEM((N_PAIRS, 128, 128), F32)],
        compiler_params=pltpu.CompilerParams(dimension_semantics=("arbitrary",)),
    )(qkv, gdnp, ba, a_log_x, dt_x, norm_x, states, invs, do)


def _local_step(x, p, target, W, P):
    S = x.shape[0]
    D = D_MODEL
    xb = x.astype(BF16)
    pb = p.astype(BF16)

    def ffn_fwd(tag, h, hb, w_in, w_out, g, b):
        a, gu = _ffn_up(f"{tag}_up", hb, w_in)
        f = _ffn_down(f"{tag}_down", a, w_out)

        def fn(h, f, g, b):
            r = ALPHA * h + 0.5 * f
            y = _ln(r, g, b)
            return (r, y, y), ()

        (r, y, yb), _ = _rowwise(f"{tag}_ln", fn, [(h, 0, D), (f, 0, D)], [g, b], [(D, F32), (D, F32), (D, BF16)])
        return a, gu, r, y, yb

    a1, gu1, r1, h1, h1b = ffn_fwd("ffn1", x, xb, W["ffn1_w_in"], W["ffn1_w_out"], P["ln1_g"], P["ln1_b"])

    wmix = W["w_mix_pad"]
    sbp = _mm("mix_sb", h1b, wmix, BF16, n=3 * HEAD_W, col_off=MIX_SB)
    gdnp = _mm("mix_gdn", h1b, wmix, F32, n=4 * HEAD_W, col_off=MIX_GDN)
    gates = _mm("mix_gates", h1b, wmix, F32, n=2 * D, col_off=MIX_GATES)
    ba = _mm("mix_ba", h1b, wmix, F32, n=128, col_off=MIX_BA)

    attb, sb_tot_a, sb_tot_b, sb_visited = _sb_fwd(sbp)
    y_sb = _mm("sb_out", attb, W["w_branch_sb"], F32)

    conv_w = P["conv_w"]
    qkv = _conv_fwd("conv_fwd", gdnp, conv_w)
    a_log_x = jnp.repeat(P["a_log"], HEAD_DIM, axis=1)
    dt_x = jnp.repeat(P["dt_bias"], HEAD_DIM, axis=1)
    norm_x = jnp.tile(P["gdn_norm_w"], (1, N_HEADS))
    gob, states, invs = _gdn_fwd(qkv, gdnp, ba, a_log_x, dt_x, norm_x)
    y_gdn = _mm("gdn_out", gob, W["w_branch_gdn"], F32)

    def merge_fn(gs, gg, ys, yg, bs, bg):
        return ((_sigmoid(gs + bs) * ys + _sigmoid(gg + bg) * yg),), ()

    b_gate = P["b_gate"]
    bs, bg = b_gate[:, :D], b_gate[:, D:]
    (merged,), _ = _rowwise("mix_merge", merge_fn, [(gates, 0, D), (gates, 1, D), (y_sb, 0, D), (y_gdn, 0, D)], [bs, bg],
                            [(D, BF16)])
    mix = _mm("mix_out", merged, W["w_mix_out"], F32)

    def ln2_fn(h, f, g, b):
        r = ALPHA * h + f
        y = _ln(r, g, b)
        return (r, y, y), ()

    (r2, h2, h2b), _ = _rowwise("mix_ln", ln2_fn, [(h1, 0, D), (mix, 0, D)], [P["ln2_g"], P["ln2_b"]],
                                [(D, F32), (D, F32), (D, BF16)])

    a2, gu2, r3, h3, h3b = ffn_fwd("ffn2", h2, h2b, W["ffn2_w_in"], W["ffn2_w_out"], P["ln3_g"], P["ln3_b"])

    zg = _mm("ple_gate", h3b, W["w_ple_gate"], F32)
    pp = _mm("ple_proj", pb, W["w_ple"], F32)

    def ple(h, zg, pp, bp, g, b):
        return _ln(ALPHA * h + _sigmoid(zg + bp) * pp, g, b)

    def head_fn(h, zg, pp, tgt, bp, g, b):
        y, vjp = jax.vjp(ple, h, zg, pp, bp, g, b)
        err = y - tgt
        dh, dzg, dpp, dbp, dg, db = vjp(err * (1.0 / D))
        loss = 0.5 * jnp.sum(jnp.sum(err * err, axis=1, keepdims=True), axis=0, keepdims=True) * (1.0 / D)
        return (dh, dzg, dpp), (loss, dbp, dg, db)

    (dh3_a, dzg, dpp), (loss, d_bple, d_ln4g, d_ln4b) = _rowwise(
        "ple_head", head_fn, [(h3, 0, D), (zg, 0, D), (pp, 0, D), (target, 0, D)],
        [P["b_ple_gate"], P["ln4_g"], P["ln4_b"]], [(D, F32), (D, BF16), (D, BF16)],
        [(1, 1), (1, D), (1, D), (1, D)])

    grads, small = {}, {"b_ple_gate": d_bple, "ln4_g": d_ln4g, "ln4_b": d_ln4b}
    grads["w_ple_gate"] = _mm_tn("d_w_ple_gate", h3b, dzg, BF16)
    grads["w_ple"] = _mm_tn("d_w_ple", pb, dpp, BF16)
    dh3_b = _mm_nt("d_ple_gate_in", dzg, W["w_ple_gate"], F32)

    def ffn_bwd(tag, dy_parts, r, g, b, a, gu, hb_in, w_in, w_out):
        n_parts = len(dy_parts)

        def fn(*vals):
            dy = vals[0]
            for extra in vals[1:n_parts]:
                dy = dy + extra
            r, g, b = vals[n_parts:]
            dr, dg, db = _ln_bwd(r, g, b, dy)
            return (ALPHA * dr, 0.5 * dr), (dg, db)

        (dh_res, dfb), (dg, db) = _rowwise(f"{tag}_ln_bwd", fn, [(t, 0, D) for t in dy_parts] + [(r, 0, D)], [g, b],
                                           [(D, F32), (D, BF16)], [(1, D), (1, D)])
        dgu = _ffn_bwd_act(f"{tag}_act_bwd", dfb, w_out, gu)
        dgu8 = dgu.reshape(N_DEV, S, SHARD_FF)
        d_w_out = _ffn_d_w_out(f"d_{tag}_w_out", a, dfb)
        d_w_in = _ffn_d_w_in(f"d_{tag}_w_in", hb_in, dgu8)
        dh_ffn = _ffn_in_bwd(f"{tag}_in_bwd", dgu8, w_in)
        return dh_res, dh_ffn, d_w_in, d_w_out, dg, db

    dh2_a, dh2_b, grads["ffn2_w_in"], grads["ffn2_w_out"], small["ln3_g"], small["ln3_b"] = ffn_bwd(
        "ffn2", [dh3_a, dh3_b], r3, P["ln3_g"], P["ln3_b"], a2, gu2, h2b, W["ffn2_w_in"], W["ffn2_w_out"])

    def ln2_bwd_fn(d1, d2, r, g, b):
        dr, dg, db = _ln_bwd(r, g, b, d1 + d2)
        return (ALPHA * dr, dr), (dg, db)

    (dh1_a, dmixb), (small["ln2_g"], small["ln2_b"]) = _rowwise(
        "mix_ln_bwd", ln2_bwd_fn, [(dh2_a, 0, D), (dh2_b, 0, D), (r2, 0, D)], [P["ln2_g"], P["ln2_b"]],
        [(D, F32), (D, BF16)], [(1, D), (1, D)])
    grads["w_mix_out"] = _mm_tn("d_w_mix_out", merged, dmixb, BF16)
    dmerged = _mm_nt("mix_out_bwd", dmixb, W["w_mix_out"], F32)

    def merge_bwd_fn(dm, gs, gg, ys, yg, bs, bg):
        ss, sg = _sigmoid(gs + bs), _sigmoid(gg + bg)
        dgs = dm * ys * ss * (1.0 - ss)
        dgg = dm * yg * sg * (1.0 - sg)
        return (dgs, dgg, dm * ss, dm * sg), (jnp.sum(dgs, axis=0, keepdims=True), jnp.sum(dgg, axis=0, keepdims=True))

    (dgs, dgg, dy_sb, dy_gdn), (d_bs, d_bg) = _rowwise(
        "mix_merge_bwd", merge_bwd_fn, [(dmerged, 0, D), (gates, 0, D), (gates, 1, D), (y_sb, 0, D), (y_gdn, 0, D)],
        [bs, bg], [(D, BF16)] * 4, [(1, D), (1, D)])
    small["b_gate"] = jnp.concatenate([d_bs, d_bg], axis=1)

    grads["w_branch_sb"] = _mm_tn("d_w_branch_sb", attb, dy_sb, BF16)
    datt = _mm_nt("sb_out_bwd", dy_sb, W["w_branch_sb"], BF16)
    dsq, dsk, dsv = _sb_bwd(sbp, sb_tot_a, sb_tot_b, sb_visited, datt)

    grads["w_branch_gdn"] = _mm_tn("d_w_branch_gdn", gob, dy_gdn, BF16)
    dgo = _mm_nt("gdn_out_bwd", dy_gdn, W["w_branch_gdn"], F32)
    dqkv, dgz, dba, d_alog_x, d_dt_x, d_norm_x = _gdn_bwd(qkv, gdnp, ba, a_log_x, dt_x, norm_x, states, invs, dgo)
    small["a_log"] = jnp.sum(d_alog_x.reshape(N_HEADS, HEAD_DIM), axis=1).reshape(1, N_HEADS)
    small["dt_bias"] = jnp.sum(d_dt_x.reshape(N_HEADS, HEAD_DIM), axis=1).reshape(1, N_HEADS)
    small["gdn_norm_w"] = jnp.sum(d_norm_x.reshape(N_HEADS, HEAD_DIM), axis=0).reshape(1, HEAD_DIM)
    dconv_in, d_conv_w = _conv_bwd("conv_bwd", gdnp, conv_w, dqkv)
    grads["conv_w"] = d_conv_w[:CONV_K]

    dproj = jnp.concatenate([dsq.astype(BF16), dsk.astype(BF16), dsv.astype(BF16), dconv_in, dgz, dgs, dgg,
                             dba.astype(BF16)], axis=1)
    d_wmix = _mm_tn("d_w_mix_in", h1b, dproj, BF16)
    grads["w_mix_in"] = jnp.concatenate([d_wmix[:, :MIX_GATES], d_wmix[:, MIX_BA:MIX_BA + 2 * N_HEADS],
                                         d_wmix[:, MIX_GATES:MIX_BA]], axis=1)
    dh1_b = _mm_nt("mix_in_bwd", dproj, wmix, F32)

    dx_a, dx_b, grads["ffn1_w_in"], grads["ffn1_w_out"], small["ln1_g"], small["ln1_b"] = ffn_bwd(
        "ffn1", [dh1_a, dh1_b], r1, P["ln1_g"], P["ln1_b"], a1, gu1, xb, W["ffn1_w_in"], W["ffn1_w_out"])

    (grad_x,), _ = _rowwise("grad_x", lambda a, b: ((a + b,), ()), [(dx_a, 0, D), (dx_b, 0, D)], [], [(D, F32)])
    return loss[0, 0], grad_x, grads, small


def _coords():
    return lax.axis_index("x"), lax.axis_index("y"), lax.axis_index("c")


def _all_gather(name, shards):
    n = len(shards)

    def body(*refs):
        x_refs, out_refs = refs[:n], refs[n:2 * n]
        send_sems, recv_sems, local_sems = refs[2 * n:]
        x, y, c = _coords()
        me, sibling = (x, y, c), (x, y, 1 - c)
        chips = [(1 - x, y), (x, 1 - y), (1 - x, 1 - y)]

        def copy(a, k, block, to, src=None):
            px, py, pc = block
            dst = out_refs[a].at[4 * px + 2 * py + pc]
            return pltpu.make_async_remote_copy(
                src_ref=dst if src is None else src, dst_ref=dst,
                send_sem=send_sems.at[a, k], recv_sem=recv_sems.at[a, k], device_id=to, device_id_type=MESH)

        mine = [pltpu.make_async_copy(x_refs[a], out_refs[a].at[4 * x + 2 * y + c], local_sems.at[a]) for a in range(n)]
        for cp in mine:
            cp.start()
        first = []
        for a in range(n):
            first.append(copy(a, 0, me, sibling, src=x_refs[a]))
            first += [copy(a, 1 + j, me, (*chip, c), src=x_refs[a]) for j, chip in enumerate(chips)]
        for cp in first:
            cp.start()
        passed = []
        for j, chip in enumerate(chips):
            for a in range(n):
                copy(a, 1 + j, (*chip, c), me).wait_recv()
                fwd = copy(a, 4 + j, (*chip, c), sibling)
                fwd.start()
                passed.append(fwd)
        for a in range(n):
            copy(a, 0, sibling, me).wait_recv()
            for j, chip in enumerate(chips):
                copy(a, 4 + j, (*chip, 1 - c), me).wait_recv()
        for cp in first + passed:
            cp.wait_send()
        for cp in mine:
            cp.wait()

    any_spec = pl.BlockSpec(memory_space=pl.ANY)
    return pl.pallas_call(
        body, name=name, out_shape=[jax.ShapeDtypeStruct((N_DEV, *s.shape), s.dtype) for s in shards],
        in_specs=[any_spec] * n, out_specs=[any_spec] * n,
        scratch_shapes=[pltpu.SemaphoreType.DMA((n, 7)), pltpu.SemaphoreType.DMA((n, 7)), pltpu.SemaphoreType.DMA((n,))],
    )(*shards)


N_CHIPS = N_DEV // 2


def _sibling_exchange(name, srcs):
    n = len(srcs)

    def body(*refs):
        src_refs, dst_refs = refs[:n], refs[n:2 * n]
        send_sems, recv_sems = refs[2 * n:]
        x, y, c = _coords()
        copies = []
        for a in range(n):
            for k in range(N_CHIPS):
                copies.append(pltpu.make_async_remote_copy(
                    src_ref=src_refs[a].at[2 * k + (1 - c)], dst_ref=dst_refs[a].at[k], send_sem=send_sems.at[a, k],
                    recv_sem=recv_sems.at[a, k], device_id=(x, y, 1 - c), device_id_type=MESH))
        for cp in copies:
            cp.start()
        for cp in copies:
            cp.wait_recv()
        for cp in copies:
            cp.wait_send()

    any_spec = pl.BlockSpec(memory_space=pl.ANY)
    return pl.pallas_call(
        body, name=name, out_shape=[jax.ShapeDtypeStruct((N_CHIPS, *s.shape[1:]), s.dtype) for s in srcs],
        in_specs=[any_spec] * n, out_specs=[any_spec] * n,
        scratch_shapes=[pltpu.SemaphoreType.DMA((n, N_CHIPS)), pltpu.SemaphoreType.DMA((n, N_CHIPS))],
    )(*srcs)


def _pair_add(name, core, g, got):
    _, R, C = g.shape
    tr = _pick(R, (256, 176, 128, R))

    def body(c_ref, a_ref, b_ref, o_ref):
        o_ref[...] = (a_ref[...].astype(F32) + b_ref[...].astype(F32)).astype(o_ref.dtype)

    grid_spec = pltpu.PrefetchScalarGridSpec(
        num_scalar_prefetch=1, grid=(N_CHIPS, R // tr),
        in_specs=[pl.BlockSpec((None, None, tr, C), lambda k, i, c: (k, c[0], i, 0)),
                  pl.BlockSpec((None, tr, C), lambda k, i, c: (k, i, 0))],
        out_specs=pl.BlockSpec((None, tr, C), lambda k, i, c: (k, i, 0)))
    return pl.pallas_call(body, name=name, grid_spec=grid_spec, out_shape=jax.ShapeDtypeStruct((N_CHIPS, R, C), g.dtype),
                          )(core, g.reshape(N_CHIPS, 2, R, C), got)


def _chip_exchange(name, srcs):
    n = len(srcs)

    def body(*refs):
        src_refs, dst_refs = refs[:n], refs[n:2 * n]
        send_sems, recv_sems, local_sems = refs[2 * n:]
        x, y, c = _coords()
        me = 2 * x + y
        mine = [pltpu.make_async_copy(src_refs[a].at[me], dst_refs[a].at[me], local_sems.at[a]) for a in range(n)]
        for cp in mine:
            cp.start()
        sends, recvs = [], []
        for r in range(1, N_CHIPS):
            px = 1 - x if r & 2 else x
            py = 1 - y if r & 1 else y
            peer = 2 * px + py
            for a in range(n):
                sends.append(pltpu.make_async_remote_copy(
                    src_ref=src_refs[a].at[peer], dst_ref=dst_refs[a].at[me], send_sem=send_sems.at[a, r - 1],
                    recv_sem=recv_sems.at[a, r - 1], device_id=(px, py, c), device_id_type=MESH))
                recvs.append(pltpu.make_async_remote_copy(
                    src_ref=src_refs[a].at[me], dst_ref=dst_refs[a].at[peer], send_sem=send_sems.at[a, r - 1],
                    recv_sem=recv_sems.at[a, r - 1], device_id=(px, py, c), device_id_type=MESH))
        for cp in sends:
            cp.start()
        for cp in recvs:
            cp.wait_recv()
        for cp in sends:
            cp.wait_send()
        for cp in mine:
            cp.wait()

    any_spec = pl.BlockSpec(memory_space=pl.ANY)
    return pl.pallas_call(
        body, name=name, out_shape=[jax.ShapeDtypeStruct(s.shape, s.dtype) for s in srcs],
        in_specs=[any_spec] * n, out_specs=[any_spec] * n,
        scratch_shapes=[pltpu.SemaphoreType.DMA((n, N_CHIPS - 1)), pltpu.SemaphoreType.DMA((n, N_CHIPS - 1)),
                        pltpu.SemaphoreType.DMA((n,))],
    )(*srcs)


def _adamw(name, parts, w, m, v):
    n_parts = parts.shape[0]
    R, C = w.shape
    tr = _pick(R, (256, 176, 128, R))
    c1 = 1.0 - ADAM_B1 ** ADAM_STEP
    c2 = 1.0 - ADAM_B2 ** ADAM_STEP

    def body(p_ref, w_ref, m_ref, v_ref, g_ref, d_ref, nm_ref, nv_ref):
        g = p_ref[0].astype(F32)
        for d in range(1, n_parts):
            g = g + p_ref[d].astype(F32)
        nm = ADAM_B1 * m_ref[...] + (1.0 - ADAM_B1) * g
        nv = ADAM_B2 * v_ref[...] + (1.0 - ADAM_B2) * (g * g)
        g_ref[...] = g
        nm_ref[...] = nm
        nv_ref[...] = nv
        d_ref[...] = -ADAM_LR * ((nm / c1) / (jnp.sqrt(nv / c2) + ADAM_EPS) + ADAM_WD * w_ref[...])

    t = pl.BlockSpec((tr, C), lambda i: (i, 0))
    o = jax.ShapeDtypeStruct((R, C), F32)
    return pl.pallas_call(
        body, name=name, grid=(R // tr,),
        in_specs=[pl.BlockSpec((n_parts, tr, C), lambda i: (0, i, 0)), t, t, t],
        out_specs=[t, t, t, t], out_shape=[o, o, o, o],
        compiler_params=pltpu.CompilerParams(dimension_semantics=("parallel",)),
    )(parts, w, m, v)


def _pack_rows(flats, rows):
    cat = jnp.concatenate(flats, axis=-1)
    return jnp.pad(cat, [(0, rows * PACK_COLS - cat.shape[-1])]).reshape(rows, PACK_COLS)


def _col_shards(full):
    r, cdim = full.shape
    return full.reshape(r, N_DEV, cdim // N_DEV).transpose(1, 0, 2)


def _from_col_shards(sh):
    _, r, c = sh.shape
    return sh.transpose(1, 0, 2).reshape(r, N_DEV * c)


def kernel(x, p, ffn1_w_in, ffn1_w_out, ln1_g, ln1_b, w_mix_in, b_gate, conv_w, a_log, dt_bias, gdn_norm_w, w_branch_sb, w_branch_gdn, w_mix_out, ln2_g, ln2_b, ffn2_w_in, ffn2_w_out, ln3_g, ln3_b, w_ple_gate, b_ple_gate, w_ple, ln4_g, ln4_b, loss_target, m_ffn1_w_in, m_ffn1_w_out, m_ln1_g, m_ln1_b, m_w_mix_in, m_b_gate, m_conv_w, m_a_log, m_dt_bias, m_gdn_norm_w, m_w_branch_sb, m_w_branch_gdn, m_w_mix_out, m_ln2_g, m_ln2_b, m_ffn2_w_in, m_ffn2_w_out, m_ln3_g, m_ln3_b, m_w_ple_gate, m_b_ple_gate, m_w_ple, m_ln4_g, m_ln4_b, v_ffn1_w_in, v_ffn1_w_out, v_ln1_g, v_ln1_b, v_w_mix_in, v_b_gate, v_conv_w, v_a_log, v_dt_bias, v_gdn_norm_w, v_w_branch_sb, v_w_branch_gdn, v_w_mix_out, v_ln2_g, v_ln2_b, v_ffn2_w_in, v_ffn2_w_out, v_ln3_g, v_ln3_b, v_w_ple_gate, v_b_ple_gate, v_w_ple, v_ln4_g, v_ln4_b):
    given = dict(locals())
    w_loc = {n: given[n][0] for n in WEIGHTS}
    m_loc = {n: given["m_" + n][0] for n in WEIGHTS}
    v_loc = {n: given["v_" + n][0] for n in WEIGHTS}
    sizes = {n: w_loc[n].size for n in WEIGHTS}

    conv_hi = conv_w[0].astype(BF16)
    conv_lo = (conv_w[0] - conv_hi.astype(F32)).astype(BF16)
    gathered = _all_gather("gather_weights", [w_loc[n].astype(BF16) for n in BIG] + [conv_lo])
    W = dict(zip(BIG, gathered[:-1]))
    for n in ("ffn1_w_out", "ffn2_w_out"):
        W[n] = W[n].reshape(N_DEV // 2, SHARD_FF, D_MODEL)
    for n in ("w_mix_out", "w_ple_gate"):
        W[n] = W[n].reshape(D_MODEL, D_MODEL)
    for n in ("w_branch_sb", "w_branch_gdn", "w_ple"):
        W[n] = _from_col_shards(W[n])
    wm = _from_col_shards(W.pop("w_mix_in"))
    W["w_mix_pad"] = jnp.concatenate([wm[:, :7 * HEAD_W], wm[:, 7 * HEAD_W + 2 * N_HEADS:],
                                      wm[:, 7 * HEAD_W:7 * HEAD_W + 2 * N_HEADS],
                                      jnp.zeros((D_MODEL, 128 - 2 * N_HEADS), BF16)], axis=1)
    P = {n: w_loc[n].reshape(1, -1) for n in SMALL}
    P["conv_w"] = _from_col_shards(W.pop("conv_w").astype(F32) + gathered[-1].astype(F32))

    loss, grad_x, grads, small = _local_step(x[0], p[0, 0], loss_target[0], W, P)
    loss = lax.psum(loss, ("x", "y", "c"))

    for n in ("w_mix_in", "w_branch_sb", "w_branch_gdn", "w_ple"):
        grads[n] = _col_shards(grads[n])
    grads["conv_w"] = _col_shards(grads["conv_w"]).astype(BF16)
    send = [grads[n].reshape(N_DEV, *w_loc[n].shape) for n in BIG]
    core = lax.axis_index("c").astype(jnp.int32).reshape(1)
    got = _sibling_exchange("grads_to_sibling", send)
    sums = [_pair_add(f"pair_add_{n}", core, g, h) for n, g, h in zip(BIG, send, got, strict=True)]
    parts = dict(zip(BIG, _chip_exchange("scatter_grads", sums)))
    small_rows = 16
    (small_parts,) = _all_gather("gather_small_grads", [_pack_rows([small[n].reshape(-1) for n in SMALL], small_rows)])

    res = {n: _adamw(f"adamw_{n}", parts[n], w_loc[n], m_loc[n], v_loc[n]) for n in BIG}
    pack = lambda d: _pack_rows([d[n].reshape(-1) for n in SMALL], small_rows)
    small_out = _adamw("adamw_small", small_parts, pack(w_loc), pack(m_loc), pack(v_loc))
    off = 0
    for n in SMALL:
        res[n] = [o.reshape(-1)[off:off + sizes[n]] for o in small_out]
        off += sizes[n]
    outs = [[res[n][i].reshape(given[n].shape) for n in WEIGHTS] for i in range(4)]
    g_out, d_out, nm_out, nv_out = outs
    return (loss, grad_x[None], *g_out, *d_out, *nm_out, *nv_out)
```
